```python
import math
import jax, jax.numpy as jnp
from jax import lax
import numpy as np

D_MODEL = 1024
BATCH = 16
SEQ = 2048
DEPTH = 2

N_A_LAYERS = DEPTH // 2
N_B_LAYERS = DEPTH - N_A_LAYERS
N_DENSE_FFN = (DEPTH + 1) // 2
N_MOE_FFN = DEPTH // 2

S5_GROUP = 16
S5_GROUPS = D_MODEL // S5_GROUP
S5_STATE = 64
S5_DT_MIN = 1e-3
S5_DT_MAX = 1e-1

MLA_HEADS = 8
QK_NOPE = 128
QK_ROPE = 64
V_DIM = 128
Q_LORA = 256
KV_LORA = 128
ROPE_THETA = 10000.0
Q_BLOCK = 128

FFN_DIM = 2816
N_EXPERTS = 8
TOP_K = 2
EXPERT_DIM = 3584
ROUTE_BLOCK = 128

ALPHA = (2.0 * DEPTH) ** 0.25
BETA = (8.0 * DEPTH) ** -0.25
LN_EPS = 1e-5
RMS_EPS = 1e-6

kernel_name = 'yoco_s5_mla_moe_deepnorm'


def _layer_norm(x, g, b):
    xf = x.astype(jnp.float32)
    mu = jnp.mean(xf, axis=-1, keepdims=True)
    var = jnp.mean(jnp.square(xf - mu), axis=-1, keepdims=True)
    y = (xf - mu) * lax.rsqrt(var + LN_EPS) * g.astype(jnp.float32) + b.astype(jnp.float32)
    return y.astype(x.dtype)


def _rms_norm(x, g):
    xf = x.astype(jnp.float32)
    y = xf * lax.rsqrt(jnp.mean(jnp.square(xf), axis=-1, keepdims=True) + RMS_EPS) * g.astype(jnp.float32)
    return y.astype(x.dtype)


def _rope_tables(seq_len):
    pos = jnp.arange(seq_len, dtype=jnp.float32)
    inv_freq = ROPE_THETA ** (-jnp.arange(0, QK_ROPE, 2, dtype=jnp.float32) / QK_ROPE)
    ang = pos[:, None] * inv_freq[None, :]
    return jnp.cos(ang), jnp.sin(ang)


def _apply_rope(t, cos, sin):
    half = t.shape[-1] // 2
    t1 = t[..., :half].astype(jnp.float32)
    t2 = t[..., half:].astype(jnp.float32)
    out = jnp.concatenate([t1 * cos - t2 * sin, t1 * sin + t2 * cos], axis=-1)
    return out.astype(t.dtype)


def _s5_combine(left, right):
    a_re_l, a_im_l, b_re_l, b_im_l = left
    a_re_r, a_im_r, b_re_r, b_im_r = right
    return (a_re_r * a_re_l - a_im_r * a_im_l,
            a_re_r * a_im_l + a_im_r * a_re_l,
            a_re_r * b_re_l - a_im_r * b_im_l + b_re_r,
            a_re_r * b_im_l + a_im_r * b_re_l + b_im_r)


def _s5_mixer(u, lam_re, lam_im, log_step, b_re, b_im, c_re, c_im, d_skip, w_glu):
    bsz, seq, _ = u.shape
    f32 = jnp.float32
    uf = u.astype(f32)
    lr = lam_re.astype(f32)
    li = lam_im.astype(f32)
    dt = jnp.exp(log_step.astype(f32))[:, None]
    mag = jnp.exp(lr * dt)
    lb_re = mag * jnp.cos(li * dt)
    lb_im = mag * jnp.sin(li * dt)
    den = lr * lr + li * li
    f_re = ((lb_re - 1.0) * lr + lb_im * li) / den
    f_im = (lb_im * lr - (lb_re - 1.0) * li) / den
    br = b_re.astype(f32)
    bi = b_im.astype(f32)
    bb_re = f_re[..., None] * br - f_im[..., None] * bi
    bb_im = f_re[..., None] * bi + f_im[..., None] * br
    ug = uf.reshape(bsz, seq, S5_GROUPS, S5_GROUP)
    bu_re = jnp.einsum('bsgc,gpc->sbgp', ug, bb_re)
    bu_im = jnp.einsum('bsgc,gpc->sbgp', ug, bb_im)
    a_re = jnp.broadcast_to(lb_re, (seq, 1) + lb_re.shape)
    a_im = jnp.broadcast_to(lb_im, (seq, 1) + lb_im.shape)
    _, _, s_re, s_im = lax.associative_scan(_s5_combine, (a_re, a_im, bu_re, bu_im), axis=0)
    y = (jnp.einsum('sbgp,gcp->bsgc', s_re, c_re.astype(f32))
         - jnp.einsum('sbgp,gcp->bsgc', s_im, c_im.astype(f32)))
    y = y.reshape(bsz, seq, D_MODEL) + d_skip.astype(f32) * uf
    act = jax.nn.gelu(y).astype(u.dtype)
    val, gate = jnp.split(act @ w_glu, 2, axis=-1)
    return val * jax.nn.sigmoid(gate)


def _mla_shared_kv(h, kv_w_a, kv_norm, kv_w_b, cos, sin):
    bsz, seq, _ = h.shape
    kv_a = h @ kv_w_a
    c_kv = _rms_norm(kv_a[..., :KV_LORA], kv_norm)
    k_rope = _apply_rope(kv_a[..., KV_LORA:], cos[None], sin[None])
    kv = (c_kv @ kv_w_b).reshape(bsz, seq, MLA_HEADS, QK_NOPE + V_DIM)
    return kv[..., :QK_NOPE], k_rope, kv[..., QK_NOPE:]


def _mla_attention(x, q_w_a, q_norm, q_w_b, o_w, k_nope, k_rope, v, cos, sin):
    bsz, seq, _ = x.shape
    c_q = _rms_norm(x @ q_w_a, q_norm)
    q = (c_q @ q_w_b).reshape(bsz, seq, MLA_HEADS, QK_NOPE + QK_ROPE)
    q_nope = q[..., :QK_NOPE]
    q_rope = _apply_rope(q[..., QK_NOPE:], cos[None, :, None, :], sin[None, :, None, :])
    scale = 1.0 / math.sqrt(QK_NOPE + QK_ROPE)
    outs = []
    for q0 in range(0, seq, Q_BLOCK):
        kv_len = q0 + Q_BLOCK
        s = (jnp.einsum('bqhd,bkhd->bhqk', q_nope[:, q0:kv_len], k_nope[:, :kv_len])
             + jnp.einsum('bqhr,bkr->bhqk', q_rope[:, q0:kv_len], k_rope[:, :kv_len]))
        s = s.astype(jnp.float32) * scale
        q_pos = q0 + jnp.arange(Q_BLOCK)
        k_pos = jnp.arange(kv_len)
        s = jnp.where(k_pos[None, :] <= q_pos[:, None], s, -jnp.inf)
        p = jax.nn.softmax(s, axis=-1).astype(v.dtype)
        outs.append(jnp.einsum('bhqk,bkhd->bqhd', p, v[:, :kv_len]))
    o = jnp.concatenate(outs, axis=1).reshape(bsz, seq, MLA_HEADS * V_DIM)
    return o @ o_w


def _swiglu(x, w_in, w_out):
    g, u = jnp.split(x @ w_in, 2, axis=-1)
    return (jax.nn.silu(g) * u) @ w_out


def _moe_swiglu(x, w_router, w_in, w_out):
    bsz, seq, d = x.shape
    n_tok = bsz * seq
    n_assign = n_tok * TOP_K
    xf = x.reshape(n_tok, d)
    logits = (xf @ w_router).astype(jnp.float32)
    top_val, top_idx = lax.top_k(logits, TOP_K)
    gate = jax.nn.softmax(top_val, axis=-1)
    flat_e = top_idx.reshape(-1)
    flat_tok = jnp.repeat(jnp.arange(n_tok, dtype=jnp.int32), TOP_K)
    flat_w = gate.reshape(-1)
    order = jnp.argsort(flat_e)
    sorted_e = flat_e[order]
    counts = jnp.bincount(flat_e, length=N_EXPERTS)
    starts = jnp.cumsum(counts) - counts
    padded = ((counts + ROUTE_BLOCK - 1) // ROUTE_BLOCK) * ROUTE_BLOCK
    pad_ends = jnp.cumsum(padded)
    pad_starts = pad_ends - padded
    dest = pad_starts[sorted_e] + (jnp.arange(n_assign) - starts[sorted_e])
    n_rows = n_assign + N_EXPERTS * ROUTE_BLOCK
    n_blocks = n_rows // ROUTE_BLOCK
    row_tok = jnp.full((n_rows,), n_tok, jnp.int32).at[dest].set(flat_tok[order])
    row_w = jnp.zeros((n_rows,), jnp.float32).at[dest].set(flat_w[order])
    blk_e = jnp.minimum(jnp.searchsorted(pad_ends, jnp.arange(n_blocks) * ROUTE_BLOCK, side='right'),
                        N_EXPERTS - 1)
    x_pad = jnp.concatenate([xf, jnp.zeros((1, d), xf.dtype)], axis=0)
    xs = x_pad[row_tok].reshape(n_blocks, ROUTE_BLOCK, d)

    def _expert_block(args):
        xb, e = args
        return _swiglu(xb, w_in[e], w_out[e])

    ys = lax.map(_expert_block, (xs, blk_e)).reshape(n_rows, d)
    out = jnp.zeros((n_tok + 1, d), ys.dtype).at[row_tok].add(ys * row_w[:, None].astype(ys.dtype))
    return out[:n_tok].reshape(bsz, seq, d)


def _normal(key, shape, fan_in, scale=1.0):
    return jax.random.normal(key, shape, jnp.float32) * (scale * fan_in ** -0.5)


def setup_inputs(seed: int = 0) -> dict:
    key = jax.random.key(seed)
    k = jax.random.split(key, 28)
    f32 = jnp.float32
    G, P, C = S5_GROUPS, S5_STATE, S5_GROUP
    x = jax.random.normal(k[0], (BATCH, SEQ, D_MODEL), f32)
    s5_lam_re = -0.5 + 0.01 * jax.random.normal(k[1], (N_A_LAYERS, G, P), f32)
    s5_lam_im = (jnp.pi * jnp.arange(P, dtype=f32)[None, None, :]
                 + 0.01 * jax.random.normal(k[2], (N_A_LAYERS, G, P), f32))
    s5_log_step = jax.random.uniform(k[3], (N_A_LAYERS, G), f32,
                                     minval=math.log(S5_DT_MIN), maxval=math.log(S5_DT_MAX))
    s5_b_re = _normal(k[4], (N_A_LAYERS, G, P, C), 2 * C)
    s5_b_im = _normal(k[5], (N_A_LAYERS, G, P, C), 2 * C)
    s5_c_re = _normal(k[6], (N_A_LAYERS, G, C, P), 2 * P)
    s5_c_im = _normal(k[7], (N_A_LAYERS, G, C, P), 2 * P)
    s5_d = jax.random.normal(k[8], (N_A_LAYERS, D_MODEL), f32)
    s5_w_glu = jnp.concatenate([_normal(k[9], (N_A_LAYERS, D_MODEL, D_MODEL), D_MODEL, BETA),
                                _normal(k[10], (N_A_LAYERS, D_MODEL, D_MODEL), D_MODEL)], axis=-1)
    mla_q_w_a = _normal(k[11], (N_B_LAYERS, D_MODEL, Q_LORA), D_MODEL)
    mla_q_norm = 1.0 + 0.01 * jax.random.normal(k[12], (N_B_LAYERS, Q_LORA), f32)
    mla_q_w_b = _normal(k[13], (N_B_LAYERS, Q_LORA, MLA_HEADS * (QK_NOPE + QK_ROPE)), Q_LORA)
    mla_o_w = _normal(k[14], (N_B_LAYERS, MLA_HEADS * V_DIM, D_MODEL), MLA_HEADS * V_DIM, BETA)
    kv_w_a = _normal(k[15], (D_MODEL, KV_LORA + QK_ROPE), D_MODEL)
    kv_norm = 1.0 + 0.01 * jax.random.normal(k[16], (KV_LORA,), f32)
    kv_w_b = jnp.concatenate([_normal(k[17], (KV_LORA, MLA_HEADS, QK_NOPE), KV_LORA),
                              _normal(k[18], (KV_LORA, MLA_HEADS, V_DIM), KV_LORA, BETA)],
                             axis=-1).reshape(KV_LORA, MLA_HEADS * (QK_NOPE + V_DIM))
    ffn_w_in = _normal(k[19], (N_DENSE_FFN, D_MODEL, 2 * FFN_DIM), D_MODEL)
    ffn_w_out = _normal(k[20], (N_DENSE_FFN, FFN_DIM, D_MODEL), FFN_DIM, BETA)
    moe_router = _normal(k[21], (N_MOE_FFN, D_MODEL, N_EXPERTS), D_MODEL)
    moe_w_in = _normal(k[22], (N_MOE_FFN, N_EXPERTS, D_MODEL, 2 * EXPERT_DIM), D_MODEL)
    moe_w_out = _normal(k[23], (N_MOE_FFN, N_EXPERTS, EXPERT_DIM, D_MODEL), EXPERT_DIM, BETA)
    ln_g = 1.0 + 0.01 * jax.random.normal(k[24], (DEPTH, 2, D_MODEL), f32)
    ln_b = 0.01 * jax.random.normal(k[25], (DEPTH, 2, D_MODEL), f32)
    return {'x': x,
            's5_lam_re': s5_lam_re, 's5_lam_im': s5_lam_im, 's5_log_step': s5_log_step,
            's5_b_re': s5_b_re, 's5_b_im': s5_b_im, 's5_c_re': s5_c_re, 's5_c_im': s5_c_im,
            's5_d': s5_d, 's5_w_glu': s5_w_glu,
            'mla_q_w_a': mla_q_w_a, 'mla_q_norm': mla_q_norm, 'mla_q_w_b': mla_q_w_b, 'mla_o_w': mla_o_w,
            'kv_w_a': kv_w_a, 'kv_norm': kv_norm, 'kv_w_b': kv_w_b,
            'ffn_w_in': ffn_w_in, 'ffn_w_out': ffn_w_out,
            'moe_router': moe_router, 'moe_w_in': moe_w_in, 'moe_w_out': moe_w_out,
            'ln_g': ln_g, 'ln_b': ln_b}


def reference(x, s5_lam_re, s5_lam_im, s5_log_step, s5_b_re, s5_b_im, s5_c_re, s5_c_im,
              s5_d, s5_w_glu, mla_q_w_a, mla_q_norm, mla_q_w_b, mla_o_w,
              kv_w_a, kv_norm, kv_w_b, ffn_w_in, ffn_w_out,
              moe_router, moe_w_in, moe_w_out, ln_g, ln_b):
    cos, sin = _rope_tables(x.shape[1])
    k_nope = k_rope = v = None
    for layer in range(DEPTH):
        if layer < N_A_LAYERS:
            i = layer
            mix = _s5_mixer(x, s5_lam_re[i], s5_lam_im[i], s5_log_step[i], s5_b_re[i], s5_b_im[i],
                            s5_c_re[i], s5_c_im[i], s5_d[i], s5_w_glu[i])
        else:
            if layer == N_A_LAYERS:
                k_nope, k_rope, v = _mla_shared_kv(x, kv_w_a, kv_norm, kv_w_b, cos, sin)
            i = layer - N_A_LAYERS
            mix = _mla_attention(x, mla_q_w_a[i], mla_q_norm[i], mla_q_w_b[i], mla_o_w[i],
                                 k_nope, k_rope, v, cos, sin)
        x = _layer_norm(ALPHA * x + mix, ln_g[layer, 0], ln_b[layer, 0])
        if layer % 2 == 0:
            ffn = _swiglu(x, ffn_w_in[layer // 2], ffn_w_out[layer // 2])
        else:
            ffn = _moe_swiglu(x, moe_router[layer // 2], moe_w_in[layer // 2], moe_w_out[layer // 2])
        x = _layer_norm(ALPHA * x + ffn, ln_g[layer, 1], ln_b[layer, 1])
    return x
```

```python
import functools
import math

import jax
import jax.numpy as jnp
from jax import lax
from jax.experimental import pallas as pl
from jax.experimental.pallas import tpu as pltpu

F32 = jnp.float32
BF16 = jnp.bfloat16

V7X_LANES = 128
V7X_SUBLANES = 8
V7X_VMEM_LIMIT = 56 * 1024 * 1024

DEPTH = 2
ALPHA = (2.0 * DEPTH) ** 0.25
LN_EPS = 1e-5
RMS_EPS = 1e-6
ROPE_THETA = 10000.0

S5_GROUP = 16
S5_STATE = 64
S5_BLOCK_GROUPS = 16

MLA_HEADS = 8
QK_NOPE = 128
QK_ROPE = 64
V_DIM = 128
N_EXPERTS = 8
TOP_K = 2


def _const_spec(shape):
    zeros = (0,) * len(shape)
    return pl.BlockSpec(shape, lambda *_: zeros, pipeline_mode=pl.Buffered(1))


def _layer_norm(h, g, b):
    mu = jnp.mean(h, axis=-1, keepdims=True)
    c = h - mu
    var = jnp.mean(c * c, axis=-1, keepdims=True)
    return c * lax.rsqrt(var + LN_EPS) * g + b


def _gelu_tanh(y):
    return 0.5 * y * (1.0 + jnp.tanh(math.sqrt(2.0 / math.pi) * (y + 0.044715 * (y * y * y))))


def _s5_kernel(x_ref, bb_ref, cc_ref, are_ref, aim_ref, d_ref, wglu_ref, g_ref, b_ref,
               o_ref, bu_ref, st_ref, *, n_batch, n_time, n_blocks):
    half = bb_ref.shape[2] // 2
    cb = bb_ref.shape[1]
    slab = 512
    d_model = x_ref.shape[1]

    @pl.when(pl.program_id(0) == 0)
    def _():
        st_ref[...] = jnp.zeros_like(st_ref)

    x = x_ref[...]
    xb = x.astype(BF16)
    for k in range(n_blocks):
        bu_ref[:, k * 2 * half:(k + 1) * 2 * half] = jnp.dot(
            xb[:, k * cb:(k + 1) * cb], bb_ref[k], preferred_element_type=F32)

    for k in range(n_blocks):
        for j in range(half // slab):
            re0 = k * 2 * half + j * slab
            im0 = re0 + half
            a0 = k * half + j * slab
            for bh in range(n_batch // V7X_SUBLANES):
                r0 = bh * V7X_SUBLANES
                ar = jnp.broadcast_to(are_ref[:, a0:a0 + slab], (V7X_SUBLANES, slab))
                ai = jnp.broadcast_to(aim_ref[:, a0:a0 + slab], (V7X_SUBLANES, slab))
                s_re = st_ref[r0:r0 + V7X_SUBLANES, re0:re0 + slab]
                s_im = st_ref[r0:r0 + V7X_SUBLANES, im0:im0 + slab]

                def step(t, carry, re0=re0, im0=im0, r0=r0, ar=ar, ai=ai):
                    s_re, s_im = carry
                    row = pl.multiple_of(t * n_batch + r0, V7X_SUBLANES)
                    b_re = bu_ref[pl.ds(row, V7X_SUBLANES), re0:re0 + slab]
                    b_im = bu_ref[pl.ds(row, V7X_SUBLANES), im0:im0 + slab]
                    n_re = ar * s_re - ai * s_im + b_re
                    n_im = ar * s_im + ai * s_re + b_im
                    bu_ref[pl.ds(row, V7X_SUBLANES), re0:re0 + slab] = n_re
                    bu_ref[pl.ds(row, V7X_SUBLANES), im0:im0 + slab] = n_im
                    return n_re, n_im

                s_re, s_im = lax.fori_loop(0, n_time, step, (s_re, s_im), unroll=4)
                st_ref[r0:r0 + V7X_SUBLANES, re0:re0 + slab] = s_re
                st_ref[r0:r0 + V7X_SUBLANES, im0:im0 + slab] = s_im

    ys = []
    for k in range(n_blocks):
        s_blk = bu_ref[:, k * 2 * half:(k + 1) * 2 * half].astype(BF16)
        ys.append(jnp.dot(s_blk, cc_ref[k], preferred_element_type=F32))
    y = jnp.concatenate(ys, axis=-1) + d_ref[...] * x
    act = _gelu_tanh(y).astype(BF16)
    z = jnp.dot(act, wglu_ref[...], preferred_element_type=F32)
    mix = z[:, :d_model] * jax.nn.sigmoid(z[:, d_model:])
    o_ref[...] = _layer_norm(ALPHA * x + mix, g_ref[...], b_ref[...])


def _s5_layer(xt, bb, cc, a_re, a_im, d_skip, w_glu, ln_g, ln_b, *, n_batch, n_time):
    n_rows, d_model = xt.shape
    n_blocks = bb.shape[0]
    rows = n_time * n_batch
    state_w = n_blocks * bb.shape[2]
    kern = functools.partial(_s5_kernel, n_batch=n_batch, n_time=n_time, n_blocks=n_blocks)
    return pl.pallas_call(
        kern,
        grid=(n_rows // rows,),
        in_specs=[
            pl.BlockSpec((rows, d_model), lambda i: (i, 0)),
            _const_spec(bb.shape), _const_spec(cc.shape),
            _const_spec(a_re.shape), _const_spec(a_im.shape), _const_spec(d_skip.shape),
            _const_spec(w_glu.shape), _const_spec(ln_g.shape), _const_spec(ln_b.shape),
        ],
        out_specs=pl.BlockSpec((rows, d_model), lambda i: (i, 0)),
        out_shape=jax.ShapeDtypeStruct((n_rows, d_model), F32),
        scratch_shapes=[pltpu.VMEM((rows, state_w), F32), pltpu.VMEM((n_batch, state_w), F32)],
        compiler_params=pltpu.CompilerParams(
            dimension_semantics=("arbitrary",), vmem_limit_bytes=V7X_VMEM_LIMIT),
        name="s5_mixer",
    )(xt, bb, cc, a_re, a_im, d_skip, w_glu, ln_g, ln_b)


def _ffn_kernel(x_ref, wg_ref, wu_ref, wo_ref, g_ref, b_ref, o_ref, acc_ref, xb_ref):
    f = pl.program_id(1)

    @pl.when(f == 0)
    def _():
        acc_ref[...] = jnp.zeros_like(acc_ref)
        xb_ref[...] = x_ref[...].astype(BF16)

    xb = xb_ref[...]
    gate = jnp.dot(xb, wg_ref[...], preferred_element_type=F32)
    up = jnp.dot(xb, wu_ref[...], preferred_element_type=F32)
    h = (gate * jax.nn.sigmoid(gate) * up).astype(BF16)
    acc_ref[...] += jnp.dot(h, wo_ref[...], preferred_element_type=F32)

    @pl.when(f == pl.num_programs(1) - 1)
    def _():
        o_ref[...] = _layer_norm(ALPHA * x_ref[...] + acc_ref[...], g_ref[...], b_ref[...])


def _dense_ffn(x, w_in, w_out, ln_g, ln_b, *, tm, fc):
    n_rows, d_model = x.shape
    ffn = w_out.shape[0]
    nf = ffn // fc
    return pl.pallas_call(
        _ffn_kernel,
        grid=(n_rows // tm, nf),
        in_specs=[
            pl.BlockSpec((tm, d_model), lambda i, f: (i, 0)),
            pl.BlockSpec((d_model, fc), lambda i, f: (0, f)),
            pl.BlockSpec((d_model, fc), lambda i, f: (0, f + nf)),
            pl.BlockSpec((fc, d_model), lambda i, f: (f, 0)),
            _const_spec(ln_g.shape), _const_spec(ln_b.shape),
        ],
        out_specs=pl.BlockSpec((tm, d_model), lambda i, f: (i, 0)),
        out_shape=jax.ShapeDtypeStruct((n_rows, d_model), F32),
        scratch_shapes=[pltpu.VMEM((tm, d_model), F32), pltpu.VMEM((tm, d_model), BF16)],
        compiler_params=pltpu.CompilerParams(
            dimension_semantics=("parallel", "arbitrary"), vmem_limit_bytes=V7X_VMEM_LIMIT),
        name="dense_ffn",
    )(x, w_in, w_in, w_out, ln_g, ln_b)


def _qkv_kernel(x_ref, wkva_ref, kvn_ref, wkvb_ref, wqa_ref, qn_ref, wqb_ref, cos_ref, sin_ref,
                q_ref, k_ref, v_ref, *, scale):
    xb = x_ref[...].astype(BF16)
    cosx = cos_ref[...]
    sinx = sin_ref[...]

    kva = jnp.dot(xb, wkva_ref[...], preferred_element_type=F32)
    ckv = kva[:, :V7X_LANES]
    ckv = ckv * lax.rsqrt(jnp.mean(ckv * ckv, axis=-1, keepdims=True) + RMS_EPS) * kvn_ref[...]
    k_rope = (kva[:, V7X_LANES:2 * V7X_LANES] * cosx
              + kva[:, 2 * V7X_LANES:3 * V7X_LANES] * sinx).astype(BF16)
    kv = jnp.dot(ckv.astype(BF16), wkvb_ref[...], preferred_element_type=F32)

    cq = jnp.dot(xb, wqa_ref[...], preferred_element_type=F32)
    cq = cq * lax.rsqrt(jnp.mean(cq * cq, axis=-1, keepdims=True) + RMS_EPS) * qn_ref[...]
    q = jnp.dot(cq.astype(BF16), wqb_ref[...], preferred_element_type=F32)

    for h in range(MLA_HEADS):
        kb = h * (QK_NOPE + V_DIM)
        k_ref[0, h, :, :QK_NOPE] = kv[:, kb:kb + QK_NOPE].astype(BF16)
        k_ref[0, h, :, QK_NOPE:] = k_rope
        v_ref[0, h] = kv[:, kb + QK_NOPE:kb + QK_NOPE + V_DIM].astype(BF16)
        qb = h * 3 * V7X_LANES
        q_ref[0, h, :, :QK_NOPE] = (q[:, qb:qb + QK_NOPE] * scale).astype(BF16)
        q_rope = (q[:, qb + V7X_LANES:qb + 2 * V7X_LANES] * cosx
                  + q[:, qb + 2 * V7X_LANES:qb + 3 * V7X_LANES] * sinx)
        q_ref[0, h, :, QK_NOPE:] = (q_rope * scale).astype(BF16)


def _qkv_proj(x_sbd, wkva, kvn, wkvb, wqa, qn, wqb, cosx, sinx, *, ts, n_batch):
    seq = x_sbd.shape[0]
    d_model = x_sbd.shape[1] // n_batch
    scale = 1.0 / math.sqrt(QK_NOPE + QK_ROPE)
    dk = QK_NOPE + V7X_LANES
    kern = functools.partial(_qkv_kernel, scale=scale)
    return pl.pallas_call(
        kern,
        grid=(n_batch, seq // ts),
        in_specs=[
            pl.BlockSpec((ts, d_model), lambda b, s: (s, b)),
            _const_spec(wkva.shape), _const_spec(kvn.shape), _const_spec(wkvb.shape),
            _const_spec(wqa.shape), _const_spec(qn.shape), _const_spec(wqb.shape),
            pl.BlockSpec((ts, V7X_LANES), lambda b, s: (s, 0)),
            pl.BlockSpec((ts, V7X_LANES), lambda b, s: (s, 0)),
        ],
        out_specs=[
            pl.BlockSpec((1, MLA_HEADS, ts, dk), lambda b, s: (b, 0, s, 0)),
            pl.BlockSpec((1, MLA_HEADS, ts, dk), lambda b, s: (b, 0, s, 0)),
            pl.BlockSpec((1, MLA_HEADS, ts, V_DIM), lambda b, s: (b, 0, s, 0)),
        ],
        out_shape=[
            jax.ShapeDtypeStruct((n_batch, MLA_HEADS, seq, dk), BF16),
            jax.ShapeDtypeStruct((n_batch, MLA_HEADS, seq, dk), BF16),
            jax.ShapeDtypeStruct((n_batch, MLA_HEADS, seq, V_DIM), BF16),
        ],
        compiler_params=pltpu.CompilerParams(
            dimension_semantics=("parallel", "parallel"), vmem_limit_bytes=V7X_VMEM_LIMIT),
        name="qkv_proj",
    )(x_sbd, wkva, kvn, wkvb, wqa, qn, wqb, cosx, sinx)


def _attn_kernel(q_ref, k_ref, v_ref, o_ref, *, tq):
    seq = q_ref.shape[2]
    n_tiles = seq // tq
    row = lax.broadcasted_iota(jnp.int32, (tq, tq), 0)
    col = lax.broadcasted_iota(jnp.int32, (tq, tq), 1)
    causal = col <= row
    for qi in range(n_tiles):
        q = q_ref[0, 0, qi * tq:(qi + 1) * tq, :]
        m = l = acc = None
        for kj in range(qi + 1):
            k = k_ref[0, 0, kj * tq:(kj + 1) * tq, :]
            v = v_ref[0, 0, kj * tq:(kj + 1) * tq, :]
            s = lax.dot_general(q, k, (((1,), (1,)), ((), ())), preferred_element_type=F32)
            if kj == qi:
                s = jnp.where(causal, s, -jnp.inf)
            m_new = jnp.max(s, axis=-1, keepdims=True)
            if kj > 0:
                m_new = jnp.maximum(m, m_new)
            p = jnp.exp(s - m_new)
            pv = jnp.dot(p.astype(BF16), v, preferred_element_type=F32)
            if kj == 0:
                l = jnp.sum(p, axis=-1, keepdims=True)
                acc = pv
            else:
                corr = jnp.exp(m - m_new)
                l = corr * l + jnp.sum(p, axis=-1, keepdims=True)
                acc = corr * acc + pv
            m = m_new
        o_ref[0, qi * tq:(qi + 1) * tq, :] = (acc / l).astype(o_ref.dtype)


def _attention(q, k, v, *, tq):
    n_batch, n_heads, seq, dk = q.shape
    dv = v.shape[3]
    kern = functools.partial(_attn_kernel, tq=tq)
    return pl.pallas_call(
        kern,
        grid=(n_batch, n_heads),
        in_specs=[
            pl.BlockSpec((1, 1, seq, dk), lambda b, h: (b, h, 0, 0)),
            pl.BlockSpec((1, 1, seq, dk), lambda b, h: (b, h, 0, 0)),
            pl.BlockSpec((1, 1, seq, dv), lambda b, h: (b, h, 0, 0)),
        ],
        out_specs=pl.BlockSpec((1, seq, dv), lambda b, h: (b, 0, h)),
        out_shape=jax.ShapeDtypeStruct((n_batch, seq, n_heads * dv), BF16),
        compiler_params=pltpu.CompilerParams(
            dimension_semantics=("parallel", "parallel"), vmem_limit_bytes=V7X_VMEM_LIMIT),
        name="mla_attention",
    )(q, k, v)


def _oproj_router_kernel(o_ref, x_ref, wo_ref, g_ref, b_ref, wrh_ref, wrl_ref, y_ref, info_ref):
    mix = jnp.dot(o_ref[0], wo_ref[...], preferred_element_type=F32)
    y = _layer_norm(ALPHA * x_ref[...] + mix, g_ref[...], b_ref[...])
    y_ref[...] = y

    y_hi = y.astype(BF16)
    y_lo = (y - y_hi.astype(F32)).astype(BF16)
    logits = (jnp.dot(y_hi, wrh_ref[...], preferred_element_type=F32)
              + jnp.dot(y_lo, wrh_ref[...], preferred_element_type=F32)
              + jnp.dot(y_hi, wrl_ref[...], preferred_element_type=F32))
    lane = lax.broadcasted_iota(jnp.int32, logits.shape, 1)
    logits = jnp.where(lane < N_EXPERTS, logits, -jnp.inf)
    m1 = jnp.max(logits, axis=-1, keepdims=True)
    i1 = jnp.min(jnp.where(logits == m1, lane, V7X_LANES), axis=-1, keepdims=True)
    rest = jnp.where(lane == i1, -jnp.inf, logits)
    m2 = jnp.max(rest, axis=-1, keepdims=True)
    i2 = jnp.min(jnp.where(rest == m2, lane, V7X_LANES), axis=-1, keepdims=True)
    e2 = jnp.exp(m2 - m1)
    den = 1.0 + e2
    info = jnp.where(lane == 0, i1.astype(F32),
                     jnp.where(lane == 1, i2.astype(F32),
                               jnp.where(lane == 2, 1.0 / den,
                                         jnp.where(lane == 3, e2 / den, 0.0))))
    info_ref[...] = info


def _oproj_router(o, x_sbd, wo, ln_g, ln_b, wr_hi, wr_lo, *, ts):
    n_batch, seq, d_attn = o.shape
    d_model = x_sbd.shape[1] // n_batch
    nst = seq // ts
    return pl.pallas_call(
        _oproj_router_kernel,
        grid=(n_batch, nst),
        in_specs=[
            pl.BlockSpec((1, ts, d_attn), lambda b, s: (b, s, 0)),
            pl.BlockSpec((ts, d_model), lambda b, s: (s, b)),
            _const_spec(wo.shape), _const_spec(ln_g.shape), _const_spec(ln_b.shape),
            _const_spec(wr_hi.shape), _const_spec(wr_lo.shape),
        ],
        out_specs=[
            pl.BlockSpec((ts, d_model), lambda b, s: (b * nst + s, 0)),
            pl.BlockSpec((ts, V7X_LANES), lambda b, s: (b * nst + s, 0)),
        ],
        out_shape=[
            jax.ShapeDtypeStruct((n_batch * seq, d_model), F32),
            jax.ShapeDtypeStruct((n_batch * seq, V7X_LANES), F32),
        ],
        compiler_params=pltpu.CompilerParams(
            dimension_semantics=("parallel", "parallel"), vmem_limit_bytes=V7X_VMEM_LIMIT),
        name="oproj_router",
    )(o, x_sbd, wo, ln_g, ln_b, wr_hi, wr_lo)


def _dispatch_kernel(pos_ref, x_ref, xs_in_ref, xs_ref, sem, *, td):
    del xs_in_ref

    def issue(r, carry):
        for k in range(TOP_K):
            dst = pos_ref[0, 0, TOP_K * r + k]
            pltpu.make_async_copy(x_ref.at[pl.ds(r, 1)], xs_ref.at[pl.ds(dst, 1)], sem).start()
        return carry

    lax.fori_loop(0, td, issue, 0)

    def drain(r, carry):
        for k in range(TOP_K):
            pltpu.make_async_copy(x_ref.at[pl.ds(0, 1)], xs_ref.at[pl.ds(0, 1)], sem).wait()
        return carry

    lax.fori_loop(0, td, drain, 0)


def _dispatch(pos, x, xs_init, *, td):
    n_tok, d_model = x.shape
    kern = functools.partial(_dispatch_kernel, td=td)
    return pl.pallas_call(
        kern,
        grid=(n_tok // td,),
        in_specs=[
            pl.BlockSpec((1, 1, TOP_K * td), lambda i: (i, 0, 0), memory_space=pltpu.SMEM),
            pl.BlockSpec((td, d_model), lambda i: (i, 0)),
            pl.BlockSpec(memory_space=pl.ANY),
        ],
        out_specs=pl.BlockSpec(memory_space=pl.ANY),
        out_shape=jax.ShapeDtypeStruct(xs_init.shape, xs_init.dtype),
        scratch_shapes=[pltpu.SemaphoreType.DMA(())],
        input_output_aliases={2: 0},
        compiler_params=pltpu.CompilerParams(
            dimension_semantics=("arbitrary",), vmem_limit_bytes=V7X_VMEM_LIMIT,
            has_side_effects=True),
        name="moe_dispatch",
    )(pos.reshape(n_tok // td, 1, TOP_K * td), x, xs_init)


def _moe_kernel(te_ref, nu_ref, x_ref, wg_ref, wu_ref, wo_ref, o_ref, acc_ref, xb_ref):
    del te_ref
    i = pl.program_id(0)
    f = pl.program_id(1)

    @pl.when(i < nu_ref[0])
    def _():
        @pl.when(f == 0)
        def _():
            xb_ref[...] = x_ref[...].astype(BF16)

        xb = xb_ref[...]
        gate = jnp.dot(xb, wg_ref[0], preferred_element_type=F32)
        up = jnp.dot(xb, wu_ref[0], preferred_element_type=F32)
        h = (gate * jax.nn.sigmoid(gate) * up).astype(BF16)
        part = jnp.dot(h, wo_ref[0], preferred_element_type=F32)

        @pl.when(f == 0)
        def _():
            acc_ref[...] = part

        @pl.when(f > 0)
        def _():
            acc_ref[...] += part

        @pl.when(f == pl.num_programs(1) - 1)
        def _():
            o_ref[...] = acc_ref[...]

    @pl.when((i >= nu_ref[0]) & (f == pl.num_programs(1) - 1))
    def _():
        o_ref[...] = jnp.zeros_like(o_ref)


def _moe_experts(tile_expert, n_used, xs, w_in, w_out, *, tm, fc):
    n_rows, d_model = xs.shape
    edim = w_out.shape[1]
    nf = edim // fc
    n_tiles = n_rows // tm

    def row_map(i, f, te, nu):
        return (jnp.minimum(i, nu[0] - 1), 0)

    grid_spec = pltpu.PrefetchScalarGridSpec(
        num_scalar_prefetch=2,
        grid=(n_tiles, nf),
        in_specs=[
            pl.BlockSpec((tm, d_model), row_map),
            pl.BlockSpec((1, d_model, fc), lambda i, f, te, nu: (te[i], 0, f)),
            pl.BlockSpec((1, d_model, fc), lambda i, f, te, nu: (te[i], 0, f + nf)),
            pl.BlockSpec((1, fc, d_model), lambda i, f, te, nu: (te[i], f, 0)),
        ],
        out_specs=pl.BlockSpec((tm, d_model), lambda i, f, te, nu: (i, 0)),
        scratch_shapes=[pltpu.VMEM((tm, d_model), F32), pltpu.VMEM((tm, d_model), BF16)],
    )
    return pl.pallas_call(
        _moe_kernel,
        grid_spec=grid_spec,
        out_shape=jax.ShapeDtypeStruct((n_rows, d_model), F32),
        compiler_params=pltpu.CompilerParams(
            dimension_semantics=("arbitrary", "arbitrary"), vmem_limit_bytes=V7X_VMEM_LIMIT),
        name="moe_experts",
    )(tile_expert, n_used, xs, w_in, w_in, w_out)


def _combine_kernel(pos_ref, x_ref, info_ref, ys_ref, g_ref, b_ref, o_ref, buf_ref, sem, *, tc):
    def issue(r, carry):
        for k in range(TOP_K):
            src = pos_ref[0, 0, TOP_K * r + k]
            pltpu.make_async_copy(ys_ref.at[pl.ds(src, 1)], buf_ref.at[k, pl.ds(r, 1)], sem).start()
        return carry

    lax.fori_loop(0, tc, issue, 0)

    def drain(r, carry):
        for k in range(TOP_K):
            pltpu.make_async_copy(ys_ref.at[pl.ds(0, 1)], buf_ref.at[k, pl.ds(0, 1)], sem).wait()
        return carry

    lax.fori_loop(0, tc, drain, 0)

    info = info_ref[...]
    ffn = buf_ref[0] * info[:, 2:3] + buf_ref[1] * info[:, 3:4]
    o_ref[...] = _layer_norm(ALPHA * x_ref[...] + ffn, g_ref[...], b_ref[...])


def _combine(pos, x, info, ys, ln_g, ln_b, *, tc):
    n_tok, d_model = x.shape
    kern = functools.partial(_combine_kernel, tc=tc)
    return pl.pallas_call(
        kern,
        grid=(n_tok // tc,),
        in_specs=[
            pl.BlockSpec((1, 1, TOP_K * tc), lambda i: (i, 0, 0), memory_space=pltpu.SMEM),
            pl.BlockSpec((tc, d_model), lambda i: (i, 0)),
            pl.BlockSpec((tc, V7X_LANES), lambda i: (i, 0)),
            pl.BlockSpec(memory_space=pl.ANY),
            _const_spec(ln_g.shape), _const_spec(ln_b.shape),
        ],
        out_specs=pl.BlockSpec((tc, d_model), lambda i: (i, 0)),
        out_shape=jax.ShapeDtypeStruct((n_tok, d_model), F32),
        scratch_shapes=[pltpu.VMEM((TOP_K, tc, d_model), F32), pltpu.SemaphoreType.DMA(())],
        compiler_params=pltpu.CompilerParams(
            dimension_semantics=("arbitrary",), vmem_limit_bytes=V7X_VMEM_LIMIT),
        name="moe_combine",
    )(pos.reshape(n_tok // tc, 1, TOP_K * tc), x, info, ys, ln_g, ln_b)


def _s5_params(lam_re, lam_im, log_step, b_re, b_im, c_re, c_im):
    n_groups, n_state = lam_re.shape
    dt = jnp.exp(log_step)[:, None]
    mag = jnp.exp(lam_re * dt)
    lb_re = mag * jnp.cos(lam_im * dt)
    lb_im = mag * jnp.sin(lam_im * dt)
    den = lam_re * lam_re + lam_im * lam_im
    f_re = ((lb_re - 1.0) * lam_re + lb_im * lam_im) / den
    f_im = (lb_im * lam_re - (lb_re - 1.0) * lam_im) / den
    bb_re = f_re[..., None] * b_re - f_im[..., None] * b_im
    bb_im = f_re[..., None] * b_im + f_im[..., None] * b_re
    gpb = S5_BLOCK_GROUPS
    n_blocks = n_groups // gpb
    eye = jnp.eye(gpb, dtype=F32)

    def in_blocks(w):
        w = w.transpose(0, 2, 1).reshape(n_blocks, gpb, S5_GROUP, n_state)
        return jnp.einsum('kgcp,gh->kgchp', w, eye).reshape(n_blocks, gpb * S5_GROUP, gpb * n_state)

    def out_blocks(w):
        w = w.reshape(n_blocks, gpb, S5_GROUP, n_state)
        return jnp.einsum('kgcp,gh->kgphc', w, eye).reshape(n_blocks, gpb * n_state, gpb * S5_GROUP)

    bb = jnp.concatenate([in_blocks(bb_re), in_blocks(bb_im)], axis=-1).astype(BF16)
    cc = jnp.concatenate([out_blocks(c_re), out_blocks(-c_im)], axis=1).astype(BF16)
    return bb, cc, lb_re.reshape(1, -1), lb_im.reshape(1, -1)


def _rope_lane_tables(seq):
    pos = jnp.arange(seq, dtype=F32)
    inv_freq = ROPE_THETA ** (-jnp.arange(0, QK_ROPE, 2, dtype=F32) / QK_ROPE)
    ang = pos[:, None] * inv_freq[None, :]
    cos, sin = jnp.cos(ang), jnp.sin(ang)
    pad = jnp.zeros((seq, V7X_LANES - QK_ROPE), F32)
    return (jnp.concatenate([cos, cos, pad], axis=-1),
            jnp.concatenate([-sin, sin, pad], axis=-1))


def _rope_weight_blocks(w_rope):
    half = QK_ROPE // 2
    t1, t2 = w_rope[:, :half], w_rope[:, half:]
    pad = jnp.zeros((w_rope.shape[0], V7X_LANES - QK_ROPE), w_rope.dtype)
    return jnp.concatenate([t1, t2, pad, t2, t1, pad], axis=-1)


def _routing(info, *, tm, n_tiles):
    n_tok = info.shape[0]
    flat_e = info[:, :TOP_K].astype(jnp.int32).reshape(-1)
    onehot = (flat_e[:, None] == jnp.arange(N_EXPERTS, dtype=jnp.int32)[None, :]).astype(jnp.int32)
    csum = jnp.cumsum(onehot, axis=0)
    rank = jnp.sum(onehot * csum, axis=1) - 1
    counts = csum[-1]
    tiles_e = (counts + tm - 1) // tm
    tile_end = jnp.cumsum(tiles_e)
    tile_start = tile_end - tiles_e
    pos = tile_start[flat_e] * tm + rank
    tile_expert = jnp.minimum(
        jnp.searchsorted(tile_end, jnp.arange(n_tiles, dtype=jnp.int32), side='right'),
        N_EXPERTS - 1).astype(jnp.int32)
    n_used = tile_end[-1:].astype(jnp.int32)
    del n_tok
    return pos.astype(jnp.int32), tile_expert, n_used


def kernel(x, s5_lam_re, s5_lam_im, s5_log_step, s5_b_re, s5_b_im, s5_c_re, s5_c_im, s5_d, s5_w_glu,
           mla_q_w_a, mla_q_norm, mla_q_w_b, mla_o_w, kv_w_a, kv_norm, kv_w_b, ffn_w_in, ffn_w_out,
           moe_router, moe_w_in, moe_w_out, ln_g, ln_b):
    n_batch, seq, d_model = x.shape
    n_tok = n_batch * seq
    row = lambda v: v.reshape(1, -1).astype(F32)

    bb, cc, a_re, a_im = _s5_params(s5_lam_re[0], s5_lam_im[0], s5_log_step[0], s5_b_re[0], s5_b_im[0],
                                    s5_c_re[0], s5_c_im[0])
    xt = x.transpose(1, 0, 2).reshape(n_tok, d_model)
    x1 = _s5_layer(xt, bb, cc, a_re, a_im, row(s5_d[0]), s5_w_glu[0].astype(BF16),
                   row(ln_g[0, 0]), row(ln_b[0, 0]), n_batch=n_batch, n_time=min(32, seq))
    x2 = _dense_ffn(x1, ffn_w_in[0].astype(BF16), ffn_w_out[0].astype(BF16),
                    row(ln_g[0, 1]), row(ln_b[0, 1]), tm=512, fc=ffn_w_out.shape[1] // 2)
    x2 = x2.reshape(seq, n_batch * d_model)

    kv_lora = kv_norm.shape[0]
    wkva = jnp.concatenate([kv_w_a[:, :kv_lora], _rope_weight_blocks(kv_w_a[:, kv_lora:])],
                           axis=-1).astype(BF16)
    wq = mla_q_w_b[0].reshape(-1, MLA_HEADS, QK_NOPE + QK_ROPE)
    wqb = jnp.concatenate(
        [jnp.concatenate([wq[:, h, :QK_NOPE], _rope_weight_blocks(wq[:, h, QK_NOPE:])], axis=-1)
         for h in range(MLA_HEADS)], axis=-1).astype(BF16)
    cosx, sinx = _rope_lane_tables(seq)
    ts = min(512, seq)
    q, k, v = _qkv_proj(x2, wkva, row(kv_norm), kv_w_b.astype(BF16), mla_q_w_a[0].astype(BF16),
                        row(mla_q_norm[0]), wqb, cosx, sinx, ts=ts, n_batch=n_batch)
    o = _attention(q, k, v, tq=min(512, seq))
    wr = jnp.pad(moe_router[0], ((0, 0), (0, V7X_LANES - N_EXPERTS)))
    wr_hi = wr.astype(BF16)
    wr_lo = (wr - wr_hi.astype(F32)).astype(BF16)
    x3, info = _oproj_router(o, x2, mla_o_w[0].astype(BF16), row(ln_g[1, 0]), row(ln_b[1, 0]),
                             wr_hi, wr_lo, ts=ts)

    tm = 512
    n_tiles = (n_tok * TOP_K) // tm + N_EXPERTS
    pos, tile_expert, n_used = _routing(info, tm=tm, n_tiles=n_tiles)
    xs = _dispatch(pos, x3, jnp.zeros((n_tiles * tm, d_model), F32), td=min(512, n_tok))
    ys = _moe_experts(tile_expert, n_used, xs, moe_w_in[0].astype(BF16), moe_w_out[0].astype(BF16),
                      tm=tm, fc=moe_w_out.shape[2] // 4)
    out = _combine(pos, x3, info, ys, row(ln_g[1, 1]), row(ln_b[1, 1]), tc=min(512, n_tok))
    return out.reshape(n_batch, seq, d_model)
```

```python
import functools
import math

import jax
import jax.numpy as jnp
from jax import lax
from jax.experimental import pallas as pl
from jax.experimental.pallas import tpu as pltpu

F32 = jnp.float32
BF16 = jnp.bfloat16

V7X_LANES = 128
V7X_SUBLANES = 8
V7X_VMEM_LIMIT = 56 * 1024 * 1024

DEPTH = 2
ALPHA = (2.0 * DEPTH) ** 0.25
LN_EPS = 1e-5
RMS_EPS = 1e-6
ROPE_THETA = 10000.0

S5_GROUP = 16
S5_STATE = 64
S5_BLOCK_GROUPS = 16

MLA_HEADS = 8
QK_NOPE = 128
QK_ROPE = 64
V_DIM = 128
N_EXPERTS = 8
TOP_K = 2


def _const_spec(shape):
    zeros = (0,) * len(shape)
    return pl.BlockSpec(shape, lambda *_: zeros, pipeline_mode=pl.Buffered(1))


def _layer_norm(h, g, b):
    mu = jnp.mean(h, axis=-1, keepdims=True)
    c = h - mu
    var = jnp.mean(c * c, axis=-1, keepdims=True)
    return c * lax.rsqrt(var + LN_EPS) * g + b


def _gelu_tanh(y):
    return 0.5 * y * (1.0 + jnp.tanh(math.sqrt(2.0 / math.pi) * (y + 0.044715 * (y * y * y))))


def _s5_kernel(x_ref, bb_ref, cc_ref, are_ref, aim_ref, d_ref, wglu_ref, g_ref, b_ref,
               o_ref, bu_ref, st_ref, *, n_batch, n_time, n_blocks):
    half = bb_ref.shape[2] // 2
    cb = bb_ref.shape[1]
    slab = 512
    d_model = x_ref.shape[1]

    @pl.when(pl.program_id(0) == 0)
    def _():
        st_ref[...] = jnp.zeros_like(st_ref)

    x = x_ref[...]
    xb = x.astype(BF16)
    for k in range(n_blocks):
        bu_ref[:, k * 2 * half:(k + 1) * 2 * half] = jnp.dot(
            xb[:, k * cb:(k + 1) * cb], bb_ref[k], preferred_element_type=F32)

    for k in range(n_blocks):
        for j in range(half // slab):
            re0 = k * 2 * half + j * slab
            im0 = re0 + half
            a0 = k * half + j * slab
            for bh in range(n_batch // V7X_SUBLANES):
                r0 = bh * V7X_SUBLANES
                ar = jnp.broadcast_to(are_ref[:, a0:a0 + slab], (V7X_SUBLANES, slab))
                ai = jnp.broadcast_to(aim_ref[:, a0:a0 + slab], (V7X_SUBLANES, slab))
                s_re = st_ref[r0:r0 + V7X_SUBLANES, re0:re0 + slab]
                s_im = st_ref[r0:r0 + V7X_SUBLANES, im0:im0 + slab]

                def step(t, carry, re0=re0, im0=im0, r0=r0, ar=ar, ai=ai):
                    s_re, s_im = carry
                    row = pl.multiple_of(t * n_batch + r0, V7X_SUBLANES)
                    b_re = bu_ref[pl.ds(row, V7X_SUBLANES), re0:re0 + slab]
                    b_im = bu_ref[pl.ds(row, V7X_SUBLANES), im0:im0 + slab]
                    n_re = ar * s_re - ai * s_im + b_re
                    n_im = ar * s_im + ai * s_re + b_im
                    bu_ref[pl.ds(row, V7X_SUBLANES), re0:re0 + slab] = n_re
                    bu_ref[pl.ds(row, V7X_SUBLANES), im0:im0 + slab] = n_im
                    return n_re, n_im

                s_re, s_im = lax.fori_loop(0, n_time, step, (s_re, s_im), unroll=4)
                st_ref[r0:r0 + V7X_SUBLANES, re0:re0 + slab] = s_re
                st_ref[r0:r0 + V7X_SUBLANES, im0:im0 + slab] = s_im

    ys = []
    for k in range(n_blocks):
        s_blk = bu_ref[:, k * 2 * half:(k + 1) * 2 * half].astype(BF16)
        ys.append(jnp.dot(s_blk, cc_ref[k], preferred_element_type=F32))
    y = jnp.concatenate(ys, axis=-1) + d_ref[...] * x
    act = _gelu_tanh(y).astype(BF16)
    z = jnp.dot(act, wglu_ref[...], preferred_element_type=F32)
    mix = z[:, :d_model] * jax.nn.sigmoid(z[:, d_model:])
    o_ref[...] = _layer_norm(ALPHA * x + mix, g_ref[...], b_ref[...])


def _s5_layer(xt, bb, cc, a_re, a_im, d_skip, w_glu, ln_g, ln_b, *, n_batch, n_time):
    n_rows, d_model = xt.shape
    n_blocks = bb.shape[0]
    rows = n_time * n_batch
    state_w = n_blocks * bb.shape[2]
    kern = functools.partial(_s5_kernel, n_batch=n_batch, n_time=n_time, n_blocks=n_blocks)
    return pl.pallas_call(
        kern,
        grid=(n_rows // rows,),
        in_specs=[
            pl.BlockSpec((rows, d_model), lambda i: (i, 0)),
            _const_spec(bb.shape), _const_spec(cc.shape),
            _const_spec(a_re.shape), _const_spec(a_im.shape), _const_spec(d_skip.shape),
            _const_spec(w_glu.shape), _const_spec(ln_g.shape), _const_spec(ln_b.shape),
        ],
        out_specs=pl.BlockSpec((rows, d_model), lambda i: (i, 0)),
        out_shape=jax.ShapeDtypeStruct((n_rows, d_model), F32),
        scratch_shapes=[pltpu.VMEM((rows, state_w), F32), pltpu.VMEM((n_batch, state_w), F32)],
        compiler_params=pltpu.CompilerParams(
            dimension_semantics=("arbitrary",), vmem_limit_bytes=V7X_VMEM_LIMIT),
        name="s5_mixer",
    )(xt, bb, cc, a_re, a_im, d_skip, w_glu, ln_g, ln_b)


def _ffn_kernel(x_ref, wg_ref, wu_ref, wo_ref, g_ref, b_ref, o_ref, acc_ref, xb_ref):
    f = pl.program_id(1)

    @pl.when(f == 0)
    def _():
        acc_ref[...] = jnp.zeros_like(acc_ref)
        xb_ref[...] = x_ref[...].astype(BF16)

    xb = xb_ref[...]
    gate = jnp.dot(xb, wg_ref[...], preferred_element_type=F32)
    up = jnp.dot(xb, wu_ref[...], preferred_element_type=F32)
    h = (gate * jax.nn.sigmoid(gate) * up).astype(BF16)
    acc_ref[...] += jnp.dot(h, wo_ref[...], preferred_element_type=F32)

    @pl.when(f == pl.num_programs(1) - 1)
    def _():
        o_ref[...] = _layer_norm(ALPHA * x_ref[...] + acc_ref[...], g_ref[...], b_ref[...])


def _dense_ffn(x, w_in, w_out, ln_g, ln_b, *, tm, fc):
    n_rows, d_model = x.shape
    ffn = w_out.shape[0]
    nf = ffn // fc
    return pl.pallas_call(
        _ffn_kernel,
        grid=(n_rows // tm, nf),
        in_specs=[
            pl.BlockSpec((tm, d_model), lambda i, f: (i, 0)),
            pl.BlockSpec((d_model, fc), lambda i, f: (0, f)),
            pl.BlockSpec((d_model, fc), lambda i, f: (0, f + nf)),
            pl.BlockSpec((fc, d_model), lambda i, f: (f, 0)),
            _const_spec(ln_g.shape), _const_spec(ln_b.shape),
        ],
        out_specs=pl.BlockSpec((tm, d_model), lambda i, f: (i, 0)),
        out_shape=jax.ShapeDtypeStruct((n_rows, d_model), F32),
        scratch_shapes=[pltpu.VMEM((tm, d_model), F32), pltpu.VMEM((tm, d_model), BF16)],
        compiler_params=pltpu.CompilerParams(
            dimension_semantics=("parallel", "arbitrary"), vmem_limit_bytes=V7X_VMEM_LIMIT),
        name="dense_ffn",
    )(x, w_in, w_in, w_out, ln_g, ln_b)


def _qkv_kernel(x_ref, wkva_ref, kvn_ref, wkvb_ref, wqa_ref, qn_ref, wqb_ref, cos_ref, sin_ref,
                q_ref, k_ref, v_ref, *, scale):
    xb = x_ref[...].astype(BF16)
    cosx = cos_ref[...]
    sinx = sin_ref[...]

    kva = jnp.dot(xb, wkva_ref[...], preferred_element_type=F32)
    ckv = kva[:, :V7X_LANES]
    ckv = ckv * lax.rsqrt(jnp.mean(ckv * ckv, axis=-1, keepdims=True) + RMS_EPS) * kvn_ref[...]
    k_rope = (kva[:, V7X_LANES:2 * V7X_LANES] * cosx
              + kva[:, 2 * V7X_LANES:3 * V7X_LANES] * sinx).astype(BF16)
    kv = jnp.dot(ckv.astype(BF16), wkvb_ref[...], preferred_element_type=F32)

    cq = jnp.dot(xb, wqa_ref[...], preferred_element_type=F32)
    cq = cq * lax.rsqrt(jnp.mean(cq * cq, axis=-1, keepdims=True) + RMS_EPS) * qn_ref[...]
    q = jnp.dot(cq.astype(BF16), wqb_ref[...], preferred_element_type=F32)

    for h in range(MLA_HEADS):
        kb = h * (QK_NOPE + V_DIM)
        k_ref[0, h, :, :QK_NOPE] = kv[:, kb:kb + QK_NOPE].astype(BF16)
        k_ref[0, h, :, QK_NOPE:] = k_rope
        v_ref[0, h] = kv[:, kb + QK_NOPE:kb + QK_NOPE + V_DIM].astype(BF16)
        qb = h * 3 * V7X_LANES
        q_ref[0, h, :, :QK_NOPE] = (q[:, qb:qb + QK_NOPE] * scale).astype(BF16)
        q_rope = (q[:, qb + V7X_LANES:qb + 2 * V7X_LANES] * cosx
                  + q[:, qb + 2 * V7X_LANES:qb + 3 * V7X_LANES] * sinx)
        q_ref[0, h, :, QK_NOPE:] = (q_rope * scale).astype(BF16)


def _qkv_proj(x_sbd, wkva, kvn, wkvb, wqa, qn, wqb, cosx, sinx, *, ts, n_batch):
    seq = x_sbd.shape[0]
    d_model = x_sbd.shape[1] // n_batch
    scale = 1.0 / math.sqrt(QK_NOPE + QK_ROPE)
    dk = QK_NOPE + V7X_LANES
    kern = functools.partial(_qkv_kernel, scale=scale)
    return pl.pallas_call(
        kern,
        grid=(n_batch, seq // ts),
        in_specs=[
            pl.BlockSpec((ts, d_model), lambda b, s: (s, b)),
            _const_spec(wkva.shape), _const_spec(kvn.shape), _const_spec(wkvb.shape),
            _const_spec(wqa.shape), _const_spec(qn.shape), _const_spec(wqb.shape),
            pl.BlockSpec((ts, V7X_LANES), lambda b, s: (s, 0)),
            pl.BlockSpec((ts, V7X_LANES), lambda b, s: (s, 0)),
        ],
        out_specs=[
            pl.BlockSpec((1, MLA_HEADS, ts, dk), lambda b, s: (b, 0, s, 0)),
            pl.BlockSpec((1, MLA_HEADS, ts, dk), lambda b, s: (b, 0, s, 0)),
            pl.BlockSpec((1, MLA_HEADS, ts, V_DIM), lambda b, s: (b, 0, s, 0)),
        ],
        out_shape=[
            jax.ShapeDtypeStruct((n_batch, MLA_HEADS, seq, dk), BF16),
            jax.ShapeDtypeStruct((n_batch, MLA_HEADS, seq, dk), BF16),
            jax.ShapeDtypeStruct((n_batch, MLA_HEADS, seq, V_DIM), BF16),
        ],
        compiler_params=pltpu.CompilerParams(
            dimension_semantics=("parallel", "parallel"), vmem_limit_bytes=V7X_VMEM_LIMIT),
        name="qkv_proj",
    )(x_sbd, wkva, kvn, wkvb, wqa, qn, wqb, cosx, sinx)


def _attn_kernel(q_ref, k_ref, v_ref, o_ref, *, tq):
    seq = q_ref.shape[2]
    n_tiles = seq // tq
    row = lax.broadcasted_iota(jnp.int32, (tq, tq), 0)
    col = lax.broadcasted_iota(jnp.int32, (tq, tq), 1)
    causal = col <= row
    for qi in range(n_tiles):
        q = q_ref[0, 0, qi * tq:(qi + 1) * tq, :]
        m = l = acc = None
        for kj in range(qi + 1):
            k = k_ref[0, 0, kj * tq:(kj + 1) * tq, :]
            v = v_ref[0, 0, kj * tq:(kj + 1) * tq, :]
            s = lax.dot_general(q, k, (((1,), (1,)), ((), ())), preferred_element_type=F32)
            if kj == qi:
                s = jnp.where(causal, s, -jnp.inf)
            m_new = jnp.max(s, axis=-1, keepdims=True)
            if kj > 0:
                m_new = jnp.maximum(m, m_new)
            p = jnp.exp(s - m_new)
            pv = jnp.dot(p.astype(BF16), v, preferred_element_type=F32)
            if kj == 0:
                l = jnp.sum(p, axis=-1, keepdims=True)
                acc = pv
            else:
                corr = jnp.exp(m - m_new)
                l = corr * l + jnp.sum(p, axis=-1, keepdims=True)
                acc = corr * acc + pv
            m = m_new
        o_ref[0, qi * tq:(qi + 1) * tq, :] = (acc / l).astype(o_ref.dtype)


def _attention(q, k, v, *, tq):
    n_batch, n_heads, seq, dk = q.shape
    dv = v.shape[3]
    kern = functools.partial(_attn_kernel, tq=tq)
    return pl.pallas_call(
        kern,
        grid=(n_batch, n_heads),
        in_specs=[
            pl.BlockSpec((1, 1, seq, dk), lambda b, h: (b, h, 0, 0)),
            pl.BlockSpec((1, 1, seq, dk), lambda b, h: (b, h, 0, 0)),
            pl.BlockSpec((1, 1, seq, dv), lambda b, h: (b, h, 0, 0)),
        ],
        out_specs=pl.BlockSpec((1, seq, dv), lambda b, h: (b, 0, h)),
        out_shape=jax.ShapeDtypeStruct((n_batch, seq, n_heads * dv), BF16),
        compiler_params=pltpu.CompilerParams(
            dimension_semantics=("parallel", "parallel"), vmem_limit_bytes=V7X_VMEM_LIMIT),
        name="mla_attention",
    )(q, k, v)


def _oproj_router_kernel(o_ref, x_ref, wo_ref, g_ref, b_ref, wrh_ref, wrl_ref, y_ref, info_ref):
    mix = jnp.dot(o_ref[0], wo_ref[...], preferred_element_type=F32)
    y = _layer_norm(ALPHA * x_ref[...] + mix, g_ref[...], b_ref[...])
    y_ref[...] = y

    y_hi = y.astype(BF16)
    y_lo = (y - y_hi.astype(F32)).astype(BF16)
    logits = (jnp.dot(y_hi, wrh_ref[...], preferred_element_type=F32)
              + jnp.dot(y_lo, wrh_ref[...], preferred_element_type=F32)
              + jnp.dot(y_hi, wrl_ref[...], preferred_element_type=F32))
    lane = lax.broadcasted_iota(jnp.int32, logits.shape, 1)
    logits = jnp.where(lane < N_EXPERTS, logits, -jnp.inf)
    m1 = jnp.max(logits, axis=-1, keepdims=True)
    i1 = jnp.min(jnp.where(logits == m1, lane, V7X_LANES), axis=-1, keepdims=True)
    rest = jnp.where(lane == i1, -jnp.inf, logits)
    m2 = jnp.max(rest, axis=-1, keepdims=True)
    i2 = jnp.min(jnp.where(rest == m2, lane, V7X_LANES), axis=-1, keepdims=True)
    e2 = jnp.exp(m2 - m1)
    den = 1.0 + e2
    info = jnp.where(lane == 0, i1.astype(F32),
                     jnp.where(lane == 1, i2.astype(F32),
                               jnp.where(lane == 2, 1.0 / den,
                                         jnp.where(lane == 3, e2 / den, 0.0))))
    info_ref[...] = info


def _oproj_router(o, x_sbd, wo, ln_g, ln_b, wr_hi, wr_lo, *, ts):
    n_batch, seq, d_attn = o.shape
    d_model = x_sbd.shape[1] // n_batch
    nst = seq // ts
    return pl.pallas_call(
        _oproj_router_kernel,
        grid=(n_batch, nst),
        in_specs=[
            pl.BlockSpec((1, ts, d_attn), lambda b, s: (b, s, 0)),
            pl.BlockSpec((ts, d_model), lambda b, s: (s, b)),
            _const_spec(wo.shape), _const_spec(ln_g.shape), _const_spec(ln_b.shape),
            _const_spec(wr_hi.shape), _const_spec(wr_lo.shape),
        ],
        out_specs=[
            pl.BlockSpec((ts, d_model), lambda b, s: (b * nst + s, 0)),
            pl.BlockSpec((ts, V7X_LANES), lambda b, s: (b * nst + s, 0)),
        ],
        out_shape=[
            jax.ShapeDtypeStruct((n_batch * seq, d_model), F32),
            jax.ShapeDtypeStruct((n_batch * seq, V7X_LANES), F32),
        ],
        compiler_params=pltpu.CompilerParams(
            dimension_semantics=("parallel", "parallel"), vmem_limit_bytes=V7X_VMEM_LIMIT),
        name="oproj_router",
    )(o, x_sbd, wo, ln_g, ln_b, wr_hi, wr_lo)


def _moe_kernel(te_ref, nu_ref, tok_ref, dst_ref, x_hbm, wg_ref, wu_ref, wo_ref, y_hbm,
                xs_ref, ob_ref, xb_ref, gsem, ssem, *, tm, nf, sub, n_rows):
    del te_ref
    s = pl.program_id(0)
    f = pl.program_id(1)
    rows_per_chunk = tm // nf
    gslot = s % 2
    cslot = 1 - gslot
    valid = (s >= 1) & (s <= nu_ref[0])

    def issue_row_copies(r_lo, r_hi):
        for r in range(r_lo, r_hi):
            row = f * rows_per_chunk + r
            tok = tok_ref[0, 0, row]
            dst = jnp.where(s >= 2, dst_ref[0, 0, row], n_rows + row)
            pltpu.make_async_copy(
                x_hbm.at[pl.ds(tok, 1)], xs_ref.at[gslot, pl.ds(row, 1)], gsem).start()
            pltpu.make_async_copy(
                ob_ref.at[gslot, pl.ds(row, 1)], y_hbm.at[pl.ds(dst, 1)], ssem).start()

    def wait_tile_rows(one_row_copy):
        unroll = 8

        def body(c, carry):
            for _ in range(unroll):
                one_row_copy.wait()
            return carry
        lax.fori_loop(0, tm // unroll, body, 0)

    gather_row = pltpu.make_async_copy(x_hbm.at[pl.ds(0, 1)], xs_ref.at[0, pl.ds(0, 1)], gsem)
    scatter_row = pltpu.make_async_copy(ob_ref.at[0, pl.ds(0, 1)], y_hbm.at[pl.ds(0, 1)], ssem)

    @pl.when((s == 0) & (f == 0))
    def _():
        ob_ref[...] = jnp.zeros_like(ob_ref)

    @pl.when((s >= 1) & (f == 0))
    def _():
        wait_tile_rows(gather_row)
        xb_ref[...] = xs_ref[cslot].astype(BF16)

    @pl.when(valid)
    def _():
        fc = wo_ref.shape[1]
        n_sub = fc // sub
        xb = xb_ref[...]
        for j in range(n_sub):
            issue_row_copies(rows_per_chunk * j // n_sub, rows_per_chunk * (j + 1) // n_sub)
            gate = jnp.dot(xb, wg_ref[0, :, j * sub:(j + 1) * sub], preferred_element_type=F32)
            up = jnp.dot(xb, wu_ref[0, :, j * sub:(j + 1) * sub], preferred_element_type=F32)
            h = (gate * jax.nn.sigmoid(gate) * up).astype(BF16)
            part = jnp.dot(h, wo_ref[0, j * sub:(j + 1) * sub, :], preferred_element_type=F32)
            prev = ob_ref[cslot]
            if j == 0:
                prev = jnp.where(f == 0, 0.0, prev)
            ob_ref[cslot] = prev + part

    @pl.when(jnp.logical_not(valid))
    def _():
        issue_row_copies(0, rows_per_chunk)

    @pl.when(f == nf - 1)
    def _():
        wait_tile_rows(scatter_row)

    @pl.when((s == pl.num_programs(0) - 1) & (f == nf - 1))
    def _():
        wait_tile_rows(gather_row)


def _moe_experts(tile_expert, n_used, tok_idx, dst_idx, x, w_in, w_out, *, tm, fc):
    n_tok, d_model = x.shape
    edim = w_out.shape[1]
    nf = edim // fc
    n_tiles = tok_idx.shape[0]
    n_rows = n_tiles * tm
    n_steps = n_tiles + 2

    def expert_chunk(s, f, te, nu):
        tile = jnp.clip(s - 1, 0, nu[0] - 1)
        chunk = jnp.where(s > nu[0], nf - 1, jnp.where(s < 1, 0, f))
        return te[tile], chunk

    def w_in_map(half):
        def index(s, f, te, nu):
            e, chunk = expert_chunk(s, f, te, nu)
            return e, 0, chunk + half * nf
        return index

    def w_out_map(s, f, te, nu):
        e, chunk = expert_chunk(s, f, te, nu)
        return e, chunk, 0

    grid_spec = pltpu.PrefetchScalarGridSpec(
        num_scalar_prefetch=2,
        grid=(n_steps, nf),
        in_specs=[
            pl.BlockSpec((1, 1, tm), lambda s, f, te, nu: (jnp.minimum(s, n_tiles - 1), 0, 0),
                         memory_space=pltpu.SMEM),
            pl.BlockSpec((1, 1, tm), lambda s, f, te, nu: (jnp.clip(s - 2, 0, n_tiles - 1), 0, 0),
                         memory_space=pltpu.SMEM),
            pl.BlockSpec(memory_space=pl.ANY),
            pl.BlockSpec((1, d_model, fc), w_in_map(0)),
            pl.BlockSpec((1, d_model, fc), w_in_map(1)),
            pl.BlockSpec((1, fc, d_model), w_out_map),
        ],
        out_specs=pl.BlockSpec(memory_space=pl.ANY),
        scratch_shapes=[
            pltpu.VMEM((2, tm, d_model), F32), pltpu.VMEM((2, tm, d_model), F32),
            pltpu.VMEM((tm, d_model), BF16),
            pltpu.SemaphoreType.DMA(()), pltpu.SemaphoreType.DMA(()),
        ],
    )
    kern = functools.partial(_moe_kernel, tm=tm, nf=nf, sub=fc // 2, n_rows=n_rows)
    return pl.pallas_call(
        kern,
        grid_spec=grid_spec,
        out_shape=jax.ShapeDtypeStruct((n_rows + tm, d_model), F32),
        compiler_params=pltpu.CompilerParams(
            dimension_semantics=("arbitrary", "arbitrary"), vmem_limit_bytes=V7X_VMEM_LIMIT),
        name="moe_experts",
    )(tile_expert, n_used, tok_idx.reshape(n_tiles, 1, tm), dst_idx.reshape(n_tiles, 1, tm),
      x, w_in, w_in, w_out)


def _combine_kernel(x_ref, info_ref, y0_ref, y1_ref, g_ref, b_ref, o_ref):
    info = info_ref[...]
    ffn = y0_ref[...] * info[:, 2:3] + y1_ref[...] * info[:, 3:4]
    o_ref[...] = _layer_norm(ALPHA * x_ref[...] + ffn, g_ref[...], b_ref[...])


def _combine(x, info, ys, ln_g, ln_b, *, tc):
    n_tok, d_model = x.shape
    nt = n_tok // tc
    return pl.pallas_call(
        _combine_kernel,
        grid=(nt,),
        in_specs=[
            pl.BlockSpec((tc, d_model), lambda i: (i, 0)),
            pl.BlockSpec((tc, V7X_LANES), lambda i: (i, 0)),
            pl.BlockSpec((tc, d_model), lambda i: (i, 0)),
            pl.BlockSpec((tc, d_model), lambda i: (i + nt, 0)),
            _const_spec(ln_g.shape), _const_spec(ln_b.shape),
        ],
        out_specs=pl.BlockSpec((tc, d_model), lambda i: (i, 0)),
        out_shape=jax.ShapeDtypeStruct((n_tok, d_model), F32),
        compiler_params=pltpu.CompilerParams(
            dimension_semantics=("parallel",), vmem_limit_bytes=V7X_VMEM_LIMIT),
        name="moe_combine",
    )(x, info, ys, ys, ln_g, ln_b)


def _s5_params(lam_re, lam_im, log_step, b_re, b_im, c_re, c_im):
    n_groups, n_state = lam_re.shape
    dt = jnp.exp(log_step)[:, None]
    mag = jnp.exp(lam_re * dt)
    lb_re = mag * jnp.cos(lam_im * dt)
    lb_im = mag * jnp.sin(lam_im * dt)
    den = lam_re * lam_re + lam_im * lam_im
    f_re = ((lb_re - 1.0) * lam_re + lb_im * lam_im) / den
    f_im = (lb_im * lam_re - (lb_re - 1.0) * lam_im) / den
    bb_re = f_re[..., None] * b_re - f_im[..., None] * b_im
    bb_im = f_re[..., None] * b_im + f_im[..., None] * b_re
    gpb = S5_BLOCK_GROUPS
    n_blocks = n_groups // gpb
    eye = jnp.eye(gpb, dtype=F32)

    def in_blocks(w):
        w = w.transpose(0, 2, 1).reshape(n_blocks, gpb, S5_GROUP, n_state)
        return jnp.einsum('kgcp,gh->kgchp', w, eye).reshape(n_blocks, gpb * S5_GROUP, gpb * n_state)

    def out_blocks(w):
        w = w.reshape(n_blocks, gpb, S5_GROUP, n_state)
        return jnp.einsum('kgcp,gh->kgphc', w, eye).reshape(n_blocks, gpb * n_state, gpb * S5_GROUP)

    bb = jnp.concatenate([in_blocks(bb_re), in_blocks(bb_im)], axis=-1).astype(BF16)
    cc = jnp.concatenate([out_blocks(c_re), out_blocks(-c_im)], axis=1).astype(BF16)
    return bb, cc, lb_re.reshape(1, -1), lb_im.reshape(1, -1)


def _rope_lane_tables(seq):
    pos = jnp.arange(seq, dtype=F32)
    inv_freq = ROPE_THETA ** (-jnp.arange(0, QK_ROPE, 2, dtype=F32) / QK_ROPE)
    ang = pos[:, None] * inv_freq[None, :]
    cos, sin = jnp.cos(ang), jnp.sin(ang)
    pad = jnp.zeros((seq, V7X_LANES - QK_ROPE), F32)
    return (jnp.concatenate([cos, cos, pad], axis=-1),
            jnp.concatenate([-sin, sin, pad], axis=-1))


def _rope_weight_blocks(w_rope):
    half = QK_ROPE // 2
    t1, t2 = w_rope[:, :half], w_rope[:, half:]
    pad = jnp.zeros((w_rope.shape[0], V7X_LANES - QK_ROPE), w_rope.dtype)
    return jnp.concatenate([t1, t2, pad, t2, t1, pad], axis=-1)


def _routing(info, *, tm):
    n_tok = info.shape[0]
    n_assign = n_tok * TOP_K
    flat_e = info[:, :TOP_K].astype(jnp.int32).reshape(-1)
    experts = jnp.arange(N_EXPERTS, dtype=jnp.int32)
    counts = jnp.sum((flat_e[:, None] == experts[None, :]).astype(jnp.int32), axis=0)
    pad = (-counts) % tm
    filler = jnp.arange(N_EXPERTS * tm, dtype=jnp.int32)
    filler_key = jnp.where(filler % tm < pad[filler // tm], filler // tm, N_EXPERTS)
    keys = jnp.concatenate([flat_e, filler_key])
    ids = jnp.arange(keys.shape[0], dtype=jnp.int32)
    sorted_keys, order = lax.sort((keys, ids), num_keys=1, is_stable=True)
    is_real = order < n_assign
    tok_idx = jnp.where(is_real, order // TOP_K, 0)
    dst_idx = jnp.where(is_real, (order % TOP_K) * n_tok + order // TOP_K, order)
    tile_key = sorted_keys[::tm]
    tile_expert = jnp.minimum(tile_key, N_EXPERTS - 1)
    n_used = jnp.sum((tile_key < N_EXPERTS).astype(jnp.int32)).reshape(1)
    return tok_idx.reshape(-1, tm), dst_idx.reshape(-1, tm), tile_expert, n_used


def kernel(x, s5_lam_re, s5_lam_im, s5_log_step, s5_b_re, s5_b_im, s5_c_re, s5_c_im, s5_d, s5_w_glu,
           mla_q_w_a, mla_q_norm, mla_q_w_b, mla_o_w, kv_w_a, kv_norm, kv_w_b, ffn_w_in, ffn_w_out,
           moe_router, moe_w_in, moe_w_out, ln_g, ln_b):
    n_batch, seq, d_model = x.shape
    n_tok = n_batch * seq
    row = lambda v: v.reshape(1, -1).astype(F32)

    bb, cc, a_re, a_im = _s5_params(s5_lam_re[0], s5_lam_im[0], s5_log_step[0], s5_b_re[0], s5_b_im[0],
                                    s5_c_re[0], s5_c_im[0])
    xt = x.transpose(1, 0, 2).reshape(n_tok, d_model)
    x1 = _s5_layer(xt, bb, cc, a_re, a_im, row(s5_d[0]), s5_w_glu[0].astype(BF16),
                   row(ln_g[0, 0]), row(ln_b[0, 0]), n_batch=n_batch, n_time=min(32, seq))
    x2 = _dense_ffn(x1, ffn_w_in[0].astype(BF16), ffn_w_out[0].astype(BF16),
                    row(ln_g[0, 1]), row(ln_b[0, 1]), tm=512, fc=ffn_w_out.shape[1] // 2)
    x2 = x2.reshape(seq, n_batch * d_model)

    kv_lora = kv_norm.shape[0]
    wkva = jnp.concatenate([kv_w_a[:, :kv_lora], _rope_weight_blocks(kv_w_a[:, kv_lora:])],
                           axis=-1).astype(BF16)
    wq = mla_q_w_b[0].reshape(-1, MLA_HEADS, QK_NOPE + QK_ROPE)
    wqb = jnp.concatenate(
        [jnp.concatenate([wq[:, h, :QK_NOPE], _rope_weight_blocks(wq[:, h, QK_NOPE:])], axis=-1)
         for h in range(MLA_HEADS)], axis=-1).astype(BF16)
    cosx, sinx = _rope_lane_tables(seq)
    ts = min(512, seq)
    q, k, v = _qkv_proj(x2, wkva, row(kv_norm), kv_w_b.astype(BF16), mla_q_w_a[0].astype(BF16),
                        row(mla_q_norm[0]), wqb, cosx, sinx, ts=ts, n_batch=n_batch)
    o = _attention(q, k, v, tq=min(512, seq))
    wr = jnp.pad(moe_router[0], ((0, 0), (0, V7X_LANES - N_EXPERTS)))
    wr_hi = wr.astype(BF16)
    wr_lo = (wr - wr_hi.astype(F32)).astype(BF16)
    x3, info = _oproj_router(o, x2, mla_o_w[0].astype(BF16), row(ln_g[1, 0]), row(ln_b[1, 0]),
                             wr_hi, wr_lo, ts=ts)

    tok_idx, dst_idx, tile_expert, n_used = _routing(info, tm=512)
    ys = _moe_experts(tile_expert, n_used, tok_idx, dst_idx, x3, moe_w_in[0].astype(BF16),
                      moe_w_out[0].astype(BF16), tm=512, fc=moe_w_out.shape[2] // 2)
    out = _combine(x3, info, ys, row(ln_g[1, 1]), row(ln_b[1, 1]), tc=min(512, n_tok))
    return out.reshape(n_batch, seq, d_model)
```

```python
import functools
import math

import jax
import jax.numpy as jnp
from jax import lax
from jax.experimental import pallas as pl
from jax.experimental.pallas import tpu as pltpu

F32 = jnp.float32
BF16 = jnp.bfloat16

V7X_LANES = 128
V7X_SUBLANES = 8
V7X_VMEM_LIMIT = 56 * 1024 * 1024

DEPTH = 2
ALPHA = (2.0 * DEPTH) ** 0.25
LN_EPS = 1e-5
RMS_EPS = 1e-6
ROPE_THETA = 10000.0

S5_GROUP = 16
S5_STATE = 64
S5_BLOCK_GROUPS = 16

MLA_HEADS = 8
QK_NOPE = 128
QK_ROPE = 64
V_DIM = 128
N_EXPERTS = 8
TOP_K = 2


def _const_spec(shape):
    zeros = (0,) * len(shape)
    return pl.BlockSpec(shape, lambda *_: zeros, pipeline_mode=pl.Buffered(1))


def _layer_norm(h, g, b):
    mu = jnp.mean(h, axis=-1, keepdims=True)
    c = h - mu
    var = jnp.mean(c * c, axis=-1, keepdims=True)
    return c * lax.rsqrt(var + LN_EPS) * g + b


def _gelu_tanh(y):
    return 0.5 * y * (1.0 + jnp.tanh(math.sqrt(2.0 / math.pi) * (y + 0.044715 * (y * y * y))))


def _s5_kernel(x_ref, bb_ref, cc_ref, are_ref, aim_ref, d_ref, wglu_ref, g_ref, b_ref,
               o_ref, bu_ref, st_ref, *, n_batch, n_time, n_blocks):
    half = bb_ref.shape[2] // 2
    cb = bb_ref.shape[1]
    slab = 512
    d_model = x_ref.shape[1]

    @pl.when(pl.program_id(0) == 0)
    def _():
        st_ref[...] = jnp.zeros_like(st_ref)

    x = x_ref[...]
    xb = x.astype(BF16)
    for k in range(n_blocks):
        bu_ref[:, k * 2 * half:(k + 1) * 2 * half] = jnp.dot(
            xb[:, k * cb:(k + 1) * cb], bb_ref[k], preferred_element_type=F32)

    for k in range(n_blocks):
        for j in range(half // slab):
            re0 = k * 2 * half + j * slab
            im0 = re0 + half
            a0 = k * half + j * slab
            for bh in range(n_batch // V7X_SUBLANES):
                r0 = bh * V7X_SUBLANES
                ar = jnp.broadcast_to(are_ref[:, a0:a0 + slab], (V7X_SUBLANES, slab))
                ai = jnp.broadcast_to(aim_ref[:, a0:a0 + slab], (V7X_SUBLANES, slab))
                s_re = st_ref[r0:r0 + V7X_SUBLANES, re0:re0 + slab]
                s_im = st_ref[r0:r0 + V7X_SUBLANES, im0:im0 + slab]

                def step(t, carry, re0=re0, im0=im0, r0=r0, ar=ar, ai=ai):
                    s_re, s_im = carry
                    row = pl.multiple_of(t * n_batch + r0, V7X_SUBLANES)
                    b_re = bu_ref[pl.ds(row, V7X_SUBLANES), re0:re0 + slab]
                    b_im = bu_ref[pl.ds(row, V7X_SUBLANES), im0:im0 + slab]
                    n_re = ar * s_re - ai * s_im + b_re
                    n_im = ar * s_im + ai * s_re + b_im
                    bu_ref[pl.ds(row, V7X_SUBLANES), re0:re0 + slab] = n_re
                    bu_ref[pl.ds(row, V7X_SUBLANES), im0:im0 + slab] = n_im
                    return n_re, n_im

                s_re, s_im = lax.fori_loop(0, n_time, step, (s_re, s_im), unroll=4)
                st_ref[r0:r0 + V7X_SUBLANES, re0:re0 + slab] = s_re
                st_ref[r0:r0 + V7X_SUBLANES, im0:im0 + slab] = s_im

    ys = []
    for k in range(n_blocks):
        s_blk = bu_ref[:, k * 2 * half:(k + 1) * 2 * half].astype(BF16)
        ys.append(jnp.dot(s_blk, cc_ref[k], preferred_element_type=F32))
    y = jnp.concatenate(ys, axis=-1) + d_ref[...] * x
    act = _gelu_tanh(y).astype(BF16)
    z = jnp.dot(act, wglu_ref[...], preferred_element_type=F32)
    mix = z[:, :d_model] * jax.nn.sigmoid(z[:, d_model:])
    o_ref[...] = _layer_norm(ALPHA * x + mix, g_ref[...], b_ref[...])


def _s5_layer(xt, bb, cc, a_re, a_im, d_skip, w_glu, ln_g, ln_b, *, n_batch, n_time):
    n_rows, d_model = xt.shape
    n_blocks = bb.shape[0]
    rows = n_time * n_batch
    state_w = n_blocks * bb.shape[2]
    kern = functools.partial(_s5_kernel, n_batch=n_batch, n_time=n_time, n_blocks=n_blocks)
    return pl.pallas_call(
        kern,
        grid=(n_rows // rows,),
        in_specs=[
            pl.BlockSpec((rows, d_model), lambda i: (i, 0)),
            _const_spec(bb.shape), _const_spec(cc.shape),
            _const_spec(a_re.shape), _const_spec(a_im.shape), _const_spec(d_skip.shape),
            _const_spec(w_glu.shape), _const_spec(ln_g.shape), _const_spec(ln_b.shape),
        ],
        out_specs=pl.BlockSpec((rows, d_model), lambda i: (i, 0)),
        out_shape=jax.ShapeDtypeStruct((n_rows, d_model), F32),
        scratch_shapes=[pltpu.VMEM((rows, state_w), F32), pltpu.VMEM((n_batch, state_w), F32)],
        compiler_params=pltpu.CompilerParams(
            dimension_semantics=("arbitrary",), vmem_limit_bytes=V7X_VMEM_LIMIT),
        name="s5_mixer",
    )(xt, bb, cc, a_re, a_im, d_skip, w_glu, ln_g, ln_b)


def _ffn_kernel(x_ref, wg_ref, wu_ref, wo_ref, g_ref, b_ref, o_ref, *, fc):
    x = x_ref[...]
    xb = x.astype(BF16)
    acc = None
    for j in range(wo_ref.shape[0] // fc):
        gate = jnp.dot(xb, wg_ref[:, j * fc:(j + 1) * fc], preferred_element_type=F32)
        up = jnp.dot(xb, wu_ref[:, j * fc:(j + 1) * fc], preferred_element_type=F32)
        h = (gate * jax.nn.sigmoid(gate) * up).astype(BF16)
        part = jnp.dot(h, wo_ref[j * fc:(j + 1) * fc, :], preferred_element_type=F32)
        acc = part if acc is None else acc + part
    o_ref[...] = _layer_norm(ALPHA * x + acc, g_ref[...], b_ref[...])


def _dense_ffn(x, w_in, w_out, ln_g, ln_b, *, tm, fc):
    n_rows, d_model = x.shape
    ffn = w_out.shape[0]
    kern = functools.partial(_ffn_kernel, fc=fc)
    return pl.pallas_call(
        kern,
        grid=(n_rows // tm,),
        in_specs=[
            pl.BlockSpec((tm, d_model), lambda i: (i, 0)),
            pl.BlockSpec((d_model, ffn), lambda i: (0, 0), pipeline_mode=pl.Buffered(1)),
            pl.BlockSpec((d_model, ffn), lambda i: (0, 1), pipeline_mode=pl.Buffered(1)),
            _const_spec(w_out.shape), _const_spec(ln_g.shape), _const_spec(ln_b.shape),
        ],
        out_specs=pl.BlockSpec((tm, d_model), lambda i: (i, 0)),
        out_shape=jax.ShapeDtypeStruct((n_rows, d_model), F32),
        compiler_params=pltpu.CompilerParams(
            dimension_semantics=("parallel",), vmem_limit_bytes=V7X_VMEM_LIMIT),
        name="dense_ffn",
    )(x, w_in, w_in, w_out, ln_g, ln_b)


def _qkv_kernel(x_ref, wkva_ref, kvn_ref, wkvb_ref, wqa_ref, qn_ref, wqb_ref, cos_ref, sin_ref,
                q_ref, k_ref, v_ref, *, scale):
    xb = x_ref[...].astype(BF16)
    cosx = cos_ref[...]
    sinx = sin_ref[...]

    kva = jnp.dot(xb, wkva_ref[...], preferred_element_type=F32)
    ckv = kva[:, :V7X_LANES]
    ckv = ckv * lax.rsqrt(jnp.mean(ckv * ckv, axis=-1, keepdims=True) + RMS_EPS) * kvn_ref[...]
    k_rope = (kva[:, V7X_LANES:2 * V7X_LANES] * cosx
              + kva[:, 2 * V7X_LANES:3 * V7X_LANES] * sinx).astype(BF16)
    kv = jnp.dot(ckv.astype(BF16), wkvb_ref[...], preferred_element_type=F32)

    cq = jnp.dot(xb, wqa_ref[...], preferred_element_type=F32)
    cq = cq * lax.rsqrt(jnp.mean(cq * cq, axis=-1, keepdims=True) + RMS_EPS) * qn_ref[...]
    q = jnp.dot(cq.astype(BF16), wqb_ref[...], preferred_element_type=F32)

    for h in range(MLA_HEADS):
        kb = h * (QK_NOPE + V_DIM)
        k_ref[0, h, :, :QK_NOPE] = kv[:, kb:kb + QK_NOPE].astype(BF16)
        k_ref[0, h, :, QK_NOPE:] = k_rope
        v_ref[0, h] = kv[:, kb + QK_NOPE:kb + QK_NOPE + V_DIM].astype(BF16)
        qb = h * 3 * V7X_LANES
        q_ref[0, h, :, :QK_NOPE] = (q[:, qb:qb + QK_NOPE] * scale).astype(BF16)
        q_rope = (q[:, qb + V7X_LANES:qb + 2 * V7X_LANES] * cosx
                  + q[:, qb + 2 * V7X_LANES:qb + 3 * V7X_LANES] * sinx)
        q_ref[0, h, :, QK_NOPE:] = (q_rope * scale).astype(BF16)


def _qkv_proj(x_sbd, wkva, kvn, wkvb, wqa, qn, wqb, cosx, sinx, *, ts, n_batch):
    seq = x_sbd.shape[0]
    d_model = x_sbd.shape[1] // n_batch
    scale = 1.0 / math.sqrt(QK_NOPE + QK_ROPE)
    dk = QK_NOPE + V7X_LANES
    kern = functools.partial(_qkv_kernel, scale=scale)
    return pl.pallas_call(
        kern,
        grid=(n_batch, seq // ts),
        in_specs=[
            pl.BlockSpec((ts, d_model), lambda b, s: (s, b)),
            _const_spec(wkva.shape), _const_spec(kvn.shape), _const_spec(wkvb.shape),
            _const_spec(wqa.shape), _const_spec(qn.shape), _const_spec(wqb.shape),
            pl.BlockSpec((ts, V7X_LANES), lambda b, s: (s, 0)),
            pl.BlockSpec((ts, V7X_LANES), lambda b, s: (s, 0)),
        ],
        out_specs=[
            pl.BlockSpec((1, MLA_HEADS, ts, dk), lambda b, s: (b, 0, s, 0)),
            pl.BlockSpec((1, MLA_HEADS, ts, dk), lambda b, s: (b, 0, s, 0)),
            pl.BlockSpec((1, MLA_HEADS, ts, V_DIM), lambda b, s: (b, 0, s, 0)),
        ],
        out_shape=[
            jax.ShapeDtypeStruct((n_batch, MLA_HEADS, seq, dk), BF16),
            jax.ShapeDtypeStruct((n_batch, MLA_HEADS, seq, dk), BF16),
            jax.ShapeDtypeStruct((n_batch, MLA_HEADS, seq, V_DIM), BF16),
        ],
        compiler_params=pltpu.CompilerParams(
            dimension_semantics=("parallel", "parallel"), vmem_limit_bytes=V7X_VMEM_LIMIT),
        name="qkv_proj",
    )(x_sbd, wkva, kvn, wkvb, wqa, qn, wqb, cosx, sinx)


def _attn_kernel(q_ref, k_ref, v_ref, o_ref, *, tq):
    seq = q_ref.shape[2]
    n_tiles = seq // tq
    row = lax.broadcasted_iota(jnp.int32, (tq, tq), 0)
    col = lax.broadcasted_iota(jnp.int32, (tq, tq), 1)
    causal = col <= row
    for qi in range(n_tiles):
        q = q_ref[0, 0, qi * tq:(qi + 1) * tq, :]
        m = l = acc = None
        for kj in range(qi + 1):
            k = k_ref[0, 0, kj * tq:(kj + 1) * tq, :]
            v = v_ref[0, 0, kj * tq:(kj + 1) * tq, :]
            s = lax.dot_general(q, k, (((1,), (1,)), ((), ())), preferred_element_type=F32)
            if kj == qi:
                s = jnp.where(causal, s, -jnp.inf)
            m_new = jnp.max(s, axis=-1, keepdims=True)
            if kj > 0:
                m_new = jnp.maximum(m, m_new)
            p = jnp.exp(s - m_new)
            pv = jnp.dot(p.astype(BF16), v, preferred_element_type=F32)
            if kj == 0:
                l = jnp.sum(p, axis=-1, keepdims=True)
                acc = pv
            else:
                corr = jnp.exp(m - m_new)
                l = corr * l + jnp.sum(p, axis=-1, keepdims=True)
                acc = corr * acc + pv
            m = m_new
        o_ref[0, qi * tq:(qi + 1) * tq, :] = (acc / l).astype(o_ref.dtype)


def _attention(q, k, v, *, tq):
    n_batch, n_heads, seq, dk = q.shape
    dv = v.shape[3]
    kern = functools.partial(_attn_kernel, tq=tq)
    return pl.pallas_call(
        kern,
        grid=(n_batch, n_heads),
        in_specs=[
            pl.BlockSpec((1, 1, seq, dk), lambda b, h: (b, h, 0, 0)),
            pl.BlockSpec((1, 1, seq, dk), lambda b, h: (b, h, 0, 0)),
            pl.BlockSpec((1, 1, seq, dv), lambda b, h: (b, h, 0, 0)),
        ],
        out_specs=pl.BlockSpec((1, seq, dv), lambda b, h: (b, 0, h)),
        out_shape=jax.ShapeDtypeStruct((n_batch, seq, n_heads * dv), BF16),
        compiler_params=pltpu.CompilerParams(
            dimension_semantics=("parallel", "parallel"), vmem_limit_bytes=V7X_VMEM_LIMIT),
        name="mla_attention",
    )(q, k, v)


def _oproj_router_kernel(o_ref, x_ref, wo_ref, g_ref, b_ref, wrh_ref, wrl_ref, y_ref, info_ref):
    half = y_ref.shape[0] // 2
    for r in range(2):
        rows = pl.ds(r * half, half)
        y, info = _oproj_router_rows(o_ref[0, rows, :], x_ref[rows, :], wo_ref, g_ref, b_ref,
                                     wrh_ref, wrl_ref)
        y_ref[rows, :] = y
        info_ref[rows, :] = info


def _oproj_router_rows(o, x, wo_ref, g_ref, b_ref, wrh_ref, wrl_ref):
    mix = jnp.dot(o, wo_ref[...], preferred_element_type=F32)
    y = _layer_norm(ALPHA * x + mix, g_ref[...], b_ref[...])

    y_hi = y.astype(BF16)
    y_lo = (y - y_hi.astype(F32)).astype(BF16)
    logits = (jnp.dot(y_hi, wrh_ref[...], preferred_element_type=F32)
              + jnp.dot(y_lo, wrh_ref[...], preferred_element_type=F32)
              + jnp.dot(y_hi, wrl_ref[...], preferred_element_type=F32))
    lane = lax.broadcasted_iota(jnp.int32, logits.shape, 1)
    logits = jnp.where(lane < N_EXPERTS, logits, -jnp.inf)
    m1 = jnp.max(logits, axis=-1, keepdims=True)
    i1 = jnp.min(jnp.where(logits == m1, lane, V7X_LANES), axis=-1, keepdims=True)
    rest = jnp.where(lane == i1, -jnp.inf, logits)
    m2 = jnp.max(rest, axis=-1, keepdims=True)
    i2 = jnp.min(jnp.where(rest == m2, lane, V7X_LANES), axis=-1, keepdims=True)
    e2 = jnp.exp(m2 - m1)
    den = 1.0 + e2
    info = jnp.where(lane == 0, i1.astype(F32),
                     jnp.where(lane == 1, i2.astype(F32),
                               jnp.where(lane == 2, 1.0 / den,
                                         jnp.where(lane == 3, e2 / den, 0.0))))
    return y, info


def _oproj_router(o, x_sbd, wo, ln_g, ln_b, wr_hi, wr_lo, *, ts):
    n_batch, seq, d_attn = o.shape
    d_model = x_sbd.shape[1] // n_batch
    nst = seq // ts
    return pl.pallas_call(
        _oproj_router_kernel,
        grid=(n_batch, nst),
        in_specs=[
            pl.BlockSpec((1, ts, d_attn), lambda b, s: (b, s, 0)),
            pl.BlockSpec((ts, d_model), lambda b, s: (s, b)),
            _const_spec(wo.shape), _const_spec(ln_g.shape), _const_spec(ln_b.shape),
            _const_spec(wr_hi.shape), _const_spec(wr_lo.shape),
        ],
        out_specs=[
            pl.BlockSpec((ts, d_model), lambda b, s: (b * nst + s, 0)),
            pl.BlockSpec((ts, V7X_LANES), lambda b, s: (b * nst + s, 0)),
        ],
        out_shape=[
            jax.ShapeDtypeStruct((n_batch * seq, d_model), F32),
            jax.ShapeDtypeStruct((n_batch * seq, V7X_LANES), F32),
        ],
        compiler_params=pltpu.CompilerParams(
            dimension_semantics=("parallel", "parallel"), vmem_limit_bytes=V7X_VMEM_LIMIT),
        name="oproj_router",
    )(o, x_sbd, wo, ln_g, ln_b, wr_hi, wr_lo)


def _moe_kernel(te_ref, nu_ref, tok_ref, dst_ref, x_hbm, wg_ref, wu_ref, wo_ref, y_hbm,
                xs_ref, ob_ref, xb_ref, gsem, ssem, *, tm, nf, n_rows):
    del te_ref
    s = pl.program_id(0)
    f = pl.program_id(1)
    gslot = s % 2
    cslot = 1 - gslot
    valid = (s >= 1) & (s <= nu_ref[0])

    def wait_tile_rows(one_row_copy):
        unroll = 8

        def body(c, carry):
            for _ in range(unroll):
                one_row_copy.wait()
            return carry
        lax.fori_loop(0, tm // unroll, body, 0)

    gather_row = pltpu.make_async_copy(x_hbm.at[pl.ds(0, 1)], xs_ref.at[0, pl.ds(0, 1)], gsem)
    scatter_row = pltpu.make_async_copy(ob_ref.at[0, pl.ds(0, 1)], y_hbm.at[pl.ds(0, 1)], ssem)

    @pl.when((s == 0) & (f == 0))
    def _():
        ob_ref[...] = jnp.zeros_like(ob_ref)

    @pl.when((s >= 1) & (f == 0))
    def _():
        wait_tile_rows(gather_row)
        xb_ref[...] = xs_ref[cslot].astype(BF16)

    @pl.when(f == 0)
    def _():
        for row in range(tm):
            tok = tok_ref[0, 0, row]
            dst = jnp.where(s >= 2, dst_ref[0, 0, row], n_rows + row)
            pltpu.make_async_copy(
                x_hbm.at[pl.ds(tok, 1)], xs_ref.at[gslot, pl.ds(row, 1)], gsem
            ).start(priority=row % 2)
            pltpu.make_async_copy(
                ob_ref.at[gslot, pl.ds(row, 1)], y_hbm.at[pl.ds(dst, 1)], ssem
            ).start(priority=row % 2)

    @pl.when(valid)
    def _():
        xb = xb_ref[...]
        gate = jnp.dot(xb, wg_ref[0], preferred_element_type=F32)
        up = jnp.dot(xb, wu_ref[0], preferred_element_type=F32)
        h = (gate * jax.nn.sigmoid(gate) * up).astype(BF16)
        part = jnp.dot(h, wo_ref[0], preferred_element_type=F32)
        if nf == 1:
            ob_ref[cslot] = part
        else:
            @pl.when(f == 0)
            def _():
                ob_ref[cslot] = part

            @pl.when(f > 0)
            def _():
                ob_ref[cslot] = ob_ref[cslot] + part

    @pl.when(f == nf - 1)
    def _():
        wait_tile_rows(scatter_row)

    @pl.when((s == pl.num_programs(0) - 1) & (f == nf - 1))
    def _():
        wait_tile_rows(gather_row)


def _moe_experts(tile_expert, n_used, tok_idx, dst_idx, x, w_in, w_out, *, tm, fc):
    n_tok, d_model = x.shape
    edim = w_out.shape[1]
    nf = edim // fc
    n_tiles = tok_idx.shape[0]
    n_rows = n_tiles * tm
    n_steps = n_tiles + 2

    def expert_chunk(s, f, te, nu):
        tile = jnp.clip(s - 1, 0, nu[0] - 1)
        chunk = jnp.where(s > nu[0], nf - 1, jnp.where(s < 1, 0, f))
        return te[tile], chunk

    def w_in_map(half):
        def index(s, f, te, nu):
            e, chunk = expert_chunk(s, f, te, nu)
            return e, 0, chunk + half * nf
        return index

    def w_out_map(s, f, te, nu):
        e, chunk = expert_chunk(s, f, te, nu)
        return e, chunk, 0

    grid_spec = pltpu.PrefetchScalarGridSpec(
        num_scalar_prefetch=2,
        grid=(n_steps, nf),
        in_specs=[
            pl.BlockSpec((1, 1, tm), lambda s, f, te, nu: (jnp.minimum(s, n_tiles - 1), 0, 0),
                         memory_space=pltpu.SMEM),
            pl.BlockSpec((1, 1, tm), lambda s, f, te, nu: (jnp.clip(s - 2, 0, n_tiles - 1), 0, 0),
                         memory_space=pltpu.SMEM),
            pl.BlockSpec(memory_space=pl.ANY),
            pl.BlockSpec((1, d_model, fc), w_in_map(0)),
            pl.BlockSpec((1, d_model, fc), w_in_map(1)),
            pl.BlockSpec((1, fc, d_model), w_out_map),
        ],
        out_specs=pl.BlockSpec(memory_space=pl.ANY),
        scratch_shapes=[
            pltpu.VMEM((2, tm, d_model), F32), pltpu.VMEM((2, tm, d_model), F32),
            pltpu.VMEM((tm, d_model), BF16),
            pltpu.SemaphoreType.DMA(()), pltpu.SemaphoreType.DMA(()),
        ],
    )
    kern = functools.partial(_moe_kernel, tm=tm, nf=nf, n_rows=n_rows)
    return pl.pallas_call(
        kern,
        grid_spec=grid_spec,
        out_shape=jax.ShapeDtypeStruct((n_rows + tm, d_model), F32),
        compiler_params=pltpu.CompilerParams(
            dimension_semantics=("arbitrary", "arbitrary"), vmem_limit_bytes=V7X_VMEM_LIMIT),
        name="moe_experts",
    )(tile_expert, n_used, tok_idx.reshape(n_tiles, 1, tm), dst_idx.reshape(n_tiles, 1, tm),
      x, w_in, w_in, w_out)


def _combine_kernel(x_ref, info_ref, y0_ref, y1_ref, g_ref, b_ref, o_ref):
    info = info_ref[...]
    ffn = y0_ref[...] * info[:, 2:3] + y1_ref[...] * info[:, 3:4]
    o_ref[...] = _layer_norm(ALPHA * x_ref[...] + ffn, g_ref[...], b_ref[...])


def _combine(x, info, ys, ln_g, ln_b, *, tc):
    n_tok, d_model = x.shape
    nt = n_tok // tc
    return pl.pallas_call(
        _combine_kernel,
        grid=(nt,),
        in_specs=[
            pl.BlockSpec((tc, d_model), lambda i: (i, 0)),
            pl.BlockSpec((tc, V7X_LANES), lambda i: (i, 0)),
            pl.BlockSpec((tc, d_model), lambda i: (i, 0)),
            pl.BlockSpec((tc, d_model), lambda i: (i + nt, 0)),
            _const_spec(ln_g.shape), _const_spec(ln_b.shape),
        ],
        out_specs=pl.BlockSpec((tc, d_model), lambda i: (i, 0)),
        out_shape=jax.ShapeDtypeStruct((n_tok, d_model), F32),
        compiler_params=pltpu.CompilerParams(
            dimension_semantics=("parallel",), vmem_limit_bytes=V7X_VMEM_LIMIT),
        name="moe_combine",
    )(x, info, ys, ys, ln_g, ln_b)


def _s5_params(lam_re, lam_im, log_step, b_re, b_im, c_re, c_im):
    n_groups, n_state = lam_re.shape
    dt = jnp.exp(log_step)[:, None]
    mag = jnp.exp(lam_re * dt)
    lb_re = mag * jnp.cos(lam_im * dt)
    lb_im = mag * jnp.sin(lam_im * dt)
    den = lam_re * lam_re + lam_im * lam_im
    f_re = ((lb_re - 1.0) * lam_re + lb_im * lam_im) / den
    f_im = (lb_im * lam_re - (lb_re - 1.0) * lam_im) / den
    bb_re = f_re[..., None] * b_re - f_im[..., None] * b_im
    bb_im = f_re[..., None] * b_im + f_im[..., None] * b_re
    gpb = S5_BLOCK_GROUPS
    n_blocks = n_groups // gpb
    eye = jnp.eye(gpb, dtype=F32)

    def in_blocks(w):
        w = w.transpose(0, 2, 1).reshape(n_blocks, gpb, S5_GROUP, n_state)
        return jnp.einsum('kgcp,gh->kgchp', w, eye).reshape(n_blocks, gpb * S5_GROUP, gpb * n_state)

    def out_blocks(w):
        w = w.reshape(n_blocks, gpb, S5_GROUP, n_state)
        return jnp.einsum('kgcp,gh->kgphc', w, eye).reshape(n_blocks, gpb * n_state, gpb * S5_GROUP)

    bb = jnp.concatenate([in_blocks(bb_re), in_blocks(bb_im)], axis=-1).astype(BF16)
    cc = jnp.concatenate([out_blocks(c_re), out_blocks(-c_im)], axis=1).astype(BF16)
    return bb, cc, lb_re.reshape(1, -1), lb_im.reshape(1, -1)


def _rope_lane_tables(seq):
    pos = jnp.arange(seq, dtype=F32)
    inv_freq = ROPE_THETA ** (-jnp.arange(0, QK_ROPE, 2, dtype=F32) / QK_ROPE)
    ang = pos[:, None] * inv_freq[None, :]
    cos, sin = jnp.cos(ang), jnp.sin(ang)
    pad = jnp.zeros((seq, V7X_LANES - QK_ROPE), F32)
    return (jnp.concatenate([cos, cos, pad], axis=-1),
            jnp.concatenate([-sin, sin, pad], axis=-1))


def _rope_weight_blocks(w_rope):
    half = QK_ROPE // 2
    t1, t2 = w_rope[:, :half], w_rope[:, half:]
    pad = jnp.zeros((w_rope.shape[0], V7X_LANES - QK_ROPE), w_rope.dtype)
    return jnp.concatenate([t1, t2, pad, t2, t1, pad], axis=-1)


def _routing(info, *, tm):
    n_tok = info.shape[0]
    n_assign = n_tok * TOP_K
    flat_e = info[:, :TOP_K].astype(jnp.int32).reshape(-1)
    experts = jnp.arange(N_EXPERTS, dtype=jnp.int32)
    counts = jnp.sum((flat_e[:, None] == experts[None, :]).astype(jnp.int32), axis=0)
    pad = (-counts) % tm
    filler = jnp.arange(N_EXPERTS * tm, dtype=jnp.int32)
    filler_key = jnp.where(filler % tm < pad[filler // tm], filler // tm, N_EXPERTS)
    keys = jnp.concatenate([flat_e, filler_key])
    ids = jnp.arange(keys.shape[0], dtype=jnp.int32)
    sorted_keys, order = lax.sort((keys, ids), num_keys=1, is_stable=True)
    is_real = order < n_assign
    tok_idx = jnp.where(is_real, order // TOP_K, 0)
    dst_idx = jnp.where(is_real, (order % TOP_K) * n_tok + order // TOP_K, order)
    tile_key = sorted_keys[::tm]
    tile_expert = jnp.minimum(tile_key, N_EXPERTS - 1)
    n_used = jnp.sum((tile_key < N_EXPERTS).astype(jnp.int32)).reshape(1)
    return tok_idx.reshape(-1, tm), dst_idx.reshape(-1, tm), tile_expert, n_used


def kernel(x, s5_lam_re, s5_lam_im, s5_log_step, s5_b_re, s5_b_im, s5_c_re, s5_c_im, s5_d, s5_w_glu,
           mla_q_w_a, mla_q_norm, mla_q_w_b, mla_o_w, kv_w_a, kv_norm, kv_w_b, ffn_w_in, ffn_w_out,
           moe_router, moe_w_in, moe_w_out, ln_g, ln_b):
    n_batch, seq, d_model = x.shape
    n_tok = n_batch * seq
    row = lambda v: v.reshape(1, -1).astype(F32)

    bb, cc, a_re, a_im = _s5_params(s5_lam_re[0], s5_lam_im[0], s5_log_step[0], s5_b_re[0], s5_b_im[0],
                                    s5_c_re[0], s5_c_im[0])
    xt = x.transpose(1, 0, 2).reshape(n_tok, d_model)
    x1 = _s5_layer(xt, bb, cc, a_re, a_im, row(s5_d[0]), s5_w_glu[0].astype(BF16),
                   row(ln_g[0, 0]), row(ln_b[0, 0]), n_batch=n_batch, n_time=min(32, seq))
    x2 = _dense_ffn(x1, ffn_w_in[0].astype(BF16), ffn_w_out[0].astype(BF16),
                    row(ln_g[0, 1]), row(ln_b[0, 1]), tm=512, fc=ffn_w_out.shape[1] // 2)
    x2 = x2.reshape(seq, n_batch * d_model)

    kv_lora = kv_norm.shape[0]
    wkva = jnp.concatenate([kv_w_a[:, :kv_lora], _rope_weight_blocks(kv_w_a[:, kv_lora:])],
                           axis=-1).astype(BF16)
    wq = mla_q_w_b[0].reshape(-1, MLA_HEADS, QK_NOPE + QK_ROPE)
    wqb = jnp.concatenate(
        [jnp.concatenate([wq[:, h, :QK_NOPE], _rope_weight_blocks(wq[:, h, QK_NOPE:])], axis=-1)
         for h in range(MLA_HEADS)], axis=-1).astype(BF16)
    cosx, sinx = _rope_lane_tables(seq)
    ts = min(512, seq)
    q, k, v = _qkv_proj(x2, wkva, row(kv_norm), kv_w_b.astype(BF16), mla_q_w_a[0].astype(BF16),
                        row(mla_q_norm[0]), wqb, cosx, sinx, ts=ts, n_batch=n_batch)
    o = _attention(q, k, v, tq=min(512, seq))
    wr = jnp.pad(moe_router[0], ((0, 0), (0, V7X_LANES - N_EXPERTS)))
    wr_hi = wr.astype(BF16)
    wr_lo = (wr - wr_hi.astype(F32)).astype(BF16)
    x3, info = _oproj_router(o, x2, mla_o_w[0].astype(BF16), row(ln_g[1, 0]), row(ln_b[1, 0]),
                             wr_hi, wr_lo, ts=ts)

    tok_idx, dst_idx, tile_expert, n_used = _routing(info, tm=512)
    ys = _moe_experts(tile_expert, n_used, tok_idx, dst_idx, x3, moe_w_in[0].astype(BF16),
                      moe_w_out[0].astype(BF16), tm=512, fc=moe_w_out.shape[2] // 2)
    out = _combine(x3, info, ys, row(ln_g[1, 1]), row(ln_b[1, 1]), tc=min(512, n_tok))
    return out.reshape(n_batch, seq, d_model)
```

```python
import functools
import math

import jax
import jax.numpy as jnp
from jax import lax
from jax.experimental import pallas as pl
from jax.experimental.pallas import tpu as pltpu

F32 = jnp.float32
BF16 = jnp.bfloat16

V7X_LANES = 128
V7X_SUBLANES = 8
V7X_VMEM_LIMIT = 56 * 1024 * 1024

DEPTH = 2
ALPHA = (2.0 * DEPTH) ** 0.25
LN_EPS = 1e-5
RMS_EPS = 1e-6
ROPE_THETA = 10000.0

S5_GROUP = 16
S5_STATE = 64
S5_BLOCK_GROUPS = 16

MLA_HEADS = 8
QK_NOPE = 128
QK_ROPE = 64
V_DIM = 128
N_EXPERTS = 8
TOP_K = 2


def _const_spec(shape):
    zeros = (0,) * len(shape)
    return pl.BlockSpec(shape, lambda *_: zeros, pipeline_mode=pl.Buffered(1))


def _layer_norm(h, g, b):
    mu = jnp.mean(h, axis=-1, keepdims=True)
    c = h - mu
    var = jnp.mean(c * c, axis=-1, keepdims=True)
    return c * lax.rsqrt(var + LN_EPS) * g + b


def _gelu_tanh(y):
    return 0.5 * y * (1.0 + jnp.tanh(math.sqrt(2.0 / math.pi) * (y + 0.044715 * (y * y * y))))


def _s5_kernel(x_ref, bb_ref, cc_ref, are_ref, aim_ref, d_ref, wglu_ref, g_ref, b_ref,
               o_ref, bu_ref, st_ref, *, n_batch, n_time, n_blocks):
    half = bb_ref.shape[2] // 2
    cb = bb_ref.shape[1]
    slab = 512
    d_model = x_ref.shape[1]

    @pl.when(pl.program_id(0) == 0)
    def _():
        st_ref[...] = jnp.zeros_like(st_ref)

    x = x_ref[...]
    xb = x.astype(BF16)
    for k in range(n_blocks):
        bu_ref[:, k * 2 * half:(k + 1) * 2 * half] = jnp.dot(
            xb[:, k * cb:(k + 1) * cb], bb_ref[k], preferred_element_type=F32)

    for k in range(n_blocks):
        for j in range(half // slab):
            re0 = k * 2 * half + j * slab
            im0 = re0 + half
            a0 = k * half + j * slab
            for bh in range(n_batch // V7X_SUBLANES):
                r0 = bh * V7X_SUBLANES
                ar = jnp.broadcast_to(are_ref[:, a0:a0 + slab], (V7X_SUBLANES, slab))
                ai = jnp.broadcast_to(aim_ref[:, a0:a0 + slab], (V7X_SUBLANES, slab))
                s_re = st_ref[r0:r0 + V7X_SUBLANES, re0:re0 + slab]
                s_im = st_ref[r0:r0 + V7X_SUBLANES, im0:im0 + slab]

                def step(t, carry, re0=re0, im0=im0, r0=r0, ar=ar, ai=ai):
                    s_re, s_im = carry
                    row = pl.multiple_of(t * n_batch + r0, V7X_SUBLANES)
                    b_re = bu_ref[pl.ds(row, V7X_SUBLANES), re0:re0 + slab]
                    b_im = bu_ref[pl.ds(row, V7X_SUBLANES), im0:im0 + slab]
                    n_re = ar * s_re - ai * s_im + b_re
                    n_im = ar * s_im + ai * s_re + b_im
                    bu_ref[pl.ds(row, V7X_SUBLANES), re0:re0 + slab] = n_re
                    bu_ref[pl.ds(row, V7X_SUBLANES), im0:im0 + slab] = n_im
                    return n_re, n_im

                s_re, s_im = lax.fori_loop(0, n_time, step, (s_re, s_im), unroll=4)
                st_ref[r0:r0 + V7X_SUBLANES, re0:re0 + slab] = s_re
                st_ref[r0:r0 + V7X_SUBLANES, im0:im0 + slab] = s_im

    ys = []
    for k in range(n_blocks):
        s_blk = bu_ref[:, k * 2 * half:(k + 1) * 2 * half].astype(BF16)
        ys.append(jnp.dot(s_blk, cc_ref[k], preferred_element_type=F32))
    y = jnp.concatenate(ys, axis=-1) + d_ref[...] * x
    act = _gelu_tanh(y).astype(BF16)
    z = jnp.dot(act, wglu_ref[...], preferred_element_type=F32)
    mix = z[:, :d_model] * jax.nn.sigmoid(z[:, d_model:])
    o_ref[...] = _layer_norm(ALPHA * x + mix, g_ref[...], b_ref[...])


def _s5_layer(xt, bb, cc, a_re, a_im, d_skip, w_glu, ln_g, ln_b, *, n_batch, n_time):
    n_rows, d_model = xt.shape
    n_blocks = bb.shape[0]
    rows = n_time * n_batch
    state_w = n_blocks * bb.shape[2]
    kern = functools.partial(_s5_kernel, n_batch=n_batch, n_time=n_time, n_blocks=n_blocks)
    return pl.pallas_call(
        kern,
        grid=(n_rows // rows,),
        in_specs=[
            pl.BlockSpec((rows, d_model), lambda i: (i, 0)),
            _const_spec(bb.shape), _const_spec(cc.shape),
            _const_spec(a_re.shape), _const_spec(a_im.shape), _const_spec(d_skip.shape),
            _const_spec(w_glu.shape), _const_spec(ln_g.shape), _const_spec(ln_b.shape),
        ],
        out_specs=pl.BlockSpec((rows, d_model), lambda i: (i, 0)),
        out_shape=jax.ShapeDtypeStruct((n_rows, d_model), F32),
        scratch_shapes=[pltpu.VMEM((rows, state_w), F32), pltpu.VMEM((n_batch, state_w), F32)],
        compiler_params=pltpu.CompilerParams(
            dimension_semantics=("arbitrary",), vmem_limit_bytes=V7X_VMEM_LIMIT),
        name="s5_mixer",
    )(xt, bb, cc, a_re, a_im, d_skip, w_glu, ln_g, ln_b)


def _ffn_kernel(x_ref, wg_ref, wu_ref, wo_ref, g_ref, b_ref, o_ref, *, fc):
    x = x_ref[...]
    xb = x.astype(BF16)
    acc = None
    for j in range(wo_ref.shape[0] // fc):
        gate = jnp.dot(xb, wg_ref[:, j * fc:(j + 1) * fc], preferred_element_type=F32)
        up = jnp.dot(xb, wu_ref[:, j * fc:(j + 1) * fc], preferred_element_type=F32)
        h = (gate * jax.nn.sigmoid(gate) * up).astype(BF16)
        part = jnp.dot(h, wo_ref[j * fc:(j + 1) * fc, :], preferred_element_type=F32)
        acc = part if acc is None else acc + part
    o_ref[...] = _layer_norm(ALPHA * x + acc, g_ref[...], b_ref[...])


def _dense_ffn(x, w_in, w_out, ln_g, ln_b, *, tm, fc):
    n_rows, d_model = x.shape
    ffn = w_out.shape[0]
    kern = functools.partial(_ffn_kernel, fc=fc)
    return pl.pallas_call(
        kern,
        grid=(n_rows // tm,),
        in_specs=[
            pl.BlockSpec((tm, d_model), lambda i: (i, 0)),
            pl.BlockSpec((d_model, ffn), lambda i: (0, 0), pipeline_mode=pl.Buffered(1)),
            pl.BlockSpec((d_model, ffn), lambda i: (0, 1), pipeline_mode=pl.Buffered(1)),
            _const_spec(w_out.shape), _const_spec(ln_g.shape), _const_spec(ln_b.shape),
        ],
        out_specs=pl.BlockSpec((tm, d_model), lambda i: (i, 0)),
        out_shape=jax.ShapeDtypeStruct((n_rows, d_model), F32),
        compiler_params=pltpu.CompilerParams(
            dimension_semantics=("parallel",), vmem_limit_bytes=V7X_VMEM_LIMIT),
        name="dense_ffn",
    )(x, w_in, w_in, w_out, ln_g, ln_b)


def _qkv_kernel(x_ref, wkva_ref, kvn_ref, wkvb_ref, wqa_ref, qn_ref, wqb_ref, cos_ref, sin_ref,
                q_ref, k_ref, v_ref, *, scale):
    xb = x_ref[...].astype(BF16)
    cosx = cos_ref[...]
    sinx = sin_ref[...]

    kva = jnp.dot(xb, wkva_ref[...], preferred_element_type=F32)
    ckv = kva[:, :V7X_LANES]
    ckv = ckv * lax.rsqrt(jnp.mean(ckv * ckv, axis=-1, keepdims=True) + RMS_EPS) * kvn_ref[...]
    k_rope = (kva[:, V7X_LANES:2 * V7X_LANES] * cosx
              + kva[:, 2 * V7X_LANES:3 * V7X_LANES] * sinx).astype(BF16)
    kv = jnp.dot(ckv.astype(BF16), wkvb_ref[...], preferred_element_type=F32)

    cq = jnp.dot(xb, wqa_ref[...], preferred_element_type=F32)
    cq = cq * lax.rsqrt(jnp.mean(cq * cq, axis=-1, keepdims=True) + RMS_EPS) * qn_ref[...]
    q = jnp.dot(cq.astype(BF16), wqb_ref[...], preferred_element_type=F32)

    for h in range(MLA_HEADS):
        kb = h * (QK_NOPE + V_DIM)
        k_ref[0, h, :, :QK_NOPE] = kv[:, kb:kb + QK_NOPE].astype(BF16)
        k_ref[0, h, :, QK_NOPE:] = k_rope
        v_ref[0, h] = kv[:, kb + QK_NOPE:kb + QK_NOPE + V_DIM].astype(BF16)
        qb = h * 3 * V7X_LANES
        q_ref[0, h, :, :QK_NOPE] = (q[:, qb:qb + QK_NOPE] * scale).astype(BF16)
        q_rope = (q[:, qb + V7X_LANES:qb + 2 * V7X_LANES] * cosx
                  + q[:, qb + 2 * V7X_LANES:qb + 3 * V7X_LANES] * sinx)
        q_ref[0, h, :, QK_NOPE:] = (q_rope * scale).astype(BF16)


def _qkv_proj(x_sbd, wkva, kvn, wkvb, wqa, qn, wqb, cosx, sinx, *, ts, n_batch):
    seq = x_sbd.shape[0]
    d_model = x_sbd.shape[1] // n_batch
    scale = 1.0 / math.sqrt(QK_NOPE + QK_ROPE)
    dk = QK_NOPE + V7X_LANES
    kern = functools.partial(_qkv_kernel, scale=scale)
    return pl.pallas_call(
        kern,
        grid=(n_batch, seq // ts),
        in_specs=[
            pl.BlockSpec((ts, d_model), lambda b, s: (s, b)),
            _const_spec(wkva.shape), _const_spec(kvn.shape), _const_spec(wkvb.shape),
            _const_spec(wqa.shape), _const_spec(qn.shape), _const_spec(wqb.shape),
            pl.BlockSpec((ts, V7X_LANES), lambda b, s: (s, 0)),
            pl.BlockSpec((ts, V7X_LANES), lambda b, s: (s, 0)),
        ],
        out_specs=[
            pl.BlockSpec((1, MLA_HEADS, ts, dk), lambda b, s: (b, 0, s, 0)),
            pl.BlockSpec((1, MLA_HEADS, ts, dk), lambda b, s: (b, 0, s, 0)),
            pl.BlockSpec((1, MLA_HEADS, ts, V_DIM), lambda b, s: (b, 0, s, 0)),
        ],
        out_shape=[
            jax.ShapeDtypeStruct((n_batch, MLA_HEADS, seq, dk), BF16),
            jax.ShapeDtypeStruct((n_batch, MLA_HEADS, seq, dk), BF16),
            jax.ShapeDtypeStruct((n_batch, MLA_HEADS, seq, V_DIM), BF16),
        ],
        compiler_params=pltpu.CompilerParams(
            dimension_semantics=("parallel", "parallel"), vmem_limit_bytes=V7X_VMEM_LIMIT),
        name="qkv_proj",
    )(x_sbd, wkva, kvn, wkvb, wqa, qn, wqb, cosx, sinx)


def _attn_kernel(q_ref, k_ref, v_ref, o_ref, *, tq):
    seq = q_ref.shape[2]
    n_tiles = seq // tq
    row = lax.broadcasted_iota(jnp.int32, (tq, tq), 0)
    col = lax.broadcasted_iota(jnp.int32, (tq, tq), 1)
    causal = col <= row
    for qi in range(n_tiles):
        q = q_ref[0, 0, qi * tq:(qi + 1) * tq, :]
        m = l = acc = None
        for kj in range(qi + 1):
            k = k_ref[0, 0, kj * tq:(kj + 1) * tq, :]
            v = v_ref[0, 0, kj * tq:(kj + 1) * tq, :]
            s = lax.dot_general(q, k, (((1,), (1,)), ((), ())), preferred_element_type=F32)
            if kj == qi:
                s = jnp.where(causal, s, -jnp.inf)
            m_new = jnp.max(s, axis=-1, keepdims=True)
            if kj > 0:
                m_new = jnp.maximum(m, m_new)
            p = jnp.exp(s - m_new)
            pv = jnp.dot(p.astype(BF16), v, preferred_element_type=F32)
            if kj == 0:
                l = jnp.sum(p, axis=-1, keepdims=True)
                acc = pv
            else:
                corr = jnp.exp(m - m_new)
                l = corr * l + jnp.sum(p, axis=-1, keepdims=True)
                acc = corr * acc + pv
            m = m_new
        o_ref[0, qi * tq:(qi + 1) * tq, :] = (acc / l).astype(o_ref.dtype)


def _attention(q, k, v, *, tq):
    n_batch, n_heads, seq, dk = q.shape
    dv = v.shape[3]
    kern = functools.partial(_attn_kernel, tq=tq)
    return pl.pallas_call(
        kern,
        grid=(n_batch, n_heads),
        in_specs=[
            pl.BlockSpec((1, 1, seq, dk), lambda b, h: (b, h, 0, 0)),
            pl.BlockSpec((1, 1, seq, dk), lambda b, h: (b, h, 0, 0)),
            pl.BlockSpec((1, 1, seq, dv), lambda b, h: (b, h, 0, 0)),
        ],
        out_specs=pl.BlockSpec((1, seq, dv), lambda b, h: (b, 0, h)),
        out_shape=jax.ShapeDtypeStruct((n_batch, seq, n_heads * dv), BF16),
        compiler_params=pltpu.CompilerParams(
            dimension_semantics=("parallel", "parallel"), vmem_limit_bytes=V7X_VMEM_LIMIT),
        name="mla_attention",
    )(q, k, v)


def _oproj_router_kernel(o_ref, x_ref, wo_ref, g_ref, b_ref, wrh_ref, wrl_ref, y_ref, info_ref):
    half = info_ref.shape[0] // 2
    for r in range(2):
        rows = pl.ds(r * half, half)
        y, info = _oproj_router_rows(o_ref[0, rows, :], x_ref[rows, :], wo_ref, g_ref, b_ref,
                                     wrh_ref, wrl_ref)
        for c in range(V7X_SUBLANES):
            y_ref[pl.ds(r * half * V7X_SUBLANES + c, half, stride=V7X_SUBLANES), :] = (
                y[:, c * V7X_LANES:(c + 1) * V7X_LANES])
        info_ref[rows, :] = info


def _oproj_router_rows(o, x, wo_ref, g_ref, b_ref, wrh_ref, wrl_ref):
    mix = jnp.dot(o, wo_ref[...], preferred_element_type=F32)
    y = _layer_norm(ALPHA * x + mix, g_ref[...], b_ref[...])

    y_hi = y.astype(BF16)
    y_lo = (y - y_hi.astype(F32)).astype(BF16)
    logits = (jnp.dot(y_hi, wrh_ref[...], preferred_element_type=F32)
              + jnp.dot(y_lo, wrh_ref[...], preferred_element_type=F32)
              + jnp.dot(y_hi, wrl_ref[...], preferred_element_type=F32))
    lane = lax.broadcasted_iota(jnp.int32, logits.shape, 1)
    logits = jnp.where(lane < N_EXPERTS, logits, -jnp.inf)
    m1 = jnp.max(logits, axis=-1, keepdims=True)
    i1 = jnp.min(jnp.where(logits == m1, lane, V7X_LANES), axis=-1, keepdims=True)
    rest = jnp.where(lane == i1, -jnp.inf, logits)
    m2 = jnp.max(rest, axis=-1, keepdims=True)
    i2 = jnp.min(jnp.where(rest == m2, lane, V7X_LANES), axis=-1, keepdims=True)
    e2 = jnp.exp(m2 - m1)
    den = 1.0 + e2
    info = jnp.where(lane == 0, i1.astype(F32),
                     jnp.where(lane == 1, i2.astype(F32),
                               jnp.where(lane == 2, 1.0 / den,
                                         jnp.where(lane == 3, e2 / den, 0.0))))
    return y, info


def _oproj_router(o, x_sbd, wo, ln_g, ln_b, wr_hi, wr_lo, *, ts):
    n_batch, seq, d_attn = o.shape
    d_model = x_sbd.shape[1] // n_batch
    assert d_model == V7X_SUBLANES * V7X_LANES
    nst = seq // ts
    return pl.pallas_call(
        _oproj_router_kernel,
        grid=(n_batch, nst),
        in_specs=[
            pl.BlockSpec((1, ts, d_attn), lambda b, s: (b, s, 0)),
            pl.BlockSpec((ts, d_model), lambda b, s: (s, b)),
            _const_spec(wo.shape), _const_spec(ln_g.shape), _const_spec(ln_b.shape),
            _const_spec(wr_hi.shape), _const_spec(wr_lo.shape),
        ],
        out_specs=[
            pl.BlockSpec((ts * V7X_SUBLANES, V7X_LANES), lambda b, s: (b * nst + s, 0)),
            pl.BlockSpec((ts, V7X_LANES), lambda b, s: (b * nst + s, 0)),
        ],
        out_shape=[
            jax.ShapeDtypeStruct((n_batch * seq * V7X_SUBLANES, V7X_LANES), F32),
            jax.ShapeDtypeStruct((n_batch * seq, V7X_LANES), F32),
        ],
        compiler_params=pltpu.CompilerParams(
            dimension_semantics=("parallel", "parallel"), vmem_limit_bytes=V7X_VMEM_LIMIT),
        name="oproj_router",
    )(o, x_sbd, wo, ln_g, ln_b, wr_hi, wr_lo)


def _moe_kernel(te_ref, nu_ref, tok_ref, dst_ref, x_hbm, wg_ref, wu_ref, wo_ref, y_hbm,
                xs_ref, ob_ref, xb_ref, acc_ref, gsem, ssem, *, tm, nf, n_rows):
    del te_ref
    s = pl.program_id(0)
    f = pl.program_id(1)
    gslot = s % 2
    cslot = 1 - gslot
    valid = (s >= 1) & (s <= nu_ref[0])
    sub, lanes = V7X_SUBLANES, V7X_LANES

    def wait_tile_rows(one_row_copy):
        unroll = 8

        def body(c, carry):
            for _ in range(unroll):
                one_row_copy.wait()
            return carry
        lax.fori_loop(0, tm // unroll, body, 0)

    gather_row = pltpu.make_async_copy(x_hbm.at[0], xs_ref.at[pl.ds(0, sub)], gsem)
    scatter_row = pltpu.make_async_copy(ob_ref.at[pl.ds(0, sub)], y_hbm.at[0], ssem)

    @pl.when((s == 0) & (f == 0))
    def _():
        ob_ref[...] = jnp.zeros_like(ob_ref)

    @pl.when((s >= 1) & (f == 0))
    def _():
        wait_tile_rows(gather_row)
        for c in range(sub):
            xb_ref[:, c * lanes:(c + 1) * lanes] = (
                xs_ref[pl.ds(cslot * tm * sub + c, tm, stride=sub), :].astype(BF16))

    for parity in range(2):
        @pl.when((f == 0) & (gslot == parity))
        def _(parity=parity):
            for row in range(tm):
                tok = tok_ref[0, 0, row]
                dst = jnp.where(s >= 2, dst_ref[0, 0, row], n_rows + row)
                vmem_rows = pl.ds((parity * tm + row) * sub, sub)
                pltpu.make_async_copy(
                    x_hbm.at[tok], xs_ref.at[vmem_rows], gsem).start(priority=row % 2)
                pltpu.make_async_copy(
                    ob_ref.at[vmem_rows], y_hbm.at[dst], ssem).start(priority=row % 2)

    @pl.when(valid)
    def _():
        xb = xb_ref[...]
        gate = jnp.dot(xb, wg_ref[0], preferred_element_type=F32)
        up = jnp.dot(xb, wu_ref[0], preferred_element_type=F32)
        h = (gate * jax.nn.sigmoid(gate) * up).astype(BF16)
        part = jnp.dot(h, wo_ref[0], preferred_element_type=F32)

        def store_result(res):
            for c in range(sub):
                ob_ref[pl.ds(cslot * tm * sub + c, tm, stride=sub), :] = (
                    res[:, c * lanes:(c + 1) * lanes])

        if nf == 1:
            store_result(part)
        else:
            @pl.when(f == 0)
            def _():
                acc_ref[...] = part

            @pl.when((f > 0) & (f < nf - 1))
            def _():
                acc_ref[...] += part

            @pl.when(f == nf - 1)
            def _():
                store_result(acc_ref[...] + part)

    @pl.when(f == nf - 1)
    def _():
        wait_tile_rows(scatter_row)

    @pl.when((s == pl.num_programs(0) - 1) & (f == nf - 1))
    def _():
        wait_tile_rows(gather_row)


def _moe_experts(tile_expert, n_used, tok_idx, dst_idx, x_tiles, w_in, w_out, *, tm, fc):
    d_model = x_tiles.shape[1] * x_tiles.shape[2]
    edim = w_out.shape[1]
    nf = edim // fc
    n_tiles = tok_idx.shape[0]
    n_rows = n_tiles * tm
    n_steps = n_tiles + 2

    def expert_chunk(s, f, te, nu):
        tile = jnp.clip(s - 1, 0, nu[0] - 1)
        chunk = jnp.where(s > nu[0], nf - 1, jnp.where(s < 1, 0, f))
        return te[tile], chunk

    def w_in_map(half):
        def index(s, f, te, nu):
            e, chunk = expert_chunk(s, f, te, nu)
            return e, 0, chunk + half * nf
        return index

    def w_out_map(s, f, te, nu):
        e, chunk = expert_chunk(s, f, te, nu)
        return e, chunk, 0

    grid_spec = pltpu.PrefetchScalarGridSpec(
        num_scalar_prefetch=2,
        grid=(n_steps, nf),
        in_specs=[
            pl.BlockSpec((1, 1, tm), lambda s, f, te, nu: (jnp.minimum(s, n_tiles - 1), 0, 0),
                         memory_space=pltpu.SMEM),
            pl.BlockSpec((1, 1, tm), lambda s, f, te, nu: (jnp.clip(s - 2, 0, n_tiles - 1), 0, 0),
                         memory_space=pltpu.SMEM),
            pl.BlockSpec(memory_space=pl.ANY),
            pl.BlockSpec((1, d_model, fc), w_in_map(0)),
            pl.BlockSpec((1, d_model, fc), w_in_map(1)),
            pl.BlockSpec((1, fc, d_model), w_out_map),
        ],
        out_specs=pl.BlockSpec(memory_space=pl.ANY),
        scratch_shapes=[
            pltpu.VMEM((2 * tm * V7X_SUBLANES, V7X_LANES), F32),
            pltpu.VMEM((2 * tm * V7X_SUBLANES, V7X_LANES), F32),
            pltpu.VMEM((tm, d_model), BF16), pltpu.VMEM((tm, d_model), F32),
            pltpu.SemaphoreType.DMA(()), pltpu.SemaphoreType.DMA(()),
        ],
    )
    kern = functools.partial(_moe_kernel, tm=tm, nf=nf, n_rows=n_rows)
    return pl.pallas_call(
        kern,
        grid_spec=grid_spec,
        out_shape=jax.ShapeDtypeStruct((n_rows + tm, V7X_SUBLANES, V7X_LANES), F32),
        compiler_params=pltpu.CompilerParams(
            dimension_semantics=("arbitrary", "arbitrary"), vmem_limit_bytes=V7X_VMEM_LIMIT),
        name="moe_experts",
    )(tile_expert, n_used, tok_idx.reshape(n_tiles, 1, tm), dst_idx.reshape(n_tiles, 1, tm),
      x_tiles, w_in, w_in, w_out)


def _combine_kernel(x_ref, info_ref, y0_ref, y1_ref, g_ref, b_ref, o_ref):
    tc, d_model = o_ref.shape
    sub, lanes = V7X_SUBLANES, V7X_LANES
    info = info_ref[...]
    g0, g1 = info[:, 2:3], info[:, 3:4]
    pieces = []
    for c in range(sub):
        rows = pl.ds(c, tc, stride=sub)
        pieces.append(ALPHA * x_ref[rows, :] + (y0_ref[rows, :] * g0 + y1_ref[rows, :] * g1))
    mu = jnp.sum(sum(pieces), axis=-1, keepdims=True) / d_model
    cen = [p - mu for p in pieces]
    var = jnp.sum(sum(c * c for c in cen), axis=-1, keepdims=True) / d_model
    rstd = lax.rsqrt(var + LN_EPS)
    for c in range(sub):
        cols = slice(c * lanes, (c + 1) * lanes)
        o_ref[:, cols] = cen[c] * rstd * g_ref[:, cols] + b_ref[:, cols]


def _combine(x_tiles, info, y_tiles, ln_g, ln_b, *, tc):
    n_tok = info.shape[0]
    d_model = V7X_SUBLANES * V7X_LANES
    nt = n_tok // tc
    tile_block = (tc * V7X_SUBLANES, V7X_LANES)
    return pl.pallas_call(
        _combine_kernel,
        grid=(nt,),
        in_specs=[
            pl.BlockSpec(tile_block, lambda i: (i, 0)),
            pl.BlockSpec((tc, V7X_LANES), lambda i: (i, 0)),
            pl.BlockSpec(tile_block, lambda i: (i, 0)),
            pl.BlockSpec(tile_block, lambda i: (i + nt, 0)),
            _const_spec(ln_g.shape), _const_spec(ln_b.shape),
        ],
        out_specs=pl.BlockSpec((tc, d_model), lambda i: (i, 0)),
        out_shape=jax.ShapeDtypeStruct((n_tok, d_model), F32),
        compiler_params=pltpu.CompilerParams(
            dimension_semantics=("parallel",), vmem_limit_bytes=V7X_VMEM_LIMIT),
        name="moe_combine",
    )(x_tiles, info, y_tiles, y_tiles, ln_g, ln_b)


def _s5_params(lam_re, lam_im, log_step, b_re, b_im, c_re, c_im):
    n_groups, n_state = lam_re.shape
    dt = jnp.exp(log_step)[:, None]
    mag = jnp.exp(lam_re * dt)
    lb_re = mag * jnp.cos(lam_im * dt)
    lb_im = mag * jnp.sin(lam_im * dt)
    den = lam_re * lam_re + lam_im * lam_im
    f_re = ((lb_re - 1.0) * lam_re + lb_im * lam_im) / den
    f_im = (lb_im * lam_re - (lb_re - 1.0) * lam_im) / den
    bb_re = f_re[..., None] * b_re - f_im[..., None] * b_im
    bb_im = f_re[..., None] * b_im + f_im[..., None] * b_re
    gpb = S5_BLOCK_GROUPS
    n_blocks = n_groups // gpb
    eye = jnp.eye(gpb, dtype=F32)

    def in_blocks(w):
        w = w.transpose(0, 2, 1).reshape(n_blocks, gpb, S5_GROUP, n_state)
        return jnp.einsum('kgcp,gh->kgchp', w, eye).reshape(n_blocks, gpb * S5_GROUP, gpb * n_state)

    def out_blocks(w):
        w = w.reshape(n_blocks, gpb, S5_GROUP, n_state)
        return jnp.einsum('kgcp,gh->kgphc', w, eye).reshape(n_blocks, gpb * n_state, gpb * S5_GROUP)

    bb = jnp.concatenate([in_blocks(bb_re), in_blocks(bb_im)], axis=-1).astype(BF16)
    cc = jnp.concatenate([out_blocks(c_re), out_blocks(-c_im)], axis=1).astype(BF16)
    return bb, cc, lb_re.reshape(1, -1), lb_im.reshape(1, -1)


def _rope_lane_tables(seq):
    pos = jnp.arange(seq, dtype=F32)
    inv_freq = ROPE_THETA ** (-jnp.arange(0, QK_ROPE, 2, dtype=F32) / QK_ROPE)
    ang = pos[:, None] * inv_freq[None, :]
    cos, sin = jnp.cos(ang), jnp.sin(ang)
    pad = jnp.zeros((seq, V7X_LANES - QK_ROPE), F32)
    return (jnp.concatenate([cos, cos, pad], axis=-1),
            jnp.concatenate([-sin, sin, pad], axis=-1))


def _rope_weight_blocks(w_rope):
    half = QK_ROPE // 2
    t1, t2 = w_rope[:, :half], w_rope[:, half:]
    pad = jnp.zeros((w_rope.shape[0], V7X_LANES - QK_ROPE), w_rope.dtype)
    return jnp.concatenate([t1, t2, pad, t2, t1, pad], axis=-1)


def _routing(info, *, tm):
    n_tok = info.shape[0]
    n_assign = n_tok * TOP_K
    flat_e = info[:, :TOP_K].astype(jnp.int32).reshape(-1)
    experts = jnp.arange(N_EXPERTS, dtype=jnp.int32)
    counts = jnp.sum((flat_e[:, None] == experts[None, :]).astype(jnp.int32), axis=0)
    pad = (-counts) % tm
    filler = jnp.arange(N_EXPERTS * tm, dtype=jnp.int32)
    filler_key = jnp.where(filler % tm < pad[filler // tm], filler // tm, N_EXPERTS)
    keys = jnp.concatenate([flat_e, filler_key])
    ids = jnp.arange(keys.shape[0], dtype=jnp.int32)
    sorted_keys, order = lax.sort((keys, ids), num_keys=1, is_stable=True)
    is_real = order < n_assign
    tok_idx = jnp.where(is_real, order // TOP_K, 0)
    dst_idx = jnp.where(is_real, (order % TOP_K) * n_tok + order // TOP_K, order)
    tile_key = sorted_keys[::tm]
    tile_expert = jnp.minimum(tile_key, N_EXPERTS - 1)
    n_used = jnp.sum((tile_key < N_EXPERTS).astype(jnp.int32)).reshape(1)
    return tok_idx.reshape(-1, tm), dst_idx.reshape(-1, tm), tile_expert, n_used


def kernel(x, s5_lam_re, s5_lam_im, s5_log_step, s5_b_re, s5_b_im, s5_c_re, s5_c_im, s5_d, s5_w_glu,
           mla_q_w_a, mla_q_norm, mla_q_w_b, mla_o_w, kv_w_a, kv_norm, kv_w_b, ffn_w_in, ffn_w_out,
           moe_router, moe_w_in, moe_w_out, ln_g, ln_b):
    n_batch, seq, d_model = x.shape
    n_tok = n_batch * seq
    row = lambda v: v.reshape(1, -1).astype(F32)

    bb, cc, a_re, a_im = _s5_params(s5_lam_re[0], s5_lam_im[0], s5_log_step[0], s5_b_re[0], s5_b_im[0],
                                    s5_c_re[0], s5_c_im[0])
    xt = x.transpose(1, 0, 2).reshape(n_tok, d_model)
    x1 = _s5_layer(xt, bb, cc, a_re, a_im, row(s5_d[0]), s5_w_glu[0].astype(BF16),
                   row(ln_g[0, 0]), row(ln_b[0, 0]), n_batch=n_batch, n_time=min(32, seq))
    x2 = _dense_ffn(x1, ffn_w_in[0].astype(BF16), ffn_w_out[0].astype(BF16),
                    row(ln_g[0, 1]), row(ln_b[0, 1]), tm=512, fc=ffn_w_out.shape[1] // 2)
    x2 = x2.reshape(seq, n_batch * d_model)

    kv_lora = kv_norm.shape[0]
    wkva = jnp.concatenate([kv_w_a[:, :kv_lora], _rope_weight_blocks(kv_w_a[:, kv_lora:])],
                           axis=-1).astype(BF16)
    wq = mla_q_w_b[0].reshape(-1, MLA_HEADS, QK_NOPE + QK_ROPE)
    wqb = jnp.concatenate(
        [jnp.concatenate([wq[:, h, :QK_NOPE], _rope_weight_blocks(wq[:, h, QK_NOPE:])], axis=-1)
         for h in range(MLA_HEADS)], axis=-1).astype(BF16)
    cosx, sinx = _rope_lane_tables(seq)
    ts = min(512, seq)
    q, k, v = _qkv_proj(x2, wkva, row(kv_norm), kv_w_b.astype(BF16), mla_q_w_a[0].astype(BF16),
                        row(mla_q_norm[0]), wqb, cosx, sinx, ts=ts, n_batch=n_batch)
    o = _attention(q, k, v, tq=min(512, seq))
    wr = jnp.pad(moe_router[0], ((0, 0), (0, V7X_LANES - N_EXPERTS)))
    wr_hi = wr.astype(BF16)
    wr_lo = (wr - wr_hi.astype(F32)).astype(BF16)
    x3, info = _oproj_router(o, x2, mla_o_w[0].astype(BF16), row(ln_g[1, 0]), row(ln_b[1, 0]),
                             wr_hi, wr_lo, ts=ts)

    tok_idx, dst_idx, tile_expert, n_used = _routing(info, tm=512)
    ys = _moe_experts(tile_expert, n_used, tok_idx, dst_idx,
                      x3.reshape(n_tok, V7X_SUBLANES, V7X_LANES), moe_w_in[0].astype(BF16),
                      moe_w_out[0].astype(BF16), tm=512, fc=moe_w_out.shape[2] // 2)
    out = _combine(x3, info, ys.reshape(-1, V7X_LANES), row(ln_g[1, 1]), row(ln_b[1, 1]),
                   tc=min(512, n_tok))
    return out.reshape(n_batch, seq, d_model)
```

```python
import functools
import math

import jax
import jax.numpy as jnp
from jax import lax
from jax.experimental import pallas as pl
from jax.experimental.pallas import tpu as pltpu

F32 = jnp.float32
BF16 = jnp.bfloat16

V7X_LANES = 128
V7X_SUBLANES = 8
V7X_VMEM_LIMIT = 56 * 1024 * 1024

DEPTH = 2
ALPHA = (2.0 * DEPTH) ** 0.25
LN_EPS = 1e-5
RMS_EPS = 1e-6
ROPE_THETA = 10000.0

S5_GROUP = 16
S5_STATE = 64
S5_BLOCK_GROUPS = 16

MLA_HEADS = 8
QK_NOPE = 128
QK_ROPE = 64
V_DIM = 128
N_EXPERTS = 8
TOP_K = 2


def _const_spec(shape):
    zeros = (0,) * len(shape)
    return pl.BlockSpec(shape, lambda *_: zeros, pipeline_mode=pl.Buffered(1))


def _layer_norm(h, g, b):
    mu = jnp.mean(h, axis=-1, keepdims=True)
    c = h - mu
    var = jnp.mean(c * c, axis=-1, keepdims=True)
    return c * lax.rsqrt(var + LN_EPS) * g + b


def _gelu_tanh(y):
    return 0.5 * y * (1.0 + jnp.tanh(math.sqrt(2.0 / math.pi) * (y + 0.044715 * (y * y * y))))


def _s5_kernel(x_hbm, bb_ref, cc_ref, are_ref, aim_ref, d_ref, wglu_ref, g_ref, b_ref,
               o_ref, xin_ref, bu_ref, st_ref, sem, *, n_batch, n_time, n_blocks):
    half = bb_ref.shape[2] // 2
    cb = bb_ref.shape[1]
    slab = 512
    d_model = o_ref.shape[1]
    i = pl.program_id(0)
    slot = i % 2

    def chunk_copy(step, b, to_slot):
        return pltpu.make_async_copy(
            x_hbm.at[b, pl.ds(step * n_time, n_time), :], xin_ref.at[to_slot, :, b, :],
            sem.at[to_slot])

    @pl.when(i == 0)
    def _():
        st_ref[...] = jnp.zeros_like(st_ref)
        for b in range(n_batch):
            chunk_copy(0, b, 0).start()

    @pl.when(i + 1 < pl.num_programs(0))
    def _():
        for b in range(n_batch):
            chunk_copy(i + 1, b, 1 - slot).start()

    for b in range(n_batch):
        chunk_copy(i, b, slot).wait()

    x = xin_ref[slot].reshape(n_time * n_batch, d_model)
    xb = x.astype(BF16)
    for k in range(n_blocks):
        bu_ref[:, k * 2 * half:(k + 1) * 2 * half] = jnp.dot(
            xb[:, k * cb:(k + 1) * cb], bb_ref[k], preferred_element_type=F32)

    for k in range(n_blocks):
        for j in range(half // slab):
            re0 = k * 2 * half + j * slab
            im0 = re0 + half
            a0 = k * half + j * slab
            for bh in range(n_batch // V7X_SUBLANES):
                r0 = bh * V7X_SUBLANES
                ar = jnp.broadcast_to(are_ref[:, a0:a0 + slab], (V7X_SUBLANES, slab))
                ai = jnp.broadcast_to(aim_ref[:, a0:a0 + slab], (V7X_SUBLANES, slab))
                s_re = st_ref[r0:r0 + V7X_SUBLANES, re0:re0 + slab]
                s_im = st_ref[r0:r0 + V7X_SUBLANES, im0:im0 + slab]

                def step(t, carry, re0=re0, im0=im0, r0=r0, ar=ar, ai=ai):
                    s_re, s_im = carry
                    row = pl.multiple_of(t * n_batch + r0, V7X_SUBLANES)
                    b_re = bu_ref[pl.ds(row, V7X_SUBLANES), re0:re0 + slab]
                    b_im = bu_ref[pl.ds(row, V7X_SUBLANES), im0:im0 + slab]
                    n_re = ar * s_re - ai * s_im + b_re
                    n_im = ar * s_im + ai * s_re + b_im
                    bu_ref[pl.ds(row, V7X_SUBLANES), re0:re0 + slab] = n_re
                    bu_ref[pl.ds(row, V7X_SUBLANES), im0:im0 + slab] = n_im
                    return n_re, n_im

                s_re, s_im = lax.fori_loop(0, n_time, step, (s_re, s_im), unroll=4)
                st_ref[r0:r0 + V7X_SUBLANES, re0:re0 + slab] = s_re
                st_ref[r0:r0 + V7X_SUBLANES, im0:im0 + slab] = s_im

    ys = []
    for k in range(n_blocks):
        s_blk = bu_ref[:, k * 2 * half:(k + 1) * 2 * half].astype(BF16)
        ys.append(jnp.dot(s_blk, cc_ref[k], preferred_element_type=F32))
    y = jnp.concatenate(ys, axis=-1) + d_ref[...] * x
    act = _gelu_tanh(y).astype(BF16)
    z = jnp.dot(act, wglu_ref[...], preferred_element_type=F32)
    mix = z[:, :d_model] * jax.nn.sigmoid(z[:, d_model:])
    o_ref[...] = _layer_norm(ALPHA * x + mix, g_ref[...], b_ref[...])


def _s5_layer(x, bb, cc, a_re, a_im, d_skip, w_glu, ln_g, ln_b, *, n_time):
    n_batch, seq, d_model = x.shape
    n_rows = n_batch * seq
    n_blocks = bb.shape[0]
    rows = n_time * n_batch
    state_w = n_blocks * bb.shape[2]
    kern = functools.partial(_s5_kernel, n_batch=n_batch, n_time=n_time, n_blocks=n_blocks)
    return pl.pallas_call(
        kern,
        grid=(n_rows // rows,),
        in_specs=[
            pl.BlockSpec(memory_space=pl.ANY),
            _const_spec(bb.shape), _const_spec(cc.shape),
            _const_spec(a_re.shape), _const_spec(a_im.shape), _const_spec(d_skip.shape),
            _const_spec(w_glu.shape), _const_spec(ln_g.shape), _const_spec(ln_b.shape),
        ],
        out_specs=pl.BlockSpec((rows, d_model), lambda i: (i, 0)),
        out_shape=jax.ShapeDtypeStruct((n_rows, d_model), F32),
        scratch_shapes=[
            pltpu.VMEM((2, n_time, n_batch, d_model), F32),
            pltpu.VMEM((rows, state_w), F32), pltpu.VMEM((n_batch, state_w), F32),
            pltpu.SemaphoreType.DMA((2,)),
        ],
        compiler_params=pltpu.CompilerParams(
            dimension_semantics=("arbitrary",), vmem_limit_bytes=V7X_VMEM_LIMIT),
        name="s5_mixer",
    )(x, bb, cc, a_re, a_im, d_skip, w_glu, ln_g, ln_b)


def _ffn_kernel(x_ref, wg_ref, wu_ref, wo_ref, g_ref, b_ref, o_ref, *, fc):
    x = x_ref[...]
    xb = x.astype(BF16)
    acc = None
    for j in range(wo_ref.shape[0] // fc):
        gate = jnp.dot(xb, wg_ref[:, j * fc:(j + 1) * fc], preferred_element_type=F32)
        up = jnp.dot(xb, wu_ref[:, j * fc:(j + 1) * fc], preferred_element_type=F32)
        h = (gate * jax.nn.sigmoid(gate) * up).astype(BF16)
        part = jnp.dot(h, wo_ref[j * fc:(j + 1) * fc, :], preferred_element_type=F32)
        acc = part if acc is None else acc + part
    o_ref[...] = _layer_norm(ALPHA * x + acc, g_ref[...], b_ref[...])


def _dense_ffn(x, w_in, w_out, ln_g, ln_b, *, tm, fc):
    n_rows, d_model = x.shape
    ffn = w_out.shape[0]
    kern = functools.partial(_ffn_kernel, fc=fc)
    return pl.pallas_call(
        kern,
        grid=(n_rows // tm,),
        in_specs=[
            pl.BlockSpec((tm, d_model), lambda i: (i, 0)),
            pl.BlockSpec((d_model, ffn), lambda i: (0, 0), pipeline_mode=pl.Buffered(1)),
            pl.BlockSpec((d_model, ffn), lambda i: (0, 1), pipeline_mode=pl.Buffered(1)),
            _const_spec(w_out.shape), _const_spec(ln_g.shape), _const_spec(ln_b.shape),
        ],
        out_specs=pl.BlockSpec((tm, d_model), lambda i: (i, 0)),
        out_shape=jax.ShapeDtypeStruct((n_rows, d_model), F32),
        compiler_params=pltpu.CompilerParams(
            dimension_semantics=("parallel",), vmem_limit_bytes=V7X_VMEM_LIMIT),
        name="dense_ffn",
    )(x, w_in, w_in, w_out, ln_g, ln_b)


def _qkv_kernel(x_ref, wkva_ref, kvn_ref, wkvb_ref, wqa_ref, qn_ref, wqb_ref, cos_ref, sin_ref,
                q_ref, k_ref, v_ref, *, scale):
    xb = x_ref[...].astype(BF16)
    cosx = cos_ref[...]
    sinx = sin_ref[...]

    kva = jnp.dot(xb, wkva_ref[...], preferred_element_type=F32)
    ckv = kva[:, :V7X_LANES]
    ckv = ckv * lax.rsqrt(jnp.mean(ckv * ckv, axis=-1, keepdims=True) + RMS_EPS) * kvn_ref[...]
    k_rope = (kva[:, V7X_LANES:2 * V7X_LANES] * cosx
              + kva[:, 2 * V7X_LANES:3 * V7X_LANES] * sinx).astype(BF16)
    kv = jnp.dot(ckv.astype(BF16), wkvb_ref[...], preferred_element_type=F32)

    cq = jnp.dot(xb, wqa_ref[...], preferred_element_type=F32)
    cq = cq * lax.rsqrt(jnp.mean(cq * cq, axis=-1, keepdims=True) + RMS_EPS) * qn_ref[...]
    q = jnp.dot(cq.astype(BF16), wqb_ref[...], preferred_element_type=F32)

    for h in range(MLA_HEADS):
        kb = h * (QK_NOPE + V_DIM)
        k_ref[0, h, :, :QK_NOPE] = kv[:, kb:kb + QK_NOPE].astype(BF16)
        k_ref[0, h, :, QK_NOPE:] = k_rope
        v_ref[0, h] = kv[:, kb + QK_NOPE:kb + QK_NOPE + V_DIM].astype(BF16)
        qb = h * 3 * V7X_LANES
        q_ref[0, h, :, :QK_NOPE] = (q[:, qb:qb + QK_NOPE] * scale).astype(BF16)
        q_rope = (q[:, qb + V7X_LANES:qb + 2 * V7X_LANES] * cosx
                  + q[:, qb + 2 * V7X_LANES:qb + 3 * V7X_LANES] * sinx)
        q_ref[0, h, :, QK_NOPE:] = (q_rope * scale).astype(BF16)


def _qkv_proj(x_sbd, wkva, kvn, wkvb, wqa, qn, wqb, cosx, sinx, *, ts, n_batch):
    seq = x_sbd.shape[0]
    d_model = x_sbd.shape[1] // n_batch
    scale = 1.0 / math.sqrt(QK_NOPE + QK_ROPE)
    dk = QK_NOPE + V7X_LANES
    kern = functools.partial(_qkv_kernel, scale=scale)
    return pl.pallas_call(
        kern,
        grid=(n_batch, seq // ts),
        in_specs=[
            pl.BlockSpec((ts, d_model), lambda b, s: (s, b)),
            _const_spec(wkva.shape), _const_spec(kvn.shape), _const_spec(wkvb.shape),
            _const_spec(wqa.shape), _const_spec(qn.shape), _const_spec(wqb.shape),
            pl.BlockSpec((ts, V7X_LANES), lambda b, s: (s, 0)),
            pl.BlockSpec((ts, V7X_LANES), lambda b, s: (s, 0)),
        ],
        out_specs=[
            pl.BlockSpec((1, MLA_HEADS, ts, dk), lambda b, s: (b, 0, s, 0)),
            pl.BlockSpec((1, MLA_HEADS, ts, dk), lambda b, s: (b, 0, s, 0)),
            pl.BlockSpec((1, MLA_HEADS, ts, V_DIM), lambda b, s: (b, 0, s, 0)),
        ],
        out_shape=[
            jax.ShapeDtypeStruct((n_batch, MLA_HEADS, seq, dk), BF16),
            jax.ShapeDtypeStruct((n_batch, MLA_HEADS, seq, dk), BF16),
            jax.ShapeDtypeStruct((n_batch, MLA_HEADS, seq, V_DIM), BF16),
        ],
        compiler_params=pltpu.CompilerParams(
            dimension_semantics=("parallel", "parallel"), vmem_limit_bytes=V7X_VMEM_LIMIT),
        name="qkv_proj",
    )(x_sbd, wkva, kvn, wkvb, wqa, qn, wqb, cosx, sinx)


def _attn_kernel(q_ref, k_ref, v_ref, o_ref, *, tq):
    seq = q_ref.shape[2]
    n_tiles = seq // tq
    row = lax.broadcasted_iota(jnp.int32, (tq, tq), 0)
    col = lax.broadcasted_iota(jnp.int32, (tq, tq), 1)
    causal = col <= row
    m = [None] * n_tiles
    l = [None] * n_tiles
    acc = [None] * n_tiles
    for d in range(n_tiles):
        for qi in range(d, n_tiles):
            kj = qi - d
            q = q_ref[0, 0, qi * tq:(qi + 1) * tq, :]
            k = k_ref[0, 0, kj * tq:(kj + 1) * tq, :]
            v = v_ref[0, 0, kj * tq:(kj + 1) * tq, :]
            s = lax.dot_general(q, k, (((1,), (1,)), ((), ())), preferred_element_type=F32)
            if d == 0:
                s = jnp.where(causal, s, -jnp.inf)
            m_new = jnp.max(s, axis=-1, keepdims=True)
            if d > 0:
                m_new = jnp.maximum(m[qi], m_new)
            p = jnp.exp(s - m_new)
            pv = jnp.dot(p.astype(BF16), v, preferred_element_type=F32)
            if d == 0:
                l[qi] = jnp.sum(p, axis=-1, keepdims=True)
                acc[qi] = pv
            else:
                corr = jnp.exp(m[qi] - m_new)
                l[qi] = corr * l[qi] + jnp.sum(p, axis=-1, keepdims=True)
                acc[qi] = corr * acc[qi] + pv
            m[qi] = m_new
    for qi in range(n_tiles):
        o_ref[0, qi * tq:(qi + 1) * tq, :] = (acc[qi] / l[qi]).astype(o_ref.dtype)


def _attention(q, k, v, *, tq):
    n_batch, n_heads, seq, dk = q.shape
    dv = v.shape[3]
    kern = functools.partial(_attn_kernel, tq=tq)
    return pl.pallas_call(
        kern,
        grid=(n_batch, n_heads),
        in_specs=[
            pl.BlockSpec((1, 1, seq, dk), lambda b, h: (b, h, 0, 0)),
            pl.BlockSpec((1, 1, seq, dk), lambda b, h: (b, h, 0, 0)),
            pl.BlockSpec((1, 1, seq, dv), lambda b, h: (b, h, 0, 0)),
        ],
        out_specs=pl.BlockSpec((1, seq, dv), lambda b, h: (b, 0, h)),
        out_shape=jax.ShapeDtypeStruct((n_batch, seq, n_heads * dv), BF16),
        compiler_params=pltpu.CompilerParams(
            dimension_semantics=("parallel", "parallel"), vmem_limit_bytes=V7X_VMEM_LIMIT),
        name="mla_attention",
    )(q, k, v)


def _oproj_router_kernel(o_ref, x_ref, wo_ref, g_ref, b_ref, wrh_ref, wrl_ref, y_ref, info_ref):
    half = info_ref.shape[0] // 2
    for r in range(2):
        rows = pl.ds(r * half, half)
        y, info = _oproj_router_rows(o_ref[0, rows, :], x_ref[rows, :], wo_ref, g_ref, b_ref,
                                     wrh_ref, wrl_ref)
        for c in range(V7X_SUBLANES):
            y_ref[pl.ds(r * half * V7X_SUBLANES + c, half, stride=V7X_SUBLANES), :] = (
                y[:, c * V7X_LANES:(c + 1) * V7X_LANES])
        info_ref[rows, :] = info


def _oproj_router_rows(o, x, wo_ref, g_ref, b_ref, wrh_ref, wrl_ref):
    mix = jnp.dot(o, wo_ref[...], preferred_element_type=F32)
    y = _layer_norm(ALPHA * x + mix, g_ref[...], b_ref[...])

    y_hi = y.astype(BF16)
    y_lo = (y - y_hi.astype(F32)).astype(BF16)
    logits = (jnp.dot(y_hi, wrh_ref[...], preferred_element_type=F32)
              + jnp.dot(y_lo, wrh_ref[...], preferred_element_type=F32)
              + jnp.dot(y_hi, wrl_ref[...], preferred_element_type=F32))
    lane = lax.broadcasted_iota(jnp.int32, logits.shape, 1)
    logits = jnp.where(lane < N_EXPERTS, logits, -jnp.inf)
    m1 = jnp.max(logits, axis=-1, keepdims=True)
    i1 = jnp.min(jnp.where(logits == m1, lane, V7X_LANES), axis=-1, keepdims=True)
    rest = jnp.where(lane == i1, -jnp.inf, logits)
    m2 = jnp.max(rest, axis=-1, keepdims=True)
    i2 = jnp.min(jnp.where(rest == m2, lane, V7X_LANES), axis=-1, keepdims=True)
    e2 = jnp.exp(m2 - m1)
    den = 1.0 + e2
    info = jnp.where(lane == 0, i1.astype(F32),
                     jnp.where(lane == 1, i2.astype(F32),
                               jnp.where(lane == 2, 1.0 / den,
                                         jnp.where(lane == 3, e2 / den, 0.0))))
    return y, info


def _oproj_router(o, x_sbd, wo, ln_g, ln_b, wr_hi, wr_lo, *, ts):
    n_batch, seq, d_attn = o.shape
    d_model = x_sbd.shape[1] // n_batch
    assert d_model == V7X_SUBLANES * V7X_LANES
    nst = seq // ts
    return pl.pallas_call(
        _oproj_router_kernel,
        grid=(n_batch, nst),
        in_specs=[
            pl.BlockSpec((1, ts, d_attn), lambda b, s: (b, s, 0)),
            pl.BlockSpec((ts, d_model), lambda b, s: (s, b)),
            _const_spec(wo.shape), _const_spec(ln_g.shape), _const_spec(ln_b.shape),
            _const_spec(wr_hi.shape), _const_spec(wr_lo.shape),
        ],
        out_specs=[
            pl.BlockSpec((ts * V7X_SUBLANES, V7X_LANES), lambda b, s: (b * nst + s, 0)),
            pl.BlockSpec((ts, V7X_LANES), lambda b, s: (b * nst + s, 0)),
        ],
        out_shape=[
            jax.ShapeDtypeStruct((n_batch * seq * V7X_SUBLANES, V7X_LANES), F32),
            jax.ShapeDtypeStruct((n_batch * seq, V7X_LANES), F32),
        ],
        compiler_params=pltpu.CompilerParams(
            dimension_semantics=("parallel", "parallel"), vmem_limit_bytes=V7X_VMEM_LIMIT),
        name="oproj_router",
    )(o, x_sbd, wo, ln_g, ln_b, wr_hi, wr_lo)


def _moe_kernel(te_ref, nu_ref, tok_ref, dst_ref, x_hbm, wg_ref, wu_ref, wo_ref, y_hbm,
                xs_ref, ob_ref, xb_ref, acc_ref, gsem, ssem, *, tm, nf, n_rows):
    del te_ref
    s = pl.program_id(0)
    f = pl.program_id(1)
    gslot = s % 2
    cslot = 1 - gslot
    valid = (s >= 1) & (s <= nu_ref[0])
    sub, lanes = V7X_SUBLANES, V7X_LANES

    def wait_tile_rows(one_row_copy):
        unroll = 8

        def body(c, carry):
            for _ in range(unroll):
                one_row_copy.wait()
            return carry
        lax.fori_loop(0, tm // unroll, body, 0)

    gather_row = pltpu.make_async_copy(x_hbm.at[0], xs_ref.at[pl.ds(0, sub)], gsem)
    scatter_row = pltpu.make_async_copy(ob_ref.at[pl.ds(0, sub)], y_hbm.at[0], ssem)

    @pl.when((s == 0) & (f == 0))
    def _():
        ob_ref[...] = jnp.zeros_like(ob_ref)

    @pl.when((s >= 1) & (f == 0))
    def _():
        wait_tile_rows(gather_row)
        for c in range(sub):
            xb_ref[:, c * lanes:(c + 1) * lanes] = (
                xs_ref[pl.ds(cslot * tm * sub + c, tm, stride=sub), :].astype(BF16))

    for parity in range(2):
        @pl.when((f == 0) & (gslot == parity))
        def _(parity=parity):
            for row in range(tm):
                tok = tok_ref[0, 0, row]
                dst = jnp.where(s >= 2, dst_ref[0, 0, row], n_rows + row)
                vmem_rows = pl.ds((parity * tm + row) * sub, sub)
                pltpu.make_async_copy(
                    x_hbm.at[tok], xs_ref.at[vmem_rows], gsem).start(priority=row % 2)
                pltpu.make_async_copy(
                    ob_ref.at[vmem_rows], y_hbm.at[dst], ssem).start(priority=row % 2)

    @pl.when(valid)
    def _():
        xb = xb_ref[...]
        gate = jnp.dot(xb, wg_ref[0], preferred_element_type=F32)
        up = jnp.dot(xb, wu_ref[0], preferred_element_type=F32)
        h = (gate * jax.nn.sigmoid(gate) * up).astype(BF16)
        part = jnp.dot(h, wo_ref[0], preferred_element_type=F32)

        def store_result(res):
            for c in range(sub):
                ob_ref[pl.ds(cslot * tm * sub + c, tm, stride=sub), :] = (
                    res[:, c * lanes:(c + 1) * lanes])

        if nf == 1:
            store_result(part)
        else:
            @pl.when(f == 0)
            def _():
                acc_ref[...] = part

            @pl.when((f > 0) & (f < nf - 1))
            def _():
                acc_ref[...] += part

            @pl.when(f == nf - 1)
            def _():
                store_result(acc_ref[...] + part)

    @pl.when(f == nf - 1)
    def _():
        wait_tile_rows(scatter_row)

    @pl.when((s == pl.num_programs(0) - 1) & (f == nf - 1))
    def _():
        wait_tile_rows(gather_row)


def _moe_experts(tile_expert, n_used, tok_idx, dst_idx, x_tiles, w_in, w_out, *, tm, fc):
    d_model = x_tiles.shape[1] * x_tiles.shape[2]
    edim = w_out.shape[1]
    nf = edim // fc
    n_tiles = tok_idx.shape[0]
    n_rows = n_tiles * tm
    n_steps = n_tiles + 2

    def expert_chunk(s, f, te, nu):
        tile = jnp.clip(s - 1, 0, nu[0] - 1)
        chunk = jnp.where(s > nu[0], nf - 1, jnp.where(s < 1, 0, f))
        return te[tile], chunk

    def w_in_map(half):
        def index(s, f, te, nu):
            e, chunk = expert_chunk(s, f, te, nu)
            return e, 0, chunk + half * nf
        return index

    def w_out_map(s, f, te, nu):
        e, chunk = expert_chunk(s, f, te, nu)
        return e, chunk, 0

    grid_spec = pltpu.PrefetchScalarGridSpec(
        num_scalar_prefetch=2,
        grid=(n_steps, nf),
        in_specs=[
            pl.BlockSpec((1, 1, tm), lambda s, f, te, nu: (jnp.minimum(s, n_tiles - 1), 0, 0),
                         memory_space=pltpu.SMEM),
            pl.BlockSpec((1, 1, tm), lambda s, f, te, nu: (jnp.clip(s - 2, 0, n_tiles - 1), 0, 0),
                         memory_space=pltpu.SMEM),
            pl.BlockSpec(memory_space=pl.ANY),
            pl.BlockSpec((1, d_model, fc), w_in_map(0)),
            pl.BlockSpec((1, d_model, fc), w_in_map(1)),
            pl.BlockSpec((1, fc, d_model), w_out_map),
        ],
        out_specs=pl.BlockSpec(memory_space=pl.ANY),
        scratch_shapes=[
            pltpu.VMEM((2 * tm * V7X_SUBLANES, V7X_LANES), F32),
            pltpu.VMEM((2 * tm * V7X_SUBLANES, V7X_LANES), F32),
            pltpu.VMEM((tm, d_model), BF16), pltpu.VMEM((tm, d_model), F32),
            pltpu.SemaphoreType.DMA(()), pltpu.SemaphoreType.DMA(()),
        ],
    )
    kern = functools.partial(_moe_kernel, tm=tm, nf=nf, n_rows=n_rows)
    return pl.pallas_call(
        kern,
        grid_spec=grid_spec,
        out_shape=jax.ShapeDtypeStruct((n_rows + tm, V7X_SUBLANES, V7X_LANES), F32),
        compiler_params=pltpu.CompilerParams(
            dimension_semantics=("arbitrary", "arbitrary"), vmem_limit_bytes=V7X_VMEM_LIMIT),
        name="moe_experts",
    )(tile_expert, n_used, tok_idx.reshape(n_tiles, 1, tm), dst_idx.reshape(n_tiles, 1, tm),
      x_tiles, w_in, w_in, w_out)


def _combine_kernel(x_ref, info_ref, y0_ref, y1_ref, g_ref, b_ref, o_ref):
    tc, d_model = o_ref.shape
    sub, lanes = V7X_SUBLANES, V7X_LANES
    info = info_ref[...]
    g0, g1 = info[:, 2:3], info[:, 3:4]
    pieces = []
    for c in range(sub):
        rows = pl.ds(c, tc, stride=sub)
        pieces.append(ALPHA * x_ref[rows, :] + (y0_ref[rows, :] * g0 + y1_ref[rows, :] * g1))
    mu = jnp.sum(sum(pieces), axis=-1, keepdims=True) / d_model
    cen = [p - mu for p in pieces]
    var = jnp.sum(sum(c * c for c in cen), axis=-1, keepdims=True) / d_model
    rstd = lax.rsqrt(var + LN_EPS)
    for c in range(sub):
        cols = slice(c * lanes, (c + 1) * lanes)
        o_ref[:, cols] = cen[c] * rstd * g_ref[:, cols] + b_ref[:, cols]


def _combine(x_tiles, info, y_tiles, ln_g, ln_b, *, tc):
    n_tok = info.shape[0]
    d_model = V7X_SUBLANES * V7X_LANES
    nt = n_tok // tc
    tile_block = (tc * V7X_SUBLANES, V7X_LANES)
    return pl.pallas_call(
        _combine_kernel,
        grid=(nt,),
        in_specs=[
            pl.BlockSpec(tile_block, lambda i: (i, 0)),
            pl.BlockSpec((tc, V7X_LANES), lambda i: (i, 0)),
            pl.BlockSpec(tile_block, lambda i: (i, 0)),
            pl.BlockSpec(tile_block, lambda i: (i + nt, 0)),
            _const_spec(ln_g.shape), _const_spec(ln_b.shape),
        ],
        out_specs=pl.BlockSpec((tc, d_model), lambda i: (i, 0)),
        out_shape=jax.ShapeDtypeStruct((n_tok, d_model), F32),
        compiler_params=pltpu.CompilerParams(
            dimension_semantics=("parallel",), vmem_limit_bytes=V7X_VMEM_LIMIT),
        name="moe_combine",
    )(x_tiles, info, y_tiles, y_tiles, ln_g, ln_b)


def _s5_params(lam_re, lam_im, log_step, b_re, b_im, c_re, c_im):
    n_groups, n_state = lam_re.shape
    dt = jnp.exp(log_step)[:, None]
    mag = jnp.exp(lam_re * dt)
    lb_re = mag * jnp.cos(lam_im * dt)
    lb_im = mag * jnp.sin(lam_im * dt)
    den = lam_re * lam_re + lam_im * lam_im
    f_re = ((lb_re - 1.0) * lam_re + lb_im * lam_im) / den
    f_im = (lb_im * lam_re - (lb_re - 1.0) * lam_im) / den
    bb_re = f_re[..., None] * b_re - f_im[..., None] * b_im
    bb_im = f_re[..., None] * b_im + f_im[..., None] * b_re
    gpb = S5_BLOCK_GROUPS
    n_blocks = n_groups // gpb
    eye = jnp.eye(gpb, dtype=F32)

    def in_blocks(w):
        w = w.transpose(0, 2, 1).reshape(n_blocks, gpb, S5_GROUP, n_state)
        return jnp.einsum('kgcp,gh->kgchp', w, eye).reshape(n_blocks, gpb * S5_GROUP, gpb * n_state)

    def out_blocks(w):
        w = w.reshape(n_blocks, gpb, S5_GROUP, n_state)
        return jnp.einsum('kgcp,gh->kgphc', w, eye).reshape(n_blocks, gpb * n_state, gpb * S5_GROUP)

    bb = jnp.concatenate([in_blocks(bb_re), in_blocks(bb_im)], axis=-1).astype(BF16)
    cc = jnp.concatenate([out_blocks(c_re), out_blocks(-c_im)], axis=1).astype(BF16)
    return bb, cc, lb_re.reshape(1, -1), lb_im.reshape(1, -1)


def _rope_lane_tables(seq):
    pos = jnp.arange(seq, dtype=F32)
    inv_freq = ROPE_THETA ** (-jnp.arange(0, QK_ROPE, 2, dtype=F32) / QK_ROPE)
    ang = pos[:, None] * inv_freq[None, :]
    cos, sin = jnp.cos(ang), jnp.sin(ang)
    pad = jnp.zeros((seq, V7X_LANES - QK_ROPE), F32)
    return (jnp.concatenate([cos, cos, pad], axis=-1),
            jnp.concatenate([-sin, sin, pad], axis=-1))


def _rope_weight_blocks(w_rope):
    half = QK_ROPE // 2
    t1, t2 = w_rope[:, :half], w_rope[:, half:]
    pad = jnp.zeros((w_rope.shape[0], V7X_LANES - QK_ROPE), w_rope.dtype)
    return jnp.concatenate([t1, t2, pad, t2, t1, pad], axis=-1)


def _routing(info, *, tm):
    n_tok = info.shape[0]
    n_assign = n_tok * TOP_K
    flat_e = info[:, :TOP_K].astype(jnp.int32).reshape(-1)
    experts = jnp.arange(N_EXPERTS, dtype=jnp.int32)
    counts = jnp.sum((flat_e[:, None] == experts[None, :]).astype(jnp.int32), axis=0)
    pad = (-counts) % tm
    filler = jnp.arange(N_EXPERTS * tm, dtype=jnp.int32)
    filler_key = jnp.where(filler % tm < pad[filler // tm], filler // tm, N_EXPERTS)
    keys = jnp.concatenate([flat_e, filler_key])
    ids = jnp.arange(keys.shape[0], dtype=jnp.int32)
    sorted_keys, order = lax.sort((keys, ids), num_keys=1, is_stable=True)
    is_real = order < n_assign
    tok_idx = jnp.where(is_real, order // TOP_K, 0)
    dst_idx = jnp.where(is_real, (order % TOP_K) * n_tok + order // TOP_K, order)
    tile_key = sorted_keys[::tm]
    tile_expert = jnp.minimum(tile_key, N_EXPERTS - 1)
    n_used = jnp.sum((tile_key < N_EXPERTS).astype(jnp.int32)).reshape(1)
    return tok_idx.reshape(-1, tm), dst_idx.reshape(-1, tm), tile_expert, n_used


def kernel(x, s5_lam_re, s5_lam_im, s5_log_step, s5_b_re, s5_b_im, s5_c_re, s5_c_im, s5_d, s5_w_glu,
           mla_q_w_a, mla_q_norm, mla_q_w_b, mla_o_w, kv_w_a, kv_norm, kv_w_b, ffn_w_in, ffn_w_out,
           moe_router, moe_w_in, moe_w_out, ln_g, ln_b):
    n_batch, seq, d_model = x.shape
    n_tok = n_batch * seq
    row = lambda v: v.reshape(1, -1).astype(F32)

    bb, cc, a_re, a_im = _s5_params(s5_lam_re[0], s5_lam_im[0], s5_log_step[0], s5_b_re[0], s5_b_im[0],
                                    s5_c_re[0], s5_c_im[0])
    x1 = _s5_layer(x, bb, cc, a_re, a_im, row(s5_d[0]), s5_w_glu[0].astype(BF16),
                   row(ln_g[0, 0]), row(ln_b[0, 0]), n_time=min(32, seq))
    x2 = _dense_ffn(x1, ffn_w_in[0].astype(BF16), ffn_w_out[0].astype(BF16),
                    row(ln_g[0, 1]), row(ln_b[0, 1]), tm=512, fc=ffn_w_out.shape[1] // 2)
    x2 = x2.reshape(seq, n_batch * d_model)

    kv_lora = kv_norm.shape[0]
    wkva = jnp.concatenate([kv_w_a[:, :kv_lora], _rope_weight_blocks(kv_w_a[:, kv_lora:])],
                           axis=-1).astype(BF16)
    wq = mla_q_w_b[0].reshape(-1, MLA_HEADS, QK_NOPE + QK_ROPE)
    wqb = jnp.concatenate(
        [jnp.concatenate([wq[:, h, :QK_NOPE], _rope_weight_blocks(wq[:, h, QK_NOPE:])], axis=-1)
         for h in range(MLA_HEADS)], axis=-1).astype(BF16)
    cosx, sinx = _rope_lane_tables(seq)
    ts = min(512, seq)
    q, k, v = _qkv_proj(x2, wkva, row(kv_norm), kv_w_b.astype(BF16), mla_q_w_a[0].astype(BF16),
                        row(mla_q_norm[0]), wqb, cosx, sinx, ts=ts, n_batch=n_batch)
    o = _attention(q, k, v, tq=min(512, seq))
    wr = jnp.pad(moe_router[0], ((0, 0), (0, V7X_LANES - N_EXPERTS)))
    wr_hi = wr.astype(BF16)
    wr_lo = (wr - wr_hi.astype(F32)).astype(BF16)
    x3, info = _oproj_router(o, x2, mla_o_w[0].astype(BF16), row(ln_g[1, 0]), row(ln_b[1, 0]),
                             wr_hi, wr_lo, ts=ts)

    tok_idx, dst_idx, tile_expert, n_used = _routing(info, tm=512)
    ys = _moe_experts(tile_expert, n_used, tok_idx, dst_idx,
                      x3.reshape(n_tok, V7X_SUBLANES, V7X_LANES), moe_w_in[0].astype(BF16),
                      moe_w_out[0].astype(BF16), tm=512, fc=moe_w_out.shape[2] // 2)
    out = _combine(x3, info, ys.reshape(-1, V7X_LANES), row(ln_g[1, 1]), row(ln_b[1, 1]),
                   tc=min(512, n_tok))
    return out.reshape(n_batch, seq, d_model)
```

```python
import functools
import math

import jax
import jax.numpy as jnp
from jax import lax
from jax.experimental import pallas as pl
from jax.experimental.pallas import tpu as pltpu

F32 = jnp.float32
BF16 = jnp.bfloat16

V7X_LANES = 128
V7X_SUBLANES = 8
V7X_VMEM_LIMIT = 56 * 1024 * 1024

DEPTH = 2
ALPHA = (2.0 * DEPTH) ** 0.25
LN_EPS = 1e-5
RMS_EPS = 1e-6
ROPE_THETA = 10000.0

S5_GROUP = 16
S5_STATE = 64
S5_BLOCK_GROUPS = 16

MLA_HEADS = 8
QK_NOPE = 128
QK_ROPE = 64
V_DIM = 128
N_EXPERTS = 8
TOP_K = 2


def _const_spec(shape):
    zeros = (0,) * len(shape)
    return pl.BlockSpec(shape, lambda *_: zeros, pipeline_mode=pl.Buffered(1))


def _layer_norm(h, g, b):
    mu = jnp.mean(h, axis=-1, keepdims=True)
    c = h - mu
    var = jnp.mean(c * c, axis=-1, keepdims=True)
    return c * lax.rsqrt(var + LN_EPS) * g + b


def _gelu_tanh(y):
    return 0.5 * y * (1.0 + jnp.tanh(math.sqrt(2.0 / math.pi) * (y + 0.044715 * (y * y * y))))


def _s5_kernel(x_hbm, bb_ref, cc_ref, are_ref, aim_ref, d_ref, wglu_ref, g_ref, b_ref,
               o_ref, xin_ref, bu_ref, st_ref, sem, *, n_batch, n_time, n_blocks):
    half = bb_ref.shape[2] // 2
    cb = bb_ref.shape[1]
    slab = 512
    d_model = o_ref.shape[1]
    i = pl.program_id(0)
    slot = i % 2

    def chunk_copy(step, b, to_slot):
        return pltpu.make_async_copy(
            x_hbm.at[b, pl.ds(step * n_time, n_time), :], xin_ref.at[to_slot, :, b, :],
            sem.at[to_slot])

    @pl.when(i == 0)
    def _():
        st_ref[...] = jnp.zeros_like(st_ref)
        for b in range(n_batch):
            chunk_copy(0, b, 0).start()

    @pl.when(i + 1 < pl.num_programs(0))
    def _():
        for b in range(n_batch):
            chunk_copy(i + 1, b, 1 - slot).start()

    for b in range(n_batch):
        chunk_copy(i, b, slot).wait()

    x = xin_ref[slot].reshape(n_time * n_batch, d_model)
    xb = x.astype(BF16)
    for k in range(n_blocks):
        bu_ref[:, k * 2 * half:(k + 1) * 2 * half] = jnp.dot(
            xb[:, k * cb:(k + 1) * cb], bb_ref[k], preferred_element_type=F32)

    for k in range(n_blocks):
        for j in range(half // slab):
            re0 = k * 2 * half + j * slab
            im0 = re0 + half
            a0 = k * half + j * slab
            for bh in range(n_batch // V7X_SUBLANES):
                r0 = bh * V7X_SUBLANES
                ar = jnp.broadcast_to(are_ref[:, a0:a0 + slab], (V7X_SUBLANES, slab))
                ai = jnp.broadcast_to(aim_ref[:, a0:a0 + slab], (V7X_SUBLANES, slab))
                s_re = st_ref[r0:r0 + V7X_SUBLANES, re0:re0 + slab]
                s_im = st_ref[r0:r0 + V7X_SUBLANES, im0:im0 + slab]

                def step(t, carry, re0=re0, im0=im0, r0=r0, ar=ar, ai=ai):
                    s_re, s_im = carry
                    row = pl.multiple_of(t * n_batch + r0, V7X_SUBLANES)
                    b_re = bu_ref[pl.ds(row, V7X_SUBLANES), re0:re0 + slab]
                    b_im = bu_ref[pl.ds(row, V7X_SUBLANES), im0:im0 + slab]
                    n_re = ar * s_re - ai * s_im + b_re
                    n_im = ar * s_im + ai * s_re + b_im
                    bu_ref[pl.ds(row, V7X_SUBLANES), re0:re0 + slab] = n_re
                    bu_ref[pl.ds(row, V7X_SUBLANES), im0:im0 + slab] = n_im
                    return n_re, n_im

                s_re, s_im = lax.fori_loop(0, n_time, step, (s_re, s_im), unroll=4)
                st_ref[r0:r0 + V7X_SUBLANES, re0:re0 + slab] = s_re
                st_ref[r0:r0 + V7X_SUBLANES, im0:im0 + slab] = s_im

    ys = []
    for k in range(n_blocks):
        s_blk = bu_ref[:, k * 2 * half:(k + 1) * 2 * half].astype(BF16)
        ys.append(jnp.dot(s_blk, cc_ref[k], preferred_element_type=F32))
    y = jnp.concatenate(ys, axis=-1) + d_ref[...] * x
    act = _gelu_tanh(y).astype(BF16)
    z = jnp.dot(act, wglu_ref[...], preferred_element_type=F32)
    mix = z[:, :d_model] * jax.nn.sigmoid(z[:, d_model:])
    o_ref[...] = _layer_norm(ALPHA * x + mix, g_ref[...], b_ref[...])


def _s5_layer(x, bb, cc, a_re, a_im, d_skip, w_glu, ln_g, ln_b, *, n_time):
    n_batch, seq, d_model = x.shape
    n_rows = n_batch * seq
    n_blocks = bb.shape[0]
    rows = n_time * n_batch
    state_w = n_blocks * bb.shape[2]
    kern = functools.partial(_s5_kernel, n_batch=n_batch, n_time=n_time, n_blocks=n_blocks)
    return pl.pallas_call(
        kern,
        grid=(n_rows // rows,),
        in_specs=[
            pl.BlockSpec(memory_space=pl.ANY),
            _const_spec(bb.shape), _const_spec(cc.shape),
            _const_spec(a_re.shape), _const_spec(a_im.shape), _const_spec(d_skip.shape),
            _const_spec(w_glu.shape), _const_spec(ln_g.shape), _const_spec(ln_b.shape),
        ],
        out_specs=pl.BlockSpec((rows, d_model), lambda i: (i, 0)),
        out_shape=jax.ShapeDtypeStruct((n_rows, d_model), F32),
        scratch_shapes=[
            pltpu.VMEM((2, n_time, n_batch, d_model), F32),
            pltpu.VMEM((rows, state_w), F32), pltpu.VMEM((n_batch, state_w), F32),
            pltpu.SemaphoreType.DMA((2,)),
        ],
        compiler_params=pltpu.CompilerParams(
            dimension_semantics=("arbitrary",), vmem_limit_bytes=V7X_VMEM_LIMIT),
        name="s5_mixer",
    )(x, bb, cc, a_re, a_im, d_skip, w_glu, ln_g, ln_b)


def _ffn_kernel(x_ref, wg_ref, wu_ref, wo_ref, g_ref, b_ref, o_hbm, obuf_ref, sem, *, fc, n_batch):
    i = pl.program_id(0)
    last = pl.num_programs(0) - 1
    slot = i % 2
    n_time = x_ref.shape[0] // n_batch

    def out_copy(step, b, from_slot):
        return pltpu.make_async_copy(
            obuf_ref.at[from_slot, :, b, :], o_hbm.at[b, pl.ds(step * n_time, n_time), :],
            sem.at[from_slot])

    @pl.when(i >= 2)
    def _():
        for b in range(n_batch):
            out_copy(i - 2, b, slot).wait()

    x = x_ref[...]
    xb = x.astype(BF16)
    acc = None
    for j in range(wo_ref.shape[0] // fc):
        gate = jnp.dot(xb, wg_ref[:, j * fc:(j + 1) * fc], preferred_element_type=F32)
        up = jnp.dot(xb, wu_ref[:, j * fc:(j + 1) * fc], preferred_element_type=F32)
        h = (gate * jax.nn.sigmoid(gate) * up).astype(BF16)
        part = jnp.dot(h, wo_ref[j * fc:(j + 1) * fc, :], preferred_element_type=F32)
        acc = part if acc is None else acc + part
    out = _layer_norm(ALPHA * x + acc, g_ref[...], b_ref[...])
    obuf_ref[slot] = out.reshape(n_time, n_batch, out.shape[1])
    for b in range(n_batch):
        out_copy(i, b, slot).start()

    @pl.when((i == last) & (i >= 1))
    def _():
        for b in range(n_batch):
            out_copy(i - 1, b, 1 - slot).wait()

    @pl.when(i == last)
    def _():
        for b in range(n_batch):
            out_copy(i, b, slot).wait()


def _dense_ffn(x, w_in, w_out, ln_g, ln_b, *, n_batch, n_time, fc):
    n_rows, d_model = x.shape
    ffn = w_out.shape[0]
    tm = n_time * n_batch
    kern = functools.partial(_ffn_kernel, fc=fc, n_batch=n_batch)
    return pl.pallas_call(
        kern,
        grid=(n_rows // tm,),
        in_specs=[
            pl.BlockSpec((tm, d_model), lambda i: (i, 0)),
            pl.BlockSpec((d_model, ffn), lambda i: (0, 0), pipeline_mode=pl.Buffered(1)),
            pl.BlockSpec((d_model, ffn), lambda i: (0, 1), pipeline_mode=pl.Buffered(1)),
            _const_spec(w_out.shape), _const_spec(ln_g.shape), _const_spec(ln_b.shape),
        ],
        out_specs=pl.BlockSpec(memory_space=pl.ANY),
        out_shape=jax.ShapeDtypeStruct((n_batch, n_rows // n_batch, d_model), F32),
        scratch_shapes=[pltpu.VMEM((2, n_time, n_batch, d_model), F32),
                        pltpu.SemaphoreType.DMA((2,))],
        compiler_params=pltpu.CompilerParams(
            dimension_semantics=("arbitrary",), vmem_limit_bytes=V7X_VMEM_LIMIT),
        name="dense_ffn",
    )(x, w_in, w_in, w_out, ln_g, ln_b)


def _qkv_kernel(x_ref, wkva_ref, kvn_ref, wkvb_ref, wqa_ref, qn_ref, wqb_ref, cos_ref, sin_ref,
                q_ref, k_ref, v_ref, *, scale):
    xb = x_ref[0].astype(BF16)
    cosx = cos_ref[...]
    sinx = sin_ref[...]

    kva = jnp.dot(xb, wkva_ref[...], preferred_element_type=F32)
    ckv = kva[:, :V7X_LANES]
    ckv = ckv * lax.rsqrt(jnp.mean(ckv * ckv, axis=-1, keepdims=True) + RMS_EPS) * kvn_ref[...]
    k_rope = (kva[:, V7X_LANES:2 * V7X_LANES] * cosx
              + kva[:, 2 * V7X_LANES:3 * V7X_LANES] * sinx).astype(BF16)
    kv = jnp.dot(ckv.astype(BF16), wkvb_ref[...], preferred_element_type=F32)

    cq = jnp.dot(xb, wqa_ref[...], preferred_element_type=F32)
    cq = cq * lax.rsqrt(jnp.mean(cq * cq, axis=-1, keepdims=True) + RMS_EPS) * qn_ref[...]
    q = jnp.dot(cq.astype(BF16), wqb_ref[...], preferred_element_type=F32)

    for h in range(MLA_HEADS):
        kb = h * (QK_NOPE + V_DIM)
        k_ref[0, h, :, :QK_NOPE] = kv[:, kb:kb + QK_NOPE].astype(BF16)
        k_ref[0, h, :, QK_NOPE:] = k_rope
        v_ref[0, h] = kv[:, kb + QK_NOPE:kb + QK_NOPE + V_DIM].astype(BF16)
        qb = h * 3 * V7X_LANES
        q_ref[0, h, :, :QK_NOPE] = (q[:, qb:qb + QK_NOPE] * scale).astype(BF16)
        q_rope = (q[:, qb + V7X_LANES:qb + 2 * V7X_LANES] * cosx
                  + q[:, qb + 2 * V7X_LANES:qb + 3 * V7X_LANES] * sinx)
        q_ref[0, h, :, QK_NOPE:] = (q_rope * scale).astype(BF16)


def _qkv_proj(x, wkva, kvn, wkvb, wqa, qn, wqb, cosx, sinx, *, ts):
    n_batch, seq, d_model = x.shape
    scale = 1.0 / math.sqrt(QK_NOPE + QK_ROPE)
    dk = QK_NOPE + V7X_LANES
    kern = functools.partial(_qkv_kernel, scale=scale)
    return pl.pallas_call(
        kern,
        grid=(n_batch, seq // ts),
        in_specs=[
            pl.BlockSpec((1, ts, d_model), lambda b, s: (b, s, 0)),
            _const_spec(wkva.shape), _const_spec(kvn.shape), _const_spec(wkvb.shape),
            _const_spec(wqa.shape), _const_spec(qn.shape), _const_spec(wqb.shape),
            pl.BlockSpec((ts, V7X_LANES), lambda b, s: (s, 0)),
            pl.BlockSpec((ts, V7X_LANES), lambda b, s: (s, 0)),
        ],
        out_specs=[
            pl.BlockSpec((1, MLA_HEADS, ts, dk), lambda b, s: (b, 0, s, 0)),
            pl.BlockSpec((1, MLA_HEADS, ts, dk), lambda b, s: (b, 0, s, 0)),
            pl.BlockSpec((1, MLA_HEADS, ts, V_DIM), lambda b, s: (b, 0, s, 0)),
        ],
        out_shape=[
            jax.ShapeDtypeStruct((n_batch, MLA_HEADS, seq, dk), BF16),
            jax.ShapeDtypeStruct((n_batch, MLA_HEADS, seq, dk), BF16),
            jax.ShapeDtypeStruct((n_batch, MLA_HEADS, seq, V_DIM), BF16),
        ],
        compiler_params=pltpu.CompilerParams(
            dimension_semantics=("parallel", "parallel"), vmem_limit_bytes=V7X_VMEM_LIMIT),
        name="qkv_proj",
    )(x, wkva, kvn, wkvb, wqa, qn, wqb, cosx, sinx)


def _attn_kernel(q_ref, k_ref, v_ref, o_ref, *, tq):
    seq = q_ref.shape[2]
    n_tiles = seq // tq
    row = lax.broadcasted_iota(jnp.int32, (tq, tq), 0)
    col = lax.broadcasted_iota(jnp.int32, (tq, tq), 1)
    causal = col <= row
    for qi in range(n_tiles):
        q = q_ref[0, 0, qi * tq:(qi + 1) * tq, :]
        m = l = acc = None
        for kj in range(qi + 1):
            k = k_ref[0, 0, kj * tq:(kj + 1) * tq, :]
            v = v_ref[0, 0, kj * tq:(kj + 1) * tq, :]
            s = lax.dot_general(q, k, (((1,), (1,)), ((), ())), preferred_element_type=F32)
            if kj == qi:
                s = jnp.where(causal, s, -jnp.inf)
            m_new = jnp.max(s, axis=-1, keepdims=True)
            if kj > 0:
                m_new = jnp.maximum(m, m_new)
            p = jnp.exp(s - m_new)
            pv = jnp.dot(p.astype(BF16), v, preferred_element_type=F32)
            if kj == 0:
                l = jnp.sum(p, axis=-1, keepdims=True)
                acc = pv
            else:
                corr = jnp.exp(m - m_new)
                l = corr * l + jnp.sum(p, axis=-1, keepdims=True)
                acc = corr * acc + pv
            m = m_new
        o_ref[0, qi * tq:(qi + 1) * tq, :] = (acc / l).astype(o_ref.dtype)


def _attention(q, k, v, *, tq):
    n_batch, n_heads, seq, dk = q.shape
    dv = v.shape[3]
    kern = functools.partial(_attn_kernel, tq=tq)
    return pl.pallas_call(
        kern,
        grid=(n_batch, n_heads),
        in_specs=[
            pl.BlockSpec((1, 1, seq, dk), lambda b, h: (b, h, 0, 0)),
            pl.BlockSpec((1, 1, seq, dk), lambda b, h: (b, h, 0, 0)),
            pl.BlockSpec((1, 1, seq, dv), lambda b, h: (b, h, 0, 0)),
        ],
        out_specs=pl.BlockSpec((1, seq, dv), lambda b, h: (b, 0, h)),
        out_shape=jax.ShapeDtypeStruct((n_batch, seq, n_heads * dv), BF16),
        compiler_params=pltpu.CompilerParams(
            dimension_semantics=("parallel", "parallel"), vmem_limit_bytes=V7X_VMEM_LIMIT),
        name="mla_attention",
    )(q, k, v)


def _oproj_router_kernel(o_ref, x_ref, wo_ref, g_ref, b_ref, wrh_ref, wrl_ref, y_ref, info_ref):
    half = info_ref.shape[0] // 2
    for r in range(2):
        rows = pl.ds(r * half, half)
        y, info = _oproj_router_rows(o_ref[0, rows, :], x_ref[0, rows, :], wo_ref, g_ref, b_ref,
                                     wrh_ref, wrl_ref)
        for c in range(V7X_SUBLANES):
            y_ref[pl.ds(r * half * V7X_SUBLANES + c, half, stride=V7X_SUBLANES), :] = (
                y[:, c * V7X_LANES:(c + 1) * V7X_LANES])
        info_ref[rows, :] = info


def _oproj_router_rows(o, x, wo_ref, g_ref, b_ref, wrh_ref, wrl_ref):
    mix = jnp.dot(o, wo_ref[...], preferred_element_type=F32)
    y = _layer_norm(ALPHA * x + mix, g_ref[...], b_ref[...])

    y_hi = y.astype(BF16)
    y_lo = (y - y_hi.astype(F32)).astype(BF16)
    logits = (jnp.dot(y_hi, wrh_ref[...], preferred_element_type=F32)
              + jnp.dot(y_lo, wrh_ref[...], preferred_element_type=F32)
              + jnp.dot(y_hi, wrl_ref[...], preferred_element_type=F32))
    lane = lax.broadcasted_iota(jnp.int32, logits.shape, 1)
    logits = jnp.where(lane < N_EXPERTS, logits, -jnp.inf)
    m1 = jnp.max(logits, axis=-1, keepdims=True)
    i1 = jnp.min(jnp.where(logits == m1, lane, V7X_LANES), axis=-1, keepdims=True)
    rest = jnp.where(lane == i1, -jnp.inf, logits)
    m2 = jnp.max(rest, axis=-1, keepdims=True)
    i2 = jnp.min(jnp.where(rest == m2, lane, V7X_LANES), axis=-1, keepdims=True)
    e2 = jnp.exp(m2 - m1)
    den = 1.0 + e2
    info = jnp.where(lane == 0, i1.astype(F32),
                     jnp.where(lane == 1, i2.astype(F32),
                               jnp.where(lane == 2, 1.0 / den,
                                         jnp.where(lane == 3, e2 / den, 0.0))))
    return y, info


def _oproj_router(o, x, wo, ln_g, ln_b, wr_hi, wr_lo, *, ts):
    n_batch, seq, d_attn = o.shape
    d_model = x.shape[2]
    assert d_model == V7X_SUBLANES * V7X_LANES
    nst = seq // ts
    return pl.pallas_call(
        _oproj_router_kernel,
        grid=(n_batch, nst),
        in_specs=[
            pl.BlockSpec((1, ts, d_attn), lambda b, s: (b, s, 0)),
            pl.BlockSpec((1, ts, d_model), lambda b, s: (b, s, 0)),
            _const_spec(wo.shape), _const_spec(ln_g.shape), _const_spec(ln_b.shape),
            _const_spec(wr_hi.shape), _const_spec(wr_lo.shape),
        ],
        out_specs=[
            pl.BlockSpec((ts * V7X_SUBLANES, V7X_LANES), lambda b, s: (b * nst + s, 0)),
            pl.BlockSpec((ts, V7X_LANES), lambda b, s: (b * nst + s, 0)),
        ],
        out_shape=[
            jax.ShapeDtypeStruct((n_batch * seq * V7X_SUBLANES, V7X_LANES), F32),
            jax.ShapeDtypeStruct((n_batch * seq, V7X_LANES), F32),
        ],
        compiler_params=pltpu.CompilerParams(
            dimension_semantics=("parallel", "parallel"), vmem_limit_bytes=V7X_VMEM_LIMIT),
        name="oproj_router",
    )(o, x, wo, ln_g, ln_b, wr_hi, wr_lo)


def _moe_kernel(te_ref, nu_ref, tok_ref, dst_ref, x_hbm, wg_ref, wu_ref, wo_ref, y_hbm,
                xs_ref, ob_ref, xb_ref, acc_ref, gsem, ssem, *, tm, nf, n_rows):
    del te_ref
    s = pl.program_id(0)
    f = pl.program_id(1)
    gslot = s % 2
    cslot = 1 - gslot
    valid = (s >= 1) & (s <= nu_ref[0])
    sub, lanes = V7X_SUBLANES, V7X_LANES

    def wait_tile_rows(one_row_copy):
        unroll = 8

        def body(c, carry):
            for _ in range(unroll):
                one_row_copy.wait()
            return carry
        lax.fori_loop(0, tm // unroll, body, 0)

    gather_row = pltpu.make_async_copy(x_hbm.at[0], xs_ref.at[pl.ds(0, sub)], gsem)
    scatter_row = pltpu.make_async_copy(ob_ref.at[pl.ds(0, sub)], y_hbm.at[0], ssem)

    @pl.when((s == 0) & (f == 0))
    def _():
        ob_ref[...] = jnp.zeros_like(ob_ref)

    @pl.when((s >= 1) & (f == 0))
    def _():
        wait_tile_rows(gather_row)
        for c in range(sub):
            xb_ref[:, c * lanes:(c + 1) * lanes] = (
                xs_ref[pl.ds(cslot * tm * sub + c, tm, stride=sub), :].astype(BF16))

    for parity in range(2):
        @pl.when((f == 0) & (gslot == parity))
        def _(parity=parity):
            for row in range(tm):
                tok = tok_ref[0, 0, row]
                dst = jnp.where(s >= 2, dst_ref[0, 0, row], n_rows + row)
                vmem_rows = pl.ds((parity * tm + row) * sub, sub)
                pltpu.make_async_copy(
                    x_hbm.at[tok], xs_ref.at[vmem_rows], gsem).start(priority=row % 2)
                pltpu.make_async_copy(
                    ob_ref.at[vmem_rows], y_hbm.at[dst], ssem).start(priority=row % 2)

    @pl.when(valid)
    def _():
        xb = xb_ref[...]
        gate = jnp.dot(xb, wg_ref[0], preferred_element_type=F32)
        up = jnp.dot(xb, wu_ref[0], preferred_element_type=F32)
        h = (gate * jax.nn.sigmoid(gate) * up).astype(BF16)
        part = jnp.dot(h, wo_ref[0], preferred_element_type=F32)

        def store_result(res):
            for c in range(sub):
                ob_ref[pl.ds(cslot * tm * sub + c, tm, stride=sub), :] = (
                    res[:, c * lanes:(c + 1) * lanes])

        if nf == 1:
            store_result(part)
        else:
            @pl.when(f == 0)
            def _():
                acc_ref[...] = part

            @pl.when((f > 0) & (f < nf - 1))
            def _():
                acc_ref[...] += part

            @pl.when(f == nf - 1)
            def _():
                store_result(acc_ref[...] + part)

    @pl.when(f == nf - 1)
    def _():
        wait_tile_rows(scatter_row)

    @pl.when((s == pl.num_programs(0) - 1) & (f == nf - 1))
    def _():
        wait_tile_rows(gather_row)


def _moe_experts(tile_expert, n_used, tok_idx, dst_idx, x_tiles, w_in, w_out, *, tm, fc):
    d_model = x_tiles.shape[1] * x_tiles.shape[2]
    edim = w_out.shape[1]
    nf = edim // fc
    n_tiles = tok_idx.shape[0]
    n_rows = n_tiles * tm
    n_steps = n_tiles + 2

    def expert_chunk(s, f, te, nu):
        tile = jnp.clip(s - 1, 0, nu[0] - 1)
        chunk = jnp.where(s > nu[0], nf - 1, jnp.where(s < 1, 0, f))
        return te[tile], chunk

    def w_in_map(half):
        def index(s, f, te, nu):
            e, chunk = expert_chunk(s, f, te, nu)
            return e, 0, chunk + half * nf
        return index

    def w_out_map(s, f, te, nu):
        e, chunk = expert_chunk(s, f, te, nu)
        return e, chunk, 0

    grid_spec = pltpu.PrefetchScalarGridSpec(
        num_scalar_prefetch=2,
        grid=(n_steps, nf),
        in_specs=[
            pl.BlockSpec((1, 1, tm), lambda s, f, te, nu: (jnp.minimum(s, n_tiles - 1), 0, 0),
                         memory_space=pltpu.SMEM),
            pl.BlockSpec((1, 1, tm), lambda s, f, te, nu: (jnp.clip(s - 2, 0, n_tiles - 1), 0, 0),
                         memory_space=pltpu.SMEM),
            pl.BlockSpec(memory_space=pl.ANY),
            pl.BlockSpec((1, d_model, fc), w_in_map(0)),
            pl.BlockSpec((1, d_model, fc), w_in_map(1)),
            pl.BlockSpec((1, fc, d_model), w_out_map),
        ],
        out_specs=pl.BlockSpec(memory_space=pl.ANY),
        scratch_shapes=[
            pltpu.VMEM((2 * tm * V7X_SUBLANES, V7X_LANES), F32),
            pltpu.VMEM((2 * tm * V7X_SUBLANES, V7X_LANES), F32),
            pltpu.VMEM((tm, d_model), BF16), pltpu.VMEM((tm, d_model), F32),
            pltpu.SemaphoreType.DMA(()), pltpu.SemaphoreType.DMA(()),
        ],
    )
    kern = functools.partial(_moe_kernel, tm=tm, nf=nf, n_rows=n_rows)
    return pl.pallas_call(
        kern,
        grid_spec=grid_spec,
        out_shape=jax.ShapeDtypeStruct((n_rows + tm, V7X_SUBLANES, V7X_LANES), F32),
        compiler_params=pltpu.CompilerParams(
            dimension_semantics=("arbitrary", "arbitrary"), vmem_limit_bytes=V7X_VMEM_LIMIT),
        name="moe_experts",
    )(tile_expert, n_used, tok_idx.reshape(n_tiles, 1, tm), dst_idx.reshape(n_tiles, 1, tm),
      x_tiles, w_in, w_in, w_out)


def _combine_kernel(x_ref, info_ref, y0_ref, y1_ref, g_ref, b_ref, o_ref):
    tc, d_model = o_ref.shape
    sub, lanes = V7X_SUBLANES, V7X_LANES
    info = info_ref[...]
    g0, g1 = info[:, 2:3], info[:, 3:4]
    pieces = []
    for c in range(sub):
        rows = pl.ds(c, tc, stride=sub)
        pieces.append(ALPHA * x_ref[rows, :] + (y0_ref[rows, :] * g0 + y1_ref[rows, :] * g1))
    mu = jnp.sum(sum(pieces), axis=-1, keepdims=True) / d_model
    cen = [p - mu for p in pieces]
    var = jnp.sum(sum(c * c for c in cen), axis=-1, keepdims=True) / d_model
    rstd = lax.rsqrt(var + LN_EPS)
    for c in range(sub):
        cols = slice(c * lanes, (c + 1) * lanes)
        o_ref[:, cols] = cen[c] * rstd * g_ref[:, cols] + b_ref[:, cols]


def _combine(x_tiles, info, y_tiles, ln_g, ln_b, *, tc):
    n_tok = info.shape[0]
    d_model = V7X_SUBLANES * V7X_LANES
    nt = n_tok // tc
    tile_block = (tc * V7X_SUBLANES, V7X_LANES)
    return pl.pallas_call(
        _combine_kernel,
        grid=(nt,),
        in_specs=[
            pl.BlockSpec(tile_block, lambda i: (i, 0)),
            pl.BlockSpec((tc, V7X_LANES), lambda i: (i, 0)),
            pl.BlockSpec(tile_block, lambda i: (i, 0)),
            pl.BlockSpec(tile_block, lambda i: (i + nt, 0)),
            _const_spec(ln_g.shape), _const_spec(ln_b.shape),
        ],
        out_specs=pl.BlockSpec((tc, d_model), lambda i: (i, 0)),
        out_shape=jax.ShapeDtypeStruct((n_tok, d_model), F32),
        compiler_params=pltpu.CompilerParams(
            dimension_semantics=("parallel",), vmem_limit_bytes=V7X_VMEM_LIMIT),
        name="moe_combine",
    )(x_tiles, info, y_tiles, y_tiles, ln_g, ln_b)


def _s5_params(lam_re, lam_im, log_step, b_re, b_im, c_re, c_im):
    n_groups, n_state = lam_re.shape
    dt = jnp.exp(log_step)[:, None]
    mag = jnp.exp(lam_re * dt)
    lb_re = mag * jnp.cos(lam_im * dt)
    lb_im = mag * jnp.sin(lam_im * dt)
    den = lam_re * lam_re + lam_im * lam_im
    f_re = ((lb_re - 1.0) * lam_re + lb_im * lam_im) / den
    f_im = (lb_im * lam_re - (lb_re - 1.0) * lam_im) / den
    bb_re = f_re[..., None] * b_re - f_im[..., None] * b_im
    bb_im = f_re[..., None] * b_im + f_im[..., None] * b_re
    gpb = S5_BLOCK_GROUPS
    n_blocks = n_groups // gpb
    eye = jnp.eye(gpb, dtype=F32)

    def in_blocks(w):
        w = w.transpose(0, 2, 1).reshape(n_blocks, gpb, S5_GROUP, n_state)
        return jnp.einsum('kgcp,gh->kgchp', w, eye).reshape(n_blocks, gpb * S5_GROUP, gpb * n_state)

    def out_blocks(w):
        w = w.reshape(n_blocks, gpb, S5_GROUP, n_state)
        return jnp.einsum('kgcp,gh->kgphc', w, eye).reshape(n_blocks, gpb * n_state, gpb * S5_GROUP)

    bb = jnp.concatenate([in_blocks(bb_re), in_blocks(bb_im)], axis=-1).astype(BF16)
    cc = jnp.concatenate([out_blocks(c_re), out_blocks(-c_im)], axis=1).astype(BF16)
    return bb, cc, lb_re.reshape(1, -1), lb_im.reshape(1, -1)


def _rope_lane_tables(seq):
    pos = jnp.arange(seq, dtype=F32)
    inv_freq = ROPE_THETA ** (-jnp.arange(0, QK_ROPE, 2, dtype=F32) / QK_ROPE)
    ang = pos[:, None] * inv_freq[None, :]
    cos, sin = jnp.cos(ang), jnp.sin(ang)
    pad = jnp.zeros((seq, V7X_LANES - QK_ROPE), F32)
    return (jnp.concatenate([cos, cos, pad], axis=-1),
            jnp.concatenate([-sin, sin, pad], axis=-1))


def _rope_weight_blocks(w_rope):
    half = QK_ROPE // 2
    t1, t2 = w_rope[:, :half], w_rope[:, half:]
    pad = jnp.zeros((w_rope.shape[0], V7X_LANES - QK_ROPE), w_rope.dtype)
    return jnp.concatenate([t1, t2, pad, t2, t1, pad], axis=-1)


def _routing(info, *, tm):
    n_tok = info.shape[0]
    n_assign = n_tok * TOP_K
    flat_e = info[:, :TOP_K].astype(jnp.int32).reshape(-1)
    experts = jnp.arange(N_EXPERTS, dtype=jnp.int32)
    counts = jnp.sum((flat_e[:, None] == experts[None, :]).astype(jnp.int32), axis=0)
    pad = (-counts) % tm
    filler = jnp.arange(N_EXPERTS * tm, dtype=jnp.int32)
    filler_key = jnp.where(filler % tm < pad[filler // tm], filler // tm, N_EXPERTS)
    keys = jnp.concatenate([flat_e, filler_key])
    ids = jnp.arange(keys.shape[0], dtype=jnp.int32)
    sorted_keys, order = lax.sort((keys, ids), num_keys=1, is_stable=True)
    is_real = order < n_assign
    tok_idx = jnp.where(is_real, order // TOP_K, 0)
    dst_idx = jnp.where(is_real, (order % TOP_K) * n_tok + order // TOP_K, order)
    tile_key = sorted_keys[::tm]
    tile_expert = jnp.minimum(tile_key, N_EXPERTS - 1)
    n_used = jnp.sum((tile_key < N_EXPERTS).astype(jnp.int32)).reshape(1)
    return tok_idx.reshape(-1, tm), dst_idx.reshape(-1, tm), tile_expert, n_used


def kernel(x, s5_lam_re, s5_lam_im, s5_log_step, s5_b_re, s5_b_im, s5_c_re, s5_c_im, s5_d, s5_w_glu,
           mla_q_w_a, mla_q_norm, mla_q_w_b, mla_o_w, kv_w_a, kv_norm, kv_w_b, ffn_w_in, ffn_w_out,
           moe_router, moe_w_in, moe_w_out, ln_g, ln_b):
    n_batch, seq, d_model = x.shape
    n_tok = n_batch * seq
    row = lambda v: v.reshape(1, -1).astype(F32)

    bb, cc, a_re, a_im = _s5_params(s5_lam_re[0], s5_lam_im[0], s5_log_step[0], s5_b_re[0], s5_b_im[0],
                                    s5_c_re[0], s5_c_im[0])
    n_time = min(32, seq)
    x1 = _s5_layer(x, bb, cc, a_re, a_im, row(s5_d[0]), s5_w_glu[0].astype(BF16),
                   row(ln_g[0, 0]), row(ln_b[0, 0]), n_time=n_time)
    x2 = _dense_ffn(x1, ffn_w_in[0].astype(BF16), ffn_w_out[0].astype(BF16),
                    row(ln_g[0, 1]), row(ln_b[0, 1]), n_batch=n_batch, n_time=n_time,
                    fc=ffn_w_out.shape[1] // 2)

    kv_lora = kv_norm.shape[0]
    wkva = jnp.concatenate([kv_w_a[:, :kv_lora], _rope_weight_blocks(kv_w_a[:, kv_lora:])],
                           axis=-1).astype(BF16)
    wq = mla_q_w_b[0].reshape(-1, MLA_HEADS, QK_NOPE + QK_ROPE)
    wqb = jnp.concatenate(
        [jnp.concatenate([wq[:, h, :QK_NOPE], _rope_weight_blocks(wq[:, h, QK_NOPE:])], axis=-1)
         for h in range(MLA_HEADS)], axis=-1).astype(BF16)
    cosx, sinx = _rope_lane_tables(seq)
    ts = min(512, seq)
    q, k, v = _qkv_proj(x2, wkva, row(kv_norm), kv_w_b.astype(BF16), mla_q_w_a[0].astype(BF16),
                        row(mla_q_norm[0]), wqb, cosx, sinx, ts=ts)
    o = _attention(q, k, v, tq=min(512, seq))
    wr = jnp.pad(moe_router[0], ((0, 0), (0, V7X_LANES - N_EXPERTS)))
    wr_hi = wr.astype(BF16)
    wr_lo = (wr - wr_hi.astype(F32)).astype(BF16)
    x3, info = _oproj_router(o, x2, mla_o_w[0].astype(BF16), row(ln_g[1, 0]), row(ln_b[1, 0]),
                             wr_hi, wr_lo, ts=ts)

    tok_idx, dst_idx, tile_expert, n_used = _routing(info, tm=512)
    ys = _moe_experts(tile_expert, n_used, tok_idx, dst_idx,
                      x3.reshape(n_tok, V7X_SUBLANES, V7X_LANES), moe_w_in[0].astype(BF16),
                      moe_w_out[0].astype(BF16), tm=512, fc=moe_w_out.shape[2] // 2)
    out = _combine(x3, info, ys.reshape(-1, V7X_LANES), row(ln_g[1, 1]), row(ln_b[1, 1]),
                   tc=min(512, n_tok))
    return out.reshape(n_batch, seq, d_model)
```

```python
import functools
import math

import jax
import jax.numpy as jnp
from jax import lax
from jax.experimental import pallas as pl
from jax.experimental.pallas import tpu as pltpu

F32 = jnp.float32
BF16 = jnp.bfloat16

V7X_LANES = 128
V7X_SUBLANES = 8
V7X_VMEM_LIMIT = 56 * 1024 * 1024

DEPTH = 2
ALPHA = (2.0 * DEPTH) ** 0.25
LN_EPS = 1e-5
RMS_EPS = 1e-6
ROPE_THETA = 10000.0

S5_GROUP = 16
S5_STATE = 64
S5_BLOCK_GROUPS = 16

MLA_HEADS = 8
QK_NOPE = 128
QK_ROPE = 64
V_DIM = 128
N_EXPERTS = 8
TOP_K = 2


def _const_spec(shape):
    zeros = (0,) * len(shape)
    return pl.BlockSpec(shape, lambda *_: zeros, pipeline_mode=pl.Buffered(1))


def _layer_norm(h, g, b):
    mu = jnp.mean(h, axis=-1, keepdims=True)
    c = h - mu
    var = jnp.mean(c * c, axis=-1, keepdims=True)
    return c * lax.rsqrt(var + LN_EPS) * g + b


def _gelu_tanh(y):
    return 0.5 * y * (1.0 + jnp.tanh(math.sqrt(2.0 / math.pi) * (y + 0.044715 * (y * y * y))))


def _s5_kernel(x_hbm, bb_ref, cc_ref, are_ref, aim_ref, d_ref, wglu_ref, g_ref, b_ref,
               o_ref, xin_ref, bu_ref, st_ref, sem, *, n_batch, n_time, n_blocks):
    half = bb_ref.shape[2] // 2
    cb = bb_ref.shape[1]
    slab = 512
    d_model = o_ref.shape[1]
    i = pl.program_id(0)
    slot = i % 2

    def chunk_copy(step, b, to_slot):
        return pltpu.make_async_copy(
            x_hbm.at[b, pl.ds(step * n_time, n_time), :], xin_ref.at[to_slot, :, b, :],
            sem.at[to_slot])

    @pl.when(i == 0)
    def _():
        st_ref[...] = jnp.zeros_like(st_ref)
        for b in range(n_batch):
            chunk_copy(0, b, 0).start()

    @pl.when(i + 1 < pl.num_programs(0))
    def _():
        for b in range(n_batch):
            chunk_copy(i + 1, b, 1 - slot).start()

    for b in range(n_batch):
        chunk_copy(i, b, slot).wait()

    x = xin_ref[slot].reshape(n_time * n_batch, d_model)
    xb = x.astype(BF16)
    for k in range(n_blocks):
        bu_ref[:, k * 2 * half:(k + 1) * 2 * half] = jnp.dot(
            xb[:, k * cb:(k + 1) * cb], bb_ref[k], preferred_element_type=F32)

    for k in range(n_blocks):
        for j in range(half // slab):
            re0 = k * 2 * half + j * slab
            im0 = re0 + half
            a0 = k * half + j * slab
            for bh in range(n_batch // V7X_SUBLANES):
                r0 = bh * V7X_SUBLANES
                ar = jnp.broadcast_to(are_ref[:, a0:a0 + slab], (V7X_SUBLANES, slab))
                ai = jnp.broadcast_to(aim_ref[:, a0:a0 + slab], (V7X_SUBLANES, slab))
                s_re = st_ref[r0:r0 + V7X_SUBLANES, re0:re0 + slab]
                s_im = st_ref[r0:r0 + V7X_SUBLANES, im0:im0 + slab]

                def step(t, carry, re0=re0, im0=im0, r0=r0, ar=ar, ai=ai):
                    s_re, s_im = carry
                    row = pl.multiple_of(t * n_batch + r0, V7X_SUBLANES)
                    b_re = bu_ref[pl.ds(row, V7X_SUBLANES), re0:re0 + slab]
                    b_im = bu_ref[pl.ds(row, V7X_SUBLANES), im0:im0 + slab]
                    n_re = ar * s_re - ai * s_im + b_re
                    n_im = ar * s_im + ai * s_re + b_im
                    bu_ref[pl.ds(row, V7X_SUBLANES), re0:re0 + slab] = n_re
                    bu_ref[pl.ds(row, V7X_SUBLANES), im0:im0 + slab] = n_im
                    return n_re, n_im

                s_re, s_im = lax.fori_loop(0, n_time, step, (s_re, s_im), unroll=4)
                st_ref[r0:r0 + V7X_SUBLANES, re0:re0 + slab] = s_re
                st_ref[r0:r0 + V7X_SUBLANES, im0:im0 + slab] = s_im

    ys = []
    for k in range(n_blocks):
        s_blk = bu_ref[:, k * 2 * half:(k + 1) * 2 * half].astype(BF16)
        ys.append(jnp.dot(s_blk, cc_ref[k], preferred_element_type=F32))
    y = jnp.concatenate(ys, axis=-1) + d_ref[...] * x
    act = _gelu_tanh(y).astype(BF16)
    z = jnp.dot(act, wglu_ref[...], preferred_element_type=F32)
    mix = z[:, :d_model] * jax.nn.sigmoid(z[:, d_model:])
    o_ref[...] = _layer_norm(ALPHA * x + mix, g_ref[...], b_ref[...])


def _s5_layer(x, bb, cc, a_re, a_im, d_skip, w_glu, ln_g, ln_b, *, n_time):
    n_batch, seq, d_model = x.shape
    n_rows = n_batch * seq
    n_blocks = bb.shape[0]
    rows = n_time * n_batch
    state_w = n_blocks * bb.shape[2]
    kern = functools.partial(_s5_kernel, n_batch=n_batch, n_time=n_time, n_blocks=n_blocks)
    return pl.pallas_call(
        kern,
        grid=(n_rows // rows,),
        in_specs=[
            pl.BlockSpec(memory_space=pl.ANY),
            _const_spec(bb.shape), _const_spec(cc.shape),
            _const_spec(a_re.shape), _const_spec(a_im.shape), _const_spec(d_skip.shape),
            _const_spec(w_glu.shape), _const_spec(ln_g.shape), _const_spec(ln_b.shape),
        ],
        out_specs=pl.BlockSpec((rows, d_model), lambda i: (i, 0)),
        out_shape=jax.ShapeDtypeStruct((n_rows, d_model), F32),
        scratch_shapes=[
            pltpu.VMEM((2, n_time, n_batch, d_model), F32),
            pltpu.VMEM((rows, state_w), F32), pltpu.VMEM((n_batch, state_w), F32),
            pltpu.SemaphoreType.DMA((2,)),
        ],
        compiler_params=pltpu.CompilerParams(
            dimension_semantics=("arbitrary",), vmem_limit_bytes=V7X_VMEM_LIMIT),
        name="s5_mixer",
    )(x, bb, cc, a_re, a_im, d_skip, w_glu, ln_g, ln_b)


def _ffn_kernel(x_ref, wg_ref, wu_ref, wo_ref, g_ref, b_ref, o_hbm, obuf_ref, sem, *, fc, n_batch):
    i = pl.program_id(0)
    last = pl.num_programs(0) - 1
    slot = i % 2
    n_time = x_ref.shape[0] // n_batch

    def out_copy(step, b, from_slot):
        return pltpu.make_async_copy(
            obuf_ref.at[from_slot, :, b, :], o_hbm.at[b, pl.ds(step * n_time, n_time), :],
            sem.at[from_slot])

    @pl.when(i >= 2)
    def _():
        for b in range(n_batch):
            out_copy(i - 2, b, slot).wait()

    x = x_ref[...]
    xb = x.astype(BF16)
    acc = None
    for j in range(wo_ref.shape[0] // fc):
        gate = jnp.dot(xb, wg_ref[:, j * fc:(j + 1) * fc], preferred_element_type=F32)
        up = jnp.dot(xb, wu_ref[:, j * fc:(j + 1) * fc], preferred_element_type=F32)
        h = (gate * jax.nn.sigmoid(gate) * up).astype(BF16)
        part = jnp.dot(h, wo_ref[j * fc:(j + 1) * fc, :], preferred_element_type=F32)
        acc = part if acc is None else acc + part
    out = _layer_norm(ALPHA * x + acc, g_ref[...], b_ref[...])
    obuf_ref[slot] = out.reshape(n_time, n_batch, out.shape[1])
    for b in range(n_batch):
        out_copy(i, b, slot).start()

    @pl.when((i == last) & (i >= 1))
    def _():
        for b in range(n_batch):
            out_copy(i - 1, b, 1 - slot).wait()

    @pl.when(i == last)
    def _():
        for b in range(n_batch):
            out_copy(i, b, slot).wait()


def _dense_ffn(x, w_in, w_out, ln_g, ln_b, *, n_batch, n_time, fc):
    n_rows, d_model = x.shape
    ffn = w_out.shape[0]
    tm = n_time * n_batch
    kern = functools.partial(_ffn_kernel, fc=fc, n_batch=n_batch)
    return pl.pallas_call(
        kern,
        grid=(n_rows // tm,),
        in_specs=[
            pl.BlockSpec((tm, d_model), lambda i: (i, 0)),
            pl.BlockSpec((d_model, ffn), lambda i: (0, 0), pipeline_mode=pl.Buffered(1)),
            pl.BlockSpec((d_model, ffn), lambda i: (0, 1), pipeline_mode=pl.Buffered(1)),
            _const_spec(w_out.shape), _const_spec(ln_g.shape), _const_spec(ln_b.shape),
        ],
        out_specs=pl.BlockSpec(memory_space=pl.ANY),
        out_shape=jax.ShapeDtypeStruct((n_batch, n_rows // n_batch, d_model), F32),
        scratch_shapes=[pltpu.VMEM((2, n_time, n_batch, d_model), F32),
                        pltpu.SemaphoreType.DMA((2,))],
        compiler_params=pltpu.CompilerParams(
            dimension_semantics=("arbitrary",), vmem_limit_bytes=V7X_VMEM_LIMIT),
        name="dense_ffn",
    )(x, w_in, w_in, w_out, ln_g, ln_b)


def _qkv_kernel(x_ref, wkva_ref, kvn_ref, wkvb_ref, wqa_ref, qn_ref, wqb_ref, cos_ref, sin_ref,
                q_ref, k_ref, v_ref, *, scale):
    xb = x_ref[0].astype(BF16)
    cosx = cos_ref[...]
    sinx = sin_ref[...]

    kva = jnp.dot(xb, wkva_ref[...], preferred_element_type=F32)
    ckv = kva[:, :V7X_LANES]
    ckv = ckv * lax.rsqrt(jnp.mean(ckv * ckv, axis=-1, keepdims=True) + RMS_EPS) * kvn_ref[...]
    k_rope = (kva[:, V7X_LANES:2 * V7X_LANES] * cosx
              + kva[:, 2 * V7X_LANES:3 * V7X_LANES] * sinx).astype(BF16)
    kv = jnp.dot(ckv.astype(BF16), wkvb_ref[...], preferred_element_type=F32)

    cq = jnp.dot(xb, wqa_ref[...], preferred_element_type=F32)
    cq = cq * lax.rsqrt(jnp.mean(cq * cq, axis=-1, keepdims=True) + RMS_EPS) * qn_ref[...]
    q = jnp.dot(cq.astype(BF16), wqb_ref[...], preferred_element_type=F32)

    for h in range(MLA_HEADS):
        kb = h * (QK_NOPE + V_DIM)
        k_ref[0, h, :, :QK_NOPE] = kv[:, kb:kb + QK_NOPE].astype(BF16)
        k_ref[0, h, :, QK_NOPE:] = k_rope
        v_ref[0, h] = kv[:, kb + QK_NOPE:kb + QK_NOPE + V_DIM].astype(BF16)
        qb = h * 3 * V7X_LANES
        q_ref[0, h, :, :QK_NOPE] = (q[:, qb:qb + QK_NOPE] * scale).astype(BF16)
        q_rope = (q[:, qb + V7X_LANES:qb + 2 * V7X_LANES] * cosx
                  + q[:, qb + 2 * V7X_LANES:qb + 3 * V7X_LANES] * sinx)
        q_ref[0, h, :, QK_NOPE:] = (q_rope * scale).astype(BF16)


def _qkv_proj(x, wkva, kvn, wkvb, wqa, qn, wqb, cosx, sinx, *, ts):
    n_batch, seq, d_model = x.shape
    scale = 1.0 / math.sqrt(QK_NOPE + QK_ROPE)
    dk = QK_NOPE + V7X_LANES
    kern = functools.partial(_qkv_kernel, scale=scale)
    return pl.pallas_call(
        kern,
        grid=(n_batch, seq // ts),
        in_specs=[
            pl.BlockSpec((1, ts, d_model), lambda b, s: (b, s, 0)),
            _const_spec(wkva.shape), _const_spec(kvn.shape), _const_spec(wkvb.shape),
            _const_spec(wqa.shape), _const_spec(qn.shape), _const_spec(wqb.shape),
            pl.BlockSpec((ts, V7X_LANES), lambda b, s: (s, 0)),
            pl.BlockSpec((ts, V7X_LANES), lambda b, s: (s, 0)),
        ],
        out_specs=[
            pl.BlockSpec((1, MLA_HEADS, ts, dk), lambda b, s: (b, 0, s, 0)),
            pl.BlockSpec((1, MLA_HEADS, ts, dk), lambda b, s: (b, 0, s, 0)),
            pl.BlockSpec((1, MLA_HEADS, ts, V_DIM), lambda b, s: (b, 0, s, 0)),
        ],
        out_shape=[
            jax.ShapeDtypeStruct((n_batch, MLA_HEADS, seq, dk), BF16),
            jax.ShapeDtypeStruct((n_batch, MLA_HEADS, seq, dk), BF16),
            jax.ShapeDtypeStruct((n_batch, MLA_HEADS, seq, V_DIM), BF16),
        ],
        compiler_params=pltpu.CompilerParams(
            dimension_semantics=("parallel", "parallel"), vmem_limit_bytes=V7X_VMEM_LIMIT),
        name="qkv_proj",
    )(x, wkva, kvn, wkvb, wqa, qn, wqb, cosx, sinx)


def _attn_kernel(q_ref, k_ref, v_ref, w_ref, o_ref, wb_ref, *, tq):
    wb_ref[...] = w_ref[...].astype(BF16)
    seq = q_ref.shape[2]
    n_tiles = seq // tq
    row = lax.broadcasted_iota(jnp.int32, (tq, tq), 0)
    col = lax.broadcasted_iota(jnp.int32, (tq, tq), 1)
    causal = col <= row
    for qi in range(n_tiles):
        q = q_ref[0, 0, qi * tq:(qi + 1) * tq, :]
        m = l = acc = None
        for kj in range(qi + 1):
            k = k_ref[0, 0, kj * tq:(kj + 1) * tq, :]
            v = v_ref[0, 0, kj * tq:(kj + 1) * tq, :]
            s = lax.dot_general(q, k, (((1,), (1,)), ((), ())), preferred_element_type=F32)
            if kj == qi:
                s = jnp.where(causal, s, -jnp.inf)
            m_new = jnp.max(s, axis=-1, keepdims=True)
            if kj > 0:
                m_new = jnp.maximum(m, m_new)
            p = jnp.exp(s - m_new)
            pv = jnp.dot(p.astype(BF16), v, preferred_element_type=F32)
            if kj == 0:
                l = jnp.sum(p, axis=-1, keepdims=True)
                acc = pv
            else:
                corr = jnp.exp(m - m_new)
                l = corr * l + jnp.sum(p, axis=-1, keepdims=True)
                acc = corr * acc + pv
            m = m_new
        o_ref[0, qi * tq:(qi + 1) * tq, :] = (acc / l).astype(o_ref.dtype)


def _cast_slab_rows(w2d, n_steps):
    rows = w2d.shape[0] // n_steps
    assert rows * n_steps == w2d.shape[0] and rows % (2 * V7X_SUBLANES) == 0
    return rows


def _attention(q, k, v, w_cast, *, tq):
    n_batch, n_heads, seq, dk = q.shape
    dv = v.shape[3]
    w_rows = _cast_slab_rows(w_cast, n_batch * n_heads)
    w_spec = pl.BlockSpec((w_rows, w_cast.shape[1]), lambda b, h: (b * n_heads + h, 0))
    kern = functools.partial(_attn_kernel, tq=tq)
    return pl.pallas_call(
        kern,
        grid=(n_batch, n_heads),
        in_specs=[
            pl.BlockSpec((1, 1, seq, dk), lambda b, h: (b, h, 0, 0)),
            pl.BlockSpec((1, 1, seq, dk), lambda b, h: (b, h, 0, 0)),
            pl.BlockSpec((1, 1, seq, dv), lambda b, h: (b, h, 0, 0)),
            w_spec,
        ],
        out_specs=[pl.BlockSpec((1, seq, dv), lambda b, h: (b, 0, h)), w_spec],
        out_shape=[jax.ShapeDtypeStruct((n_batch, seq, n_heads * dv), BF16),
                   jax.ShapeDtypeStruct(w_cast.shape, BF16)],
        compiler_params=pltpu.CompilerParams(
            dimension_semantics=("parallel", "parallel"), vmem_limit_bytes=V7X_VMEM_LIMIT),
        name="mla_attention",
    )(q, k, v, w_cast)


def _oproj_router_kernel(o_ref, x_ref, wo_ref, g_ref, b_ref, wrh_ref, wrl_ref, w_ref,
                         y_ref, info_ref, wb_ref):
    wb_ref[...] = w_ref[...].astype(BF16)
    half = info_ref.shape[0] // 2
    for r in range(2):
        rows = pl.ds(r * half, half)
        y, info = _oproj_router_rows(o_ref[0, rows, :], x_ref[0, rows, :], wo_ref, g_ref, b_ref,
                                     wrh_ref, wrl_ref)
        for c in range(V7X_SUBLANES):
            y_ref[pl.ds(r * half * V7X_SUBLANES + c, half, stride=V7X_SUBLANES), :] = (
                y[:, c * V7X_LANES:(c + 1) * V7X_LANES])
        info_ref[rows, :] = info


def _oproj_router_rows(o, x, wo_ref, g_ref, b_ref, wrh_ref, wrl_ref):
    mix = jnp.dot(o, wo_ref[...], preferred_element_type=F32)
    y = _layer_norm(ALPHA * x + mix, g_ref[...], b_ref[...])

    y_hi = y.astype(BF16)
    y_lo = (y - y_hi.astype(F32)).astype(BF16)
    logits = (jnp.dot(y_hi, wrh_ref[...], preferred_element_type=F32)
              + jnp.dot(y_lo, wrh_ref[...], preferred_element_type=F32)
              + jnp.dot(y_hi, wrl_ref[...], preferred_element_type=F32))
    lane = lax.broadcasted_iota(jnp.int32, logits.shape, 1)
    logits = jnp.where(lane < N_EXPERTS, logits, -jnp.inf)
    m1 = jnp.max(logits, axis=-1, keepdims=True)
    i1 = jnp.min(jnp.where(logits == m1, lane, V7X_LANES), axis=-1, keepdims=True)
    rest = jnp.where(lane == i1, -jnp.inf, logits)
    m2 = jnp.max(rest, axis=-1, keepdims=True)
    i2 = jnp.min(jnp.where(rest == m2, lane, V7X_LANES), axis=-1, keepdims=True)
    e2 = jnp.exp(m2 - m1)
    den = 1.0 + e2
    info = jnp.where(lane == 0, i1.astype(F32),
                     jnp.where(lane == 1, i2.astype(F32),
                               jnp.where(lane == 2, 1.0 / den,
                                         jnp.where(lane == 3, e2 / den, 0.0))))
    return y, info


def _oproj_router(o, x, wo, ln_g, ln_b, wr_hi, wr_lo, w_cast, *, ts):
    n_batch, seq, d_attn = o.shape
    d_model = x.shape[2]
    assert d_model == V7X_SUBLANES * V7X_LANES
    nst = seq // ts
    w_rows = _cast_slab_rows(w_cast, n_batch * nst)
    w_spec = pl.BlockSpec((w_rows, w_cast.shape[1]), lambda b, s: (b * nst + s, 0))
    return pl.pallas_call(
        _oproj_router_kernel,
        grid=(n_batch, nst),
        in_specs=[
            pl.BlockSpec((1, ts, d_attn), lambda b, s: (b, s, 0)),
            pl.BlockSpec((1, ts, d_model), lambda b, s: (b, s, 0)),
            _const_spec(wo.shape), _const_spec(ln_g.shape), _const_spec(ln_b.shape),
            _const_spec(wr_hi.shape), _const_spec(wr_lo.shape),
            w_spec,
        ],
        out_specs=[
            pl.BlockSpec((ts * V7X_SUBLANES, V7X_LANES), lambda b, s: (b * nst + s, 0)),
            pl.BlockSpec((ts, V7X_LANES), lambda b, s: (b * nst + s, 0)),
            w_spec,
        ],
        out_shape=[
            jax.ShapeDtypeStruct((n_batch * seq * V7X_SUBLANES, V7X_LANES), F32),
            jax.ShapeDtypeStruct((n_batch * seq, V7X_LANES), F32),
            jax.ShapeDtypeStruct(w_cast.shape, BF16),
        ],
        compiler_params=pltpu.CompilerParams(
            dimension_semantics=("parallel", "parallel"), vmem_limit_bytes=V7X_VMEM_LIMIT),
        name="oproj_router",
    )(o, x, wo, ln_g, ln_b, wr_hi, wr_lo, w_cast)


def _moe_kernel(te_ref, nu_ref, tok_ref, dst_ref, x_hbm, wg_ref, wu_ref, wo_ref, y_hbm,
                xs_ref, ob_ref, xb_ref, acc_ref, gsem, ssem, *, tm, nf, n_rows):
    del te_ref
    s = pl.program_id(0)
    f = pl.program_id(1)
    gslot = s % 2
    cslot = 1 - gslot
    valid = (s >= 1) & (s <= nu_ref[0])
    sub, lanes = V7X_SUBLANES, V7X_LANES

    def wait_tile_rows(one_row_copy):
        unroll = 8

        def body(c, carry):
            for _ in range(unroll):
                one_row_copy.wait()
            return carry
        lax.fori_loop(0, tm // unroll, body, 0)

    gather_row = pltpu.make_async_copy(x_hbm.at[0], xs_ref.at[pl.ds(0, sub)], gsem)
    scatter_row = pltpu.make_async_copy(ob_ref.at[pl.ds(0, sub)], y_hbm.at[0], ssem)

    @pl.when((s == 0) & (f == 0))
    def _():
        ob_ref[...] = jnp.zeros_like(ob_ref)

    @pl.when((s >= 1) & (f == 0))
    def _():
        wait_tile_rows(gather_row)
        for c in range(sub):
            xb_ref[:, c * lanes:(c + 1) * lanes] = (
                xs_ref[pl.ds(cslot * tm * sub + c, tm, stride=sub), :].astype(BF16))

    for parity in range(2):
        @pl.when((f == 0) & (gslot == parity))
        def _(parity=parity):
            for row in range(tm):
                tok = tok_ref[0, 0, row]
                dst = jnp.where(s >= 2, dst_ref[0, 0, row], n_rows + row)
                vmem_rows = pl.ds((parity * tm + row) * sub, sub)
                pltpu.make_async_copy(
                    x_hbm.at[tok], xs_ref.at[vmem_rows], gsem).start(priority=row % 2)
                pltpu.make_async_copy(
                    ob_ref.at[vmem_rows], y_hbm.at[dst], ssem).start(priority=row % 2)

    @pl.when(valid)
    def _():
        xb = xb_ref[...]
        gate = jnp.dot(xb, wg_ref[0], preferred_element_type=F32)
        up = jnp.dot(xb, wu_ref[0], preferred_element_type=F32)
        h = (gate * jax.nn.sigmoid(gate) * up).astype(BF16)
        part = jnp.dot(h, wo_ref[0], preferred_element_type=F32)

        def store_result(res):
            for c in range(sub):
                ob_ref[pl.ds(cslot * tm * sub + c, tm, stride=sub), :] = (
                    res[:, c * lanes:(c + 1) * lanes])

        if nf == 1:
            store_result(part)
        else:
            @pl.when(f == 0)
            def _():
                acc_ref[...] = part

            @pl.when((f > 0) & (f < nf - 1))
            def _():
                acc_ref[...] += part

            @pl.when(f == nf - 1)
            def _():
                store_result(acc_ref[...] + part)

    @pl.when(f == nf - 1)
    def _():
        wait_tile_rows(scatter_row)

    @pl.when((s == pl.num_programs(0) - 1) & (f == nf - 1))
    def _():
        wait_tile_rows(gather_row)


def _moe_experts(tile_expert, n_used, tok_idx, dst_idx, x_tiles, w_in, w_out, *, tm, fc):
    d_model = x_tiles.shape[1] * x_tiles.shape[2]
    edim = w_out.shape[1]
    nf = edim // fc
    n_tiles = tok_idx.shape[0]
    n_rows = n_tiles * tm
    n_steps = n_tiles + 2

    def expert_chunk(s, f, te, nu):
        tile = jnp.clip(s - 1, 0, nu[0] - 1)
        chunk = jnp.where(s > nu[0], nf - 1, jnp.where(s < 1, 0, f))
        return te[tile], chunk

    def w_in_map(half):
        def index(s, f, te, nu):
            e, chunk = expert_chunk(s, f, te, nu)
            return e, 0, chunk + half * nf
        return index

    def w_out_map(s, f, te, nu):
        e, chunk = expert_chunk(s, f, te, nu)
        return e, chunk, 0

    grid_spec = pltpu.PrefetchScalarGridSpec(
        num_scalar_prefetch=2,
        grid=(n_steps, nf),
        in_specs=[
            pl.BlockSpec((1, 1, tm), lambda s, f, te, nu: (jnp.minimum(s, n_tiles - 1), 0, 0),
                         memory_space=pltpu.SMEM),
            pl.BlockSpec((1, 1, tm), lambda s, f, te, nu: (jnp.clip(s - 2, 0, n_tiles - 1), 0, 0),
                         memory_space=pltpu.SMEM),
            pl.BlockSpec(memory_space=pl.ANY),
            pl.BlockSpec((1, d_model, fc), w_in_map(0)),
            pl.BlockSpec((1, d_model, fc), w_in_map(1)),
            pl.BlockSpec((1, fc, d_model), w_out_map),
        ],
        out_specs=pl.BlockSpec(memory_space=pl.ANY),
        scratch_shapes=[
            pltpu.VMEM((2 * tm * V7X_SUBLANES, V7X_LANES), F32),
            pltpu.VMEM((2 * tm * V7X_SUBLANES, V7X_LANES), F32),
            pltpu.VMEM((tm, d_model), BF16), pltpu.VMEM((tm, d_model), F32),
            pltpu.SemaphoreType.DMA(()), pltpu.SemaphoreType.DMA(()),
        ],
    )
    kern = functools.partial(_moe_kernel, tm=tm, nf=nf, n_rows=n_rows)
    return pl.pallas_call(
        kern,
        grid_spec=grid_spec,
        out_shape=jax.ShapeDtypeStruct((n_rows + tm, V7X_SUBLANES, V7X_LANES), F32),
        compiler_params=pltpu.CompilerParams(
            dimension_semantics=("arbitrary", "arbitrary"), vmem_limit_bytes=V7X_VMEM_LIMIT),
        name="moe_experts",
    )(tile_expert, n_used, tok_idx.reshape(n_tiles, 1, tm), dst_idx.reshape(n_tiles, 1, tm),
      x_tiles, w_in, w_in, w_out)


def _combine_kernel(x_ref, info_ref, y0_ref, y1_ref, g_ref, b_ref, o_ref):
    tc, d_model = o_ref.shape
    sub, lanes = V7X_SUBLANES, V7X_LANES
    info = info_ref[...]
    g0, g1 = info[:, 2:3], info[:, 3:4]
    pieces = []
    for c in range(sub):
        rows = pl.ds(c, tc, stride=sub)
        pieces.append(ALPHA * x_ref[rows, :] + (y0_ref[rows, :] * g0 + y1_ref[rows, :] * g1))
    mu = jnp.sum(sum(pieces), axis=-1, keepdims=True) / d_model
    cen = [p - mu for p in pieces]
    var = jnp.sum(sum(c * c for c in cen), axis=-1, keepdims=True) / d_model
    rstd = lax.rsqrt(var + LN_EPS)
    for c in range(sub):
        cols = slice(c * lanes, (c + 1) * lanes)
        o_ref[:, cols] = cen[c] * rstd * g_ref[:, cols] + b_ref[:, cols]


def _combine(x_tiles, info, y_tiles, ln_g, ln_b, *, tc):
    n_tok = info.shape[0]
    d_model = V7X_SUBLANES * V7X_LANES
    nt = n_tok // tc
    tile_block = (tc * V7X_SUBLANES, V7X_LANES)
    return pl.pallas_call(
        _combine_kernel,
        grid=(nt,),
        in_specs=[
            pl.BlockSpec(tile_block, lambda i: (i, 0)),
            pl.BlockSpec((tc, V7X_LANES), lambda i: (i, 0)),
            pl.BlockSpec(tile_block, lambda i: (i, 0)),
            pl.BlockSpec(tile_block, lambda i: (i + nt, 0)),
            _const_spec(ln_g.shape), _const_spec(ln_b.shape),
        ],
        out_specs=pl.BlockSpec((tc, d_model), lambda i: (i, 0)),
        out_shape=jax.ShapeDtypeStruct((n_tok, d_model), F32),
        compiler_params=pltpu.CompilerParams(
            dimension_semantics=("parallel",), vmem_limit_bytes=V7X_VMEM_LIMIT),
        name="moe_combine",
    )(x_tiles, info, y_tiles, y_tiles, ln_g, ln_b)


def _s5_params(lam_re, lam_im, log_step, b_re, b_im, c_re, c_im):
    n_groups, n_state = lam_re.shape
    dt = jnp.exp(log_step)[:, None]
    mag = jnp.exp(lam_re * dt)
    lb_re = mag * jnp.cos(lam_im * dt)
    lb_im = mag * jnp.sin(lam_im * dt)
    den = lam_re * lam_re + lam_im * lam_im
    f_re = ((lb_re - 1.0) * lam_re + lb_im * lam_im) / den
    f_im = (lb_im * lam_re - (lb_re - 1.0) * lam_im) / den
    bb_re = f_re[..., None] * b_re - f_im[..., None] * b_im
    bb_im = f_re[..., None] * b_im + f_im[..., None] * b_re
    gpb = S5_BLOCK_GROUPS
    n_blocks = n_groups // gpb
    eye = jnp.eye(gpb, dtype=F32)

    def in_blocks(w):
        w = w.transpose(0, 2, 1).reshape(n_blocks, gpb, S5_GROUP, n_state)
        return jnp.einsum('kgcp,gh->kgchp', w, eye).reshape(n_blocks, gpb * S5_GROUP, gpb * n_state)

    def out_blocks(w):
        w = w.reshape(n_blocks, gpb, S5_GROUP, n_state)
        return jnp.einsum('kgcp,gh->kgphc', w, eye).reshape(n_blocks, gpb * n_state, gpb * S5_GROUP)

    bb = jnp.concatenate([in_blocks(bb_re), in_blocks(bb_im)], axis=-1).astype(BF16)
    cc = jnp.concatenate([out_blocks(c_re), out_blocks(-c_im)], axis=1).astype(BF16)
    return bb, cc, lb_re.reshape(1, -1), lb_im.reshape(1, -1)


def _rope_lane_tables(seq):
    pos = jnp.arange(seq, dtype=F32)
    inv_freq = ROPE_THETA ** (-jnp.arange(0, QK_ROPE, 2, dtype=F32) / QK_ROPE)
    ang = pos[:, None] * inv_freq[None, :]
    cos, sin = jnp.cos(ang), jnp.sin(ang)
    pad = jnp.zeros((seq, V7X_LANES - QK_ROPE), F32)
    return (jnp.concatenate([cos, cos, pad], axis=-1),
            jnp.concatenate([-sin, sin, pad], axis=-1))


def _rope_weight_blocks(w_rope):
    half = QK_ROPE // 2
    t1, t2 = w_rope[:, :half], w_rope[:, half:]
    pad = jnp.zeros((w_rope.shape[0], V7X_LANES - QK_ROPE), w_rope.dtype)
    return jnp.concatenate([t1, t2, pad, t2, t1, pad], axis=-1)


def _routing(info, *, tm):
    n_tok = info.shape[0]
    n_assign = n_tok * TOP_K
    flat_e = info[:, :TOP_K].astype(jnp.int32).reshape(-1)
    experts = jnp.arange(N_EXPERTS, dtype=jnp.int32)
    counts = jnp.sum((flat_e[:, None] == experts[None, :]).astype(jnp.int32), axis=0)
    pad = (-counts) % tm
    filler = jnp.arange(N_EXPERTS * tm, dtype=jnp.int32)
    filler_key = jnp.where(filler % tm < pad[filler // tm], filler // tm, N_EXPERTS)
    keys = jnp.concatenate([flat_e, filler_key])
    ids = jnp.arange(keys.shape[0], dtype=jnp.int32)
    sorted_keys, order = lax.sort((keys, ids), num_keys=1, is_stable=True)
    is_real = order < n_assign
    tok_idx = jnp.where(is_real, order // TOP_K, 0)
    dst_idx = jnp.where(is_real, (order % TOP_K) * n_tok + order // TOP_K, order)
    tile_key = sorted_keys[::tm]
    tile_expert = jnp.minimum(tile_key, N_EXPERTS - 1)
    n_used = jnp.sum((tile_key < N_EXPERTS).astype(jnp.int32)).reshape(1)
    return tok_idx.reshape(-1, tm), dst_idx.reshape(-1, tm), tile_expert, n_used


def kernel(x, s5_lam_re, s5_lam_im, s5_log_step, s5_b_re, s5_b_im, s5_c_re, s5_c_im, s5_d, s5_w_glu,
           mla_q_w_a, mla_q_norm, mla_q_w_b, mla_o_w, kv_w_a, kv_norm, kv_w_b, ffn_w_in, ffn_w_out,
           moe_router, moe_w_in, moe_w_out, ln_g, ln_b):
    n_batch, seq, d_model = x.shape
    n_tok = n_batch * seq
    row = lambda v: v.reshape(1, -1).astype(F32)

    bb, cc, a_re, a_im = _s5_params(s5_lam_re[0], s5_lam_im[0], s5_log_step[0], s5_b_re[0], s5_b_im[0],
                                    s5_c_re[0], s5_c_im[0])
    n_time = min(32, seq)
    x1 = _s5_layer(x, bb, cc, a_re, a_im, row(s5_d[0]), s5_w_glu[0].astype(BF16),
                   row(ln_g[0, 0]), row(ln_b[0, 0]), n_time=n_time)
    x2 = _dense_ffn(x1, ffn_w_in[0].astype(BF16), ffn_w_out[0].astype(BF16),
                    row(ln_g[0, 1]), row(ln_b[0, 1]), n_batch=n_batch, n_time=n_time,
                    fc=ffn_w_out.shape[1] // 2)

    kv_lora = kv_norm.shape[0]
    wkva = jnp.concatenate([kv_w_a[:, :kv_lora], _rope_weight_blocks(kv_w_a[:, kv_lora:])],
                           axis=-1).astype(BF16)
    wq = mla_q_w_b[0].reshape(-1, MLA_HEADS, QK_NOPE + QK_ROPE)
    wqb = jnp.concatenate(
        [jnp.concatenate([wq[:, h, :QK_NOPE], _rope_weight_blocks(wq[:, h, QK_NOPE:])], axis=-1)
         for h in range(MLA_HEADS)], axis=-1).astype(BF16)
    cosx, sinx = _rope_lane_tables(seq)
    ts = min(512, seq)
    q, k, v = _qkv_proj(x2, wkva, row(kv_norm), kv_w_b.astype(BF16), mla_q_w_a[0].astype(BF16),
                        row(mla_q_norm[0]), wqb, cosx, sinx, ts=ts)
    w_in, w_out = moe_w_in[0], moe_w_out[0]
    o, w_in_bf = _attention(q, k, v, w_in.reshape(-1, w_in.shape[2]), tq=min(512, seq))
    wr = jnp.pad(moe_router[0], ((0, 0), (0, V7X_LANES - N_EXPERTS)))
    wr_hi = wr.astype(BF16)
    wr_lo = (wr - wr_hi.astype(F32)).astype(BF16)
    x3, info, w_out_bf = _oproj_router(o, x2, mla_o_w[0].astype(BF16), row(ln_g[1, 0]),
                                       row(ln_b[1, 0]), wr_hi, wr_lo,
                                       w_out.reshape(-1, w_out.shape[2]), ts=ts)

    tok_idx, dst_idx, tile_expert, n_used = _routing(info, tm=512)
    ys = _moe_experts(tile_expert, n_used, tok_idx, dst_idx,
                      x3.reshape(n_tok, V7X_SUBLANES, V7X_LANES), w_in_bf.reshape(w_in.shape),
                      w_out_bf.reshape(w_out.shape), tm=512, fc=w_out.shape[1] // 2)
    out = _combine(x3, info, ys.reshape(-1, V7X_LANES), row(ln_g[1, 1]), row(ln_b[1, 1]),
                   tc=min(512, n_tok))
    return out.reshape(n_batch, seq, d_model)
```

```python
import functools
import math

import jax
import jax.numpy as jnp
from jax import lax
from jax.experimental import pallas as pl
from jax.experimental.pallas import tpu as pltpu

F32 = jnp.float32
BF16 = jnp.bfloat16

V7X_LANES = 128
V7X_SUBLANES = 8
V7X_VMEM_LIMIT = 56 * 1024 * 1024

DEPTH = 2
ALPHA = (2.0 * DEPTH) ** 0.25
LN_EPS = 1e-5
RMS_EPS = 1e-6
ROPE_THETA = 10000.0

S5_GROUP = 16
S5_STATE = 64
S5_BLOCK_GROUPS = 16

MLA_HEADS = 8
QK_NOPE = 128
QK_ROPE = 64
V_DIM = 128
N_EXPERTS = 8
TOP_K = 2


def _const_spec(shape):
    zeros = (0,) * len(shape)
    return pl.BlockSpec(shape, lambda *_: zeros, pipeline_mode=pl.Buffered(1))


def _layer_norm(h, g, b):
    mu = jnp.mean(h, axis=-1, keepdims=True)
    c = h - mu
    var = jnp.mean(c * c, axis=-1, keepdims=True)
    return c * lax.rsqrt(var + LN_EPS) * g + b


def _gelu_tanh(y):
    return 0.5 * y * (1.0 + jnp.tanh(math.sqrt(2.0 / math.pi) * (y + 0.044715 * (y * y * y))))


def _s5_kernel(x_hbm, bb_ref, cc_ref, are_ref, aim_ref, d_ref, wglu_ref, g_ref, b_ref,
               o_ref, xin_ref, bu_ref, st_ref, sem, *, n_batch, n_time, n_blocks):
    half = bb_ref.shape[2] // 2
    cb = bb_ref.shape[1]
    slab = 512
    d_model = o_ref.shape[1]
    i = pl.program_id(0)
    slot = i % 2

    def chunk_copy(step, b, to_slot):
        return pltpu.make_async_copy(
            x_hbm.at[b, pl.ds(step * n_time, n_time), :], xin_ref.at[to_slot, :, b, :],
            sem.at[to_slot])

    @pl.when(i == 0)
    def _():
        st_ref[...] = jnp.zeros_like(st_ref)
        for b in range(n_batch):
            chunk_copy(0, b, 0).start()

    @pl.when(i + 1 < pl.num_programs(0))
    def _():
        for b in range(n_batch):
            chunk_copy(i + 1, b, 1 - slot).start()

    for b in range(n_batch):
        chunk_copy(i, b, slot).wait()

    x = xin_ref[slot].reshape(n_time * n_batch, d_model)
    xb = x.astype(BF16)
    for k in range(n_blocks):
        bu_ref[:, k * 2 * half:(k + 1) * 2 * half] = jnp.dot(
            xb[:, k * cb:(k + 1) * cb], bb_ref[k], preferred_element_type=F32)

    for k in range(n_blocks):
        for j in range(half // slab):
            re0 = k * 2 * half + j * slab
            im0 = re0 + half
            a0 = k * half + j * slab
            for bh in range(n_batch // V7X_SUBLANES):
                r0 = bh * V7X_SUBLANES
                ar = jnp.broadcast_to(are_ref[:, a0:a0 + slab], (V7X_SUBLANES, slab))
                ai = jnp.broadcast_to(aim_ref[:, a0:a0 + slab], (V7X_SUBLANES, slab))
                s_re = st_ref[r0:r0 + V7X_SUBLANES, re0:re0 + slab]
                s_im = st_ref[r0:r0 + V7X_SUBLANES, im0:im0 + slab]

                def step(t, carry, re0=re0, im0=im0, r0=r0, ar=ar, ai=ai):
                    s_re, s_im = carry
                    row = pl.multiple_of(t * n_batch + r0, V7X_SUBLANES)
                    b_re = bu_ref[pl.ds(row, V7X_SUBLANES), re0:re0 + slab]
                    b_im = bu_ref[pl.ds(row, V7X_SUBLANES), im0:im0 + slab]
                    n_re = ar * s_re - ai * s_im + b_re
                    n_im = ar * s_im + ai * s_re + b_im
                    bu_ref[pl.ds(row, V7X_SUBLANES), re0:re0 + slab] = n_re
                    bu_ref[pl.ds(row, V7X_SUBLANES), im0:im0 + slab] = n_im
                    return n_re, n_im

                s_re, s_im = lax.fori_loop(0, n_time, step, (s_re, s_im), unroll=4)
                st_ref[r0:r0 + V7X_SUBLANES, re0:re0 + slab] = s_re
                st_ref[r0:r0 + V7X_SUBLANES, im0:im0 + slab] = s_im

    ys = []
    for k in range(n_blocks):
        s_blk = bu_ref[:, k * 2 * half:(k + 1) * 2 * half].astype(BF16)
        ys.append(jnp.dot(s_blk, cc_ref[k], preferred_element_type=F32))
    y = jnp.concatenate(ys, axis=-1) + d_ref[...] * x
    act = _gelu_tanh(y).astype(BF16)
    z = jnp.dot(act, wglu_ref[...], preferred_element_type=F32)
    mix = z[:, :d_model] * jax.nn.sigmoid(z[:, d_model:])
    o_ref[...] = _layer_norm(ALPHA * x + mix, g_ref[...], b_ref[...])


def _s5_layer(x, bb, cc, a_re, a_im, d_skip, w_glu, ln_g, ln_b, *, n_time):
    n_batch, seq, d_model = x.shape
    n_rows = n_batch * seq
    n_blocks = bb.shape[0]
    rows = n_time * n_batch
    state_w = n_blocks * bb.shape[2]
    kern = functools.partial(_s5_kernel, n_batch=n_batch, n_time=n_time, n_blocks=n_blocks)
    return pl.pallas_call(
        kern,
        grid=(n_rows // rows,),
        in_specs=[
            pl.BlockSpec(memory_space=pl.ANY),
            _const_spec(bb.shape), _const_spec(cc.shape),
            _const_spec(a_re.shape), _const_spec(a_im.shape), _const_spec(d_skip.shape),
            _const_spec(w_glu.shape), _const_spec(ln_g.shape), _const_spec(ln_b.shape),
        ],
        out_specs=pl.BlockSpec((rows, d_model), lambda i: (i, 0)),
        out_shape=jax.ShapeDtypeStruct((n_rows, d_model), F32),
        scratch_shapes=[
            pltpu.VMEM((2, n_time, n_batch, d_model), F32),
            pltpu.VMEM((rows, state_w), F32), pltpu.VMEM((n_batch, state_w), F32),
            pltpu.SemaphoreType.DMA((2,)),
        ],
        compiler_params=pltpu.CompilerParams(
            dimension_semantics=("arbitrary",), vmem_limit_bytes=V7X_VMEM_LIMIT),
        name="s5_mixer",
    )(x, bb, cc, a_re, a_im, d_skip, w_glu, ln_g, ln_b)


def _ffn_kernel(x_ref, wg_ref, wu_ref, wo_ref, g_ref, b_ref, o_hbm, obuf_ref, sem, *, fc, n_batch):
    i = pl.program_id(0)
    last = pl.num_programs(0) - 1
    slot = i % 2
    n_time = x_ref.shape[0] // n_batch

    def out_copy(step, b, from_slot):
        return pltpu.make_async_copy(
            obuf_ref.at[from_slot, :, b, :], o_hbm.at[b, pl.ds(step * n_time, n_time), :],
            sem.at[from_slot])

    @pl.when(i >= 2)
    def _():
        for b in range(n_batch):
            out_copy(i - 2, b, slot).wait()

    x = x_ref[...]
    xb = x.astype(BF16)
    acc = None
    for j in range(wo_ref.shape[0] // fc):
        gate = jnp.dot(xb, wg_ref[:, j * fc:(j + 1) * fc], preferred_element_type=F32)
        up = jnp.dot(xb, wu_ref[:, j * fc:(j + 1) * fc], preferred_element_type=F32)
        h = (gate * jax.nn.sigmoid(gate) * up).astype(BF16)
        part = jnp.dot(h, wo_ref[j * fc:(j + 1) * fc, :], preferred_element_type=F32)
        acc = part if acc is None else acc + part
    out = _layer_norm(ALPHA * x + acc, g_ref[...], b_ref[...])
    obuf_ref[slot] = out.reshape(n_time, n_batch, out.shape[1])
    for b in range(n_batch):
        out_copy(i, b, slot).start()

    @pl.when((i == last) & (i >= 1))
    def _():
        for b in range(n_batch):
            out_copy(i - 1, b, 1 - slot).wait()

    @pl.when(i == last)
    def _():
        for b in range(n_batch):
            out_copy(i, b, slot).wait()


def _dense_ffn(x, w_in, w_out, ln_g, ln_b, *, n_batch, n_time, fc):
    n_rows, d_model = x.shape
    ffn = w_out.shape[0]
    tm = n_time * n_batch
    kern = functools.partial(_ffn_kernel, fc=fc, n_batch=n_batch)
    return pl.pallas_call(
        kern,
        grid=(n_rows // tm,),
        in_specs=[
            pl.BlockSpec((tm, d_model), lambda i: (i, 0)),
            pl.BlockSpec((d_model, ffn), lambda i: (0, 0), pipeline_mode=pl.Buffered(1)),
            pl.BlockSpec((d_model, ffn), lambda i: (0, 1), pipeline_mode=pl.Buffered(1)),
            _const_spec(w_out.shape), _const_spec(ln_g.shape), _const_spec(ln_b.shape),
        ],
        out_specs=pl.BlockSpec(memory_space=pl.ANY),
        out_shape=jax.ShapeDtypeStruct((n_batch, n_rows // n_batch, d_model), F32),
        scratch_shapes=[pltpu.VMEM((2, n_time, n_batch, d_model), F32),
                        pltpu.SemaphoreType.DMA((2,))],
        compiler_params=pltpu.CompilerParams(
            dimension_semantics=("arbitrary",), vmem_limit_bytes=V7X_VMEM_LIMIT),
        name="dense_ffn",
    )(x, w_in, w_in, w_out, ln_g, ln_b)


def _qkv_kernel(x_ref, wkva_ref, kvn_ref, wkvb_ref, wqa_ref, qn_ref, wqb_ref, cos_ref, sin_ref,
                q_ref, k_ref, v_ref, *, scale):
    xb = x_ref[0].astype(BF16)
    cosx = cos_ref[...]
    sinx = sin_ref[...]

    kva = jnp.dot(xb, wkva_ref[...], preferred_element_type=F32)
    ckv = kva[:, :V7X_LANES]
    ckv = ckv * lax.rsqrt(jnp.mean(ckv * ckv, axis=-1, keepdims=True) + RMS_EPS) * kvn_ref[...]
    k_rope = (kva[:, V7X_LANES:2 * V7X_LANES] * cosx
              + kva[:, 2 * V7X_LANES:3 * V7X_LANES] * sinx).astype(BF16)
    kv = jnp.dot(ckv.astype(BF16), wkvb_ref[...], preferred_element_type=F32)

    cq = jnp.dot(xb, wqa_ref[...], preferred_element_type=F32)
    cq = cq * lax.rsqrt(jnp.mean(cq * cq, axis=-1, keepdims=True) + RMS_EPS) * qn_ref[...]
    q = jnp.dot(cq.astype(BF16), wqb_ref[...], preferred_element_type=F32)

    for h in range(MLA_HEADS):
        kb = h * (QK_NOPE + V_DIM)
        k_ref[0, h, :, :QK_NOPE] = kv[:, kb:kb + QK_NOPE].astype(BF16)
        k_ref[0, h, :, QK_NOPE:] = k_rope
        v_ref[0, h] = kv[:, kb + QK_NOPE:kb + QK_NOPE + V_DIM].astype(BF16)
        qb = h * 3 * V7X_LANES
        q_ref[0, h, :, :QK_NOPE] = (q[:, qb:qb + QK_NOPE] * scale).astype(BF16)
        q_rope = (q[:, qb + V7X_LANES:qb + 2 * V7X_LANES] * cosx
                  + q[:, qb + 2 * V7X_LANES:qb + 3 * V7X_LANES] * sinx)
        q_ref[0, h, :, QK_NOPE:] = (q_rope * scale).astype(BF16)


def _qkv_proj(x, wkva, kvn, wkvb, wqa, qn, wqb, cosx, sinx, *, ts):
    n_batch, seq, d_model = x.shape
    scale = 1.0 / math.sqrt(QK_NOPE + QK_ROPE)
    dk = QK_NOPE + V7X_LANES
    kern = functools.partial(_qkv_kernel, scale=scale)
    return pl.pallas_call(
        kern,
        grid=(n_batch, seq // ts),
        in_specs=[
            pl.BlockSpec((1, ts, d_model), lambda b, s: (b, s, 0)),
            _const_spec(wkva.shape), _const_spec(kvn.shape), _const_spec(wkvb.shape),
            _const_spec(wqa.shape), _const_spec(qn.shape), _const_spec(wqb.shape),
            pl.BlockSpec((ts, V7X_LANES), lambda b, s: (s, 0)),
            pl.BlockSpec((ts, V7X_LANES), lambda b, s: (s, 0)),
        ],
        out_specs=[
            pl.BlockSpec((1, MLA_HEADS, ts, dk), lambda b, s: (b, 0, s, 0)),
            pl.BlockSpec((1, MLA_HEADS, ts, dk), lambda b, s: (b, 0, s, 0)),
            pl.BlockSpec((1, MLA_HEADS, ts, V_DIM), lambda b, s: (b, 0, s, 0)),
        ],
        out_shape=[
            jax.ShapeDtypeStruct((n_batch, MLA_HEADS, seq, dk), BF16),
            jax.ShapeDtypeStruct((n_batch, MLA_HEADS, seq, dk), BF16),
            jax.ShapeDtypeStruct((n_batch, MLA_HEADS, seq, V_DIM), BF16),
        ],
        compiler_params=pltpu.CompilerParams(
            dimension_semantics=("parallel", "parallel"), vmem_limit_bytes=V7X_VMEM_LIMIT),
        name="qkv_proj",
    )(x, wkva, kvn, wkvb, wqa, qn, wqb, cosx, sinx)


def _attn_kernel(q_ref, k_ref, v_ref, w_ref, o_ref, wb_ref, *, tq):
    wb_ref[...] = w_ref[...].astype(BF16)
    seq = q_ref.shape[2]
    n_tiles = seq // tq
    row = lax.broadcasted_iota(jnp.int32, (tq, tq), 0)
    col = lax.broadcasted_iota(jnp.int32, (tq, tq), 1)
    causal = col <= row
    for qi in range(n_tiles):
        q = q_ref[0, 0, qi * tq:(qi + 1) * tq, :]
        m = l = acc = None
        for kj in range(qi + 1):
            k = k_ref[0, 0, kj * tq:(kj + 1) * tq, :]
            v = v_ref[0, 0, kj * tq:(kj + 1) * tq, :]
            s = lax.dot_general(q, k, (((1,), (1,)), ((), ())), preferred_element_type=F32)
            if kj == qi:
                s = jnp.where(causal, s, -jnp.inf)
            m_new = jnp.max(s, axis=-1, keepdims=True)
            if kj > 0:
                m_new = jnp.maximum(m, m_new)
            p = jnp.exp(s - m_new)
            pv = jnp.dot(p.astype(BF16), v, preferred_element_type=F32)
            if kj == 0:
                l = jnp.sum(p, axis=-1, keepdims=True)
                acc = pv
            else:
                corr = jnp.exp(m - m_new)
                l = corr * l + jnp.sum(p, axis=-1, keepdims=True)
                acc = corr * acc + pv
            m = m_new
        o_ref[0, qi * tq:(qi + 1) * tq, :] = (acc / l).astype(o_ref.dtype)


def _cast_slab_rows(w2d, n_steps):
    rows = w2d.shape[0] // n_steps
    assert rows * n_steps == w2d.shape[0] and rows % (2 * V7X_SUBLANES) == 0
    return rows


def _attention(q, k, v, w_cast, *, tq):
    n_batch, n_heads, seq, dk = q.shape
    dv = v.shape[3]
    w_rows = _cast_slab_rows(w_cast, n_batch * n_heads)
    w_spec = pl.BlockSpec((w_rows, w_cast.shape[1]), lambda b, h: (b * n_heads + h, 0))
    kern = functools.partial(_attn_kernel, tq=tq)
    return pl.pallas_call(
        kern,
        grid=(n_batch, n_heads),
        in_specs=[
            pl.BlockSpec((1, 1, seq, dk), lambda b, h: (b, h, 0, 0)),
            pl.BlockSpec((1, 1, seq, dk), lambda b, h: (b, h, 0, 0)),
            pl.BlockSpec((1, 1, seq, dv), lambda b, h: (b, h, 0, 0)),
            w_spec,
        ],
        out_specs=[pl.BlockSpec((1, seq, dv), lambda b, h: (b, 0, h)), w_spec],
        out_shape=[jax.ShapeDtypeStruct((n_batch, seq, n_heads * dv), BF16),
                   jax.ShapeDtypeStruct(w_cast.shape, BF16)],
        compiler_params=pltpu.CompilerParams(
            dimension_semantics=("parallel", "parallel"), vmem_limit_bytes=V7X_VMEM_LIMIT),
        name="mla_attention",
    )(q, k, v, w_cast)


def _oproj_router_kernel(o_ref, x_ref, wo_ref, g_ref, b_ref, wrh_ref, wrl_ref, w_ref,
                         y_ref, info_ref, wb_ref):
    wb_ref[...] = w_ref[...].astype(BF16)
    half = info_ref.shape[0] // 2
    for r in range(2):
        rows = pl.ds(r * half, half)
        y, info = _oproj_router_rows(o_ref[0, rows, :], x_ref[0, rows, :], wo_ref, g_ref, b_ref,
                                     wrh_ref, wrl_ref)
        for c in range(V7X_SUBLANES):
            y_ref[pl.ds(r * half * V7X_SUBLANES + c, half, stride=V7X_SUBLANES), :] = (
                y[:, c * V7X_LANES:(c + 1) * V7X_LANES])
        info_ref[rows, :] = info


def _oproj_router_rows(o, x, wo_ref, g_ref, b_ref, wrh_ref, wrl_ref):
    mix = jnp.dot(o, wo_ref[...], preferred_element_type=F32)
    y = _layer_norm(ALPHA * x + mix, g_ref[...], b_ref[...])

    y_hi = y.astype(BF16)
    y_lo = (y - y_hi.astype(F32)).astype(BF16)
    logits = (jnp.dot(y_hi, wrh_ref[...], preferred_element_type=F32)
              + jnp.dot(y_lo, wrh_ref[...], preferred_element_type=F32)
              + jnp.dot(y_hi, wrl_ref[...], preferred_element_type=F32))
    lane = lax.broadcasted_iota(jnp.int32, logits.shape, 1)
    logits = jnp.where(lane < N_EXPERTS, logits, -jnp.inf)
    m1 = jnp.max(logits, axis=-1, keepdims=True)
    i1 = jnp.min(jnp.where(logits == m1, lane, V7X_LANES), axis=-1, keepdims=True)
    rest = jnp.where(lane == i1, -jnp.inf, logits)
    m2 = jnp.max(rest, axis=-1, keepdims=True)
    i2 = jnp.min(jnp.where(rest == m2, lane, V7X_LANES), axis=-1, keepdims=True)
    e2 = jnp.exp(m2 - m1)
    den = 1.0 + e2
    info = jnp.where(lane == 0, i1.astype(F32),
                     jnp.where(lane == 1, i2.astype(F32),
                               jnp.where(lane == 2, 1.0 / den,
                                         jnp.where(lane == 3, e2 / den, 0.0))))
    return y, info


def _oproj_router(o, x, wo, ln_g, ln_b, wr_hi, wr_lo, w_cast, *, ts):
    n_batch, seq, d_attn = o.shape
    d_model = x.shape[2]
    assert d_model == V7X_SUBLANES * V7X_LANES
    nst = seq // ts
    w_rows = _cast_slab_rows(w_cast, n_batch * nst)
    w_spec = pl.BlockSpec((w_rows, w_cast.shape[1]), lambda b, s: (b * nst + s, 0))
    return pl.pallas_call(
        _oproj_router_kernel,
        grid=(n_batch, nst),
        in_specs=[
            pl.BlockSpec((1, ts, d_attn), lambda b, s: (b, s, 0)),
            pl.BlockSpec((1, ts, d_model), lambda b, s: (b, s, 0)),
            _const_spec(wo.shape), _const_spec(ln_g.shape), _const_spec(ln_b.shape),
            _const_spec(wr_hi.shape), _const_spec(wr_lo.shape),
            w_spec,
        ],
        out_specs=[
            pl.BlockSpec((ts * V7X_SUBLANES, V7X_LANES), lambda b, s: (b * nst + s, 0)),
            pl.BlockSpec((ts, V7X_LANES), lambda b, s: (b * nst + s, 0)),
            w_spec,
        ],
        out_shape=[
            jax.ShapeDtypeStruct((n_batch * seq * V7X_SUBLANES, V7X_LANES), F32),
            jax.ShapeDtypeStruct((n_batch * seq, V7X_LANES), F32),
            jax.ShapeDtypeStruct(w_cast.shape, BF16),
        ],
        compiler_params=pltpu.CompilerParams(
            dimension_semantics=("parallel", "parallel"), vmem_limit_bytes=V7X_VMEM_LIMIT),
        name="oproj_router",
    )(o, x, wo, ln_g, ln_b, wr_hi, wr_lo, w_cast)


def _moe_kernel(te_ref, nu_ref, tok_ref, dst_ref, x_hbm, wg_ref, wu_ref, wo_ref, y_hbm,
                xs_ref, ob_ref, xb_ref, acc_ref, gsem, ssem, *, tm, nf, n_rows):
    del te_ref
    s = pl.program_id(0)
    f = pl.program_id(1)
    gslot = s % 2
    cslot = 1 - gslot
    valid = (s >= 1) & (s <= nu_ref[0])
    sub, lanes = V7X_SUBLANES, V7X_LANES

    def wait_tile_rows(buf_ref, sem):
        slot_rows = tm * sub
        pltpu.make_async_copy(
            buf_ref.at[pl.ds(0, slot_rows)], buf_ref.at[pl.ds(slot_rows, slot_rows)], sem).wait()

    @pl.when((s == 0) & (f == 0))
    def _():
        ob_ref[...] = jnp.zeros_like(ob_ref)

    @pl.when((s >= 1) & (f == 0))
    def _():
        wait_tile_rows(xs_ref, gsem)
        for c in range(sub):
            xb_ref[:, c * lanes:(c + 1) * lanes] = (
                xs_ref[pl.ds(cslot * tm * sub + c, tm, stride=sub), :].astype(BF16))

    for parity in range(2):
        @pl.when((f == 0) & (gslot == parity))
        def _(parity=parity):
            for row in range(tm):
                tok = tok_ref[0, 0, row]
                dst = jnp.where(s >= 2, dst_ref[0, 0, row], n_rows + row)
                vmem_rows = pl.ds((parity * tm + row) * sub, sub)
                pltpu.make_async_copy(
                    x_hbm.at[tok], xs_ref.at[vmem_rows], gsem).start(priority=row % 2)
                pltpu.make_async_copy(
                    ob_ref.at[vmem_rows], y_hbm.at[dst], ssem).start(priority=row % 2)

    @pl.when(valid)
    def _():
        xb = xb_ref[...]
        gate = jnp.dot(xb, wg_ref[0], preferred_element_type=F32)
        up = jnp.dot(xb, wu_ref[0], preferred_element_type=F32)
        h = (gate * jax.nn.sigmoid(gate) * up).astype(BF16)
        part = jnp.dot(h, wo_ref[0], preferred_element_type=F32)

        def store_result(res):
            for c in range(sub):
                ob_ref[pl.ds(cslot * tm * sub + c, tm, stride=sub), :] = (
                    res[:, c * lanes:(c + 1) * lanes])

        if nf == 1:
            store_result(part)
        else:
            @pl.when(f == 0)
            def _():
                acc_ref[...] = part

            @pl.when((f > 0) & (f < nf - 1))
            def _():
                acc_ref[...] += part

            @pl.when(f == nf - 1)
            def _():
                store_result(acc_ref[...] + part)

    @pl.when(f == nf - 1)
    def _():
        wait_tile_rows(ob_ref, ssem)

    @pl.when((s == pl.num_programs(0) - 1) & (f == nf - 1))
    def _():
        wait_tile_rows(xs_ref, gsem)


def _moe_experts(tile_expert, n_used, tok_idx, dst_idx, x_tiles, w_in, w_out, *, tm, fc):
    d_model = x_tiles.shape[1] * x_tiles.shape[2]
    edim = w_out.shape[1]
    nf = edim // fc
    n_tiles = tok_idx.shape[0]
    n_rows = n_tiles * tm
    n_steps = n_tiles + 2

    def expert_chunk(s, f, te, nu):
        tile = jnp.clip(s - 1, 0, nu[0] - 1)
        chunk = jnp.where(s > nu[0], nf - 1, jnp.where(s < 1, 0, f))
        return te[tile], chunk

    def w_in_map(half):
        def index(s, f, te, nu):
            e, chunk = expert_chunk(s, f, te, nu)
            return e, 0, chunk + half * nf
        return index

    def w_out_map(s, f, te, nu):
        e, chunk = expert_chunk(s, f, te, nu)
        return e, chunk, 0

    grid_spec = pltpu.PrefetchScalarGridSpec(
        num_scalar_prefetch=2,
        grid=(n_steps, nf),
        in_specs=[
            pl.BlockSpec((1, 1, tm), lambda s, f, te, nu: (jnp.minimum(s, n_tiles - 1), 0, 0),
                         memory_space=pltpu.SMEM),
            pl.BlockSpec((1, 1, tm), lambda s, f, te, nu: (jnp.clip(s - 2, 0, n_tiles - 1), 0, 0),
                         memory_space=pltpu.SMEM),
            pl.BlockSpec(memory_space=pl.ANY),
            pl.BlockSpec((1, d_model, fc), w_in_map(0)),
            pl.BlockSpec((1, d_model, fc), w_in_map(1)),
            pl.BlockSpec((1, fc, d_model), w_out_map),
        ],
        out_specs=pl.BlockSpec(memory_space=pl.ANY),
        scratch_shapes=[
            pltpu.VMEM((2 * tm * V7X_SUBLANES, V7X_LANES), F32),
            pltpu.VMEM((2 * tm * V7X_SUBLANES, V7X_LANES), F32),
            pltpu.VMEM((tm, d_model), BF16), pltpu.VMEM((tm, d_model), F32),
            pltpu.SemaphoreType.DMA(()), pltpu.SemaphoreType.DMA(()),
        ],
    )
    kern = functools.partial(_moe_kernel, tm=tm, nf=nf, n_rows=n_rows)
    return pl.pallas_call(
        kern,
        grid_spec=grid_spec,
        out_shape=jax.ShapeDtypeStruct((n_rows + tm, V7X_SUBLANES, V7X_LANES), F32),
        compiler_params=pltpu.CompilerParams(
            dimension_semantics=("arbitrary", "arbitrary"), vmem_limit_bytes=V7X_VMEM_LIMIT),
        name="moe_experts",
    )(tile_expert, n_used, tok_idx.reshape(n_tiles, 1, tm), dst_idx.reshape(n_tiles, 1, tm),
      x_tiles, w_in, w_in, w_out)


def _combine_kernel(x_ref, info_ref, y0_ref, y1_ref, g_ref, b_ref, o_ref):
    tc, d_model = o_ref.shape
    sub, lanes = V7X_SUBLANES, V7X_LANES
    info = info_ref[...]
    g0, g1 = info[:, 2:3], info[:, 3:4]
    pieces = []
    for c in range(sub):
        rows = pl.ds(c, tc, stride=sub)
        pieces.append(ALPHA * x_ref[rows, :] + (y0_ref[rows, :] * g0 + y1_ref[rows, :] * g1))
    mu = jnp.sum(sum(pieces), axis=-1, keepdims=True) / d_model
    cen = [p - mu for p in pieces]
    var = jnp.sum(sum(c * c for c in cen), axis=-1, keepdims=True) / d_model
    rstd = lax.rsqrt(var + LN_EPS)
    for c in range(sub):
        cols = slice(c * lanes, (c + 1) * lanes)
        o_ref[:, cols] = cen[c] * rstd * g_ref[:, cols] + b_ref[:, cols]


def _combine(x_tiles, info, y_tiles, ln_g, ln_b, *, tc):
    n_tok = info.shape[0]
    d_model = V7X_SUBLANES * V7X_LANES
    nt = n_tok // tc
    tile_block = (tc * V7X_SUBLANES, V7X_LANES)
    return pl.pallas_call(
        _combine_kernel,
        grid=(nt,),
        in_specs=[
            pl.BlockSpec(tile_block, lambda i: (i, 0)),
            pl.BlockSpec((tc, V7X_LANES), lambda i: (i, 0)),
            pl.BlockSpec(tile_block, lambda i: (i, 0)),
            pl.BlockSpec(tile_block, lambda i: (i + nt, 0)),
            _const_spec(ln_g.shape), _const_spec(ln_b.shape),
        ],
        out_specs=pl.BlockSpec((tc, d_model), lambda i: (i, 0)),
        out_shape=jax.ShapeDtypeStruct((n_tok, d_model), F32),
        compiler_params=pltpu.CompilerParams(
            dimension_semantics=("parallel",), vmem_limit_bytes=V7X_VMEM_LIMIT),
        name="moe_combine",
    )(x_tiles, info, y_tiles, y_tiles, ln_g, ln_b)


def _s5_params(lam_re, lam_im, log_step, b_re, b_im, c_re, c_im):
    n_groups, n_state = lam_re.shape
    dt = jnp.exp(log_step)[:, None]
    mag = jnp.exp(lam_re * dt)
    lb_re = mag * jnp.cos(lam_im * dt)
    lb_im = mag * jnp.sin(lam_im * dt)
    den = lam_re * lam_re + lam_im * lam_im
    f_re = ((lb_re - 1.0) * lam_re + lb_im * lam_im) / den
    f_im = (lb_im * lam_re - (lb_re - 1.0) * lam_im) / den
    bb_re = f_re[..., None] * b_re - f_im[..., None] * b_im
    bb_im = f_re[..., None] * b_im + f_im[..., None] * b_re
    gpb = S5_BLOCK_GROUPS
    n_blocks = n_groups // gpb
    eye = jnp.eye(gpb, dtype=F32)

    def in_blocks(w):
        w = w.transpose(0, 2, 1).reshape(n_blocks, gpb, S5_GROUP, n_state)
        return jnp.einsum('kgcp,gh->kgchp', w, eye).reshape(n_blocks, gpb * S5_GROUP, gpb * n_state)

    def out_blocks(w):
        w = w.reshape(n_blocks, gpb, S5_GROUP, n_state)
        return jnp.einsum('kgcp,gh->kgphc', w, eye).reshape(n_blocks, gpb * n_state, gpb * S5_GROUP)

    bb = jnp.concatenate([in_blocks(bb_re), in_blocks(bb_im)], axis=-1).astype(BF16)
    cc = jnp.concatenate([out_blocks(c_re), out_blocks(-c_im)], axis=1).astype(BF16)
    return bb, cc, lb_re.reshape(1, -1), lb_im.reshape(1, -1)


def _rope_lane_tables(seq):
    pos = jnp.arange(seq, dtype=F32)
    inv_freq = ROPE_THETA ** (-jnp.arange(0, QK_ROPE, 2, dtype=F32) / QK_ROPE)
    ang = pos[:, None] * inv_freq[None, :]
    cos, sin = jnp.cos(ang), jnp.sin(ang)
    pad = jnp.zeros((seq, V7X_LANES - QK_ROPE), F32)
    return (jnp.concatenate([cos, cos, pad], axis=-1),
            jnp.concatenate([-sin, sin, pad], axis=-1))


def _rope_weight_blocks(w_rope):
    half = QK_ROPE // 2
    t1, t2 = w_rope[:, :half], w_rope[:, half:]
    pad = jnp.zeros((w_rope.shape[0], V7X_LANES - QK_ROPE), w_rope.dtype)
    return jnp.concatenate([t1, t2, pad, t2, t1, pad], axis=-1)


def _routing(info, *, tm):
    n_tok = info.shape[0]
    n_assign = n_tok * TOP_K
    flat_e = info[:, :TOP_K].astype(jnp.int32).reshape(-1)
    experts = jnp.arange(N_EXPERTS, dtype=jnp.int32)
    counts = jnp.sum((flat_e[:, None] == experts[None, :]).astype(jnp.int32), axis=0)
    pad = (-counts) % tm
    filler = jnp.arange(N_EXPERTS * tm, dtype=jnp.int32)
    filler_key = jnp.where(filler % tm < pad[filler // tm], filler // tm, N_EXPERTS)
    keys = jnp.concatenate([flat_e, filler_key])
    ids = jnp.arange(keys.shape[0], dtype=jnp.int32)
    sorted_keys, order = lax.sort((keys, ids), num_keys=1, is_stable=True)
    is_real = order < n_assign
    tok_idx = jnp.where(is_real, order // TOP_K, 0)
    dst_idx = jnp.where(is_real, (order % TOP_K) * n_tok + order // TOP_K, order)
    tile_key = sorted_keys[::tm]
    tile_expert = jnp.minimum(tile_key, N_EXPERTS - 1)
    n_used = jnp.sum((tile_key < N_EXPERTS).astype(jnp.int32)).reshape(1)
    return tok_idx.reshape(-1, tm), dst_idx.reshape(-1, tm), tile_expert, n_used


def kernel(x, s5_lam_re, s5_lam_im, s5_log_step, s5_b_re, s5_b_im, s5_c_re, s5_c_im, s5_d, s5_w_glu,
           mla_q_w_a, mla_q_norm, mla_q_w_b, mla_o_w, kv_w_a, kv_norm, kv_w_b, ffn_w_in, ffn_w_out,
           moe_router, moe_w_in, moe_w_out, ln_g, ln_b):
    n_batch, seq, d_model = x.shape
    n_tok = n_batch * seq
    row = lambda v: v.reshape(1, -1).astype(F32)

    bb, cc, a_re, a_im = _s5_params(s5_lam_re[0], s5_lam_im[0], s5_log_step[0], s5_b_re[0], s5_b_im[0],
                                    s5_c_re[0], s5_c_im[0])
    n_time = min(32, seq)
    x1 = _s5_layer(x, bb, cc, a_re, a_im, row(s5_d[0]), s5_w_glu[0].astype(BF16),
                   row(ln_g[0, 0]), row(ln_b[0, 0]), n_time=n_time)
    x2 = _dense_ffn(x1, ffn_w_in[0].astype(BF16), ffn_w_out[0].astype(BF16),
                    row(ln_g[0, 1]), row(ln_b[0, 1]), n_batch=n_batch, n_time=n_time,
                    fc=ffn_w_out.shape[1] // 2)

    kv_lora = kv_norm.shape[0]
    wkva = jnp.concatenate([kv_w_a[:, :kv_lora], _rope_weight_blocks(kv_w_a[:, kv_lora:])],
                           axis=-1).astype(BF16)
    wq = mla_q_w_b[0].reshape(-1, MLA_HEADS, QK_NOPE + QK_ROPE)
    wqb = jnp.concatenate(
        [jnp.concatenate([wq[:, h, :QK_NOPE], _rope_weight_blocks(wq[:, h, QK_NOPE:])], axis=-1)
         for h in range(MLA_HEADS)], axis=-1).astype(BF16)
    cosx, sinx = _rope_lane_tables(seq)
    ts = min(512, seq)
    q, k, v = _qkv_proj(x2, wkva, row(kv_norm), kv_w_b.astype(BF16), mla_q_w_a[0].astype(BF16),
                        row(mla_q_norm[0]), wqb, cosx, sinx, ts=ts)
    w_in, w_out = moe_w_in[0], moe_w_out[0]
    o, w_in_bf = _attention(q, k, v, w_in.reshape(-1, w_in.shape[2]), tq=min(512, seq))
    wr = jnp.pad(moe_router[0], ((0, 0), (0, V7X_LANES - N_EXPERTS)))
    wr_hi = wr.astype(BF16)
    wr_lo = (wr - wr_hi.astype(F32)).astype(BF16)
    x3, info, w_out_bf = _oproj_router(o, x2, mla_o_w[0].astype(BF16), row(ln_g[1, 0]),
                                       row(ln_b[1, 0]), wr_hi, wr_lo,
                                       w_out.reshape(-1, w_out.shape[2]), ts=ts)

    tok_idx, dst_idx, tile_expert, n_used = _routing(info, tm=512)
    ys = _moe_experts(tile_expert, n_used, tok_idx, dst_idx,
                      x3.reshape(n_tok, V7X_SUBLANES, V7X_LANES), w_in_bf.reshape(w_in.shape),
                      w_out_bf.reshape(w_out.shape), tm=512, fc=w_out.shape[1] // 2)
    out = _combine(x3, info, ys.reshape(-1, V7X_LANES), row(ln_g[1, 1]), row(ln_b[1, 1]),
                   tc=min(512, n_tok))
    return out.reshape(n_batch, seq, d_model)
```

```python
import functools
import math

import jax
import jax.numpy as jnp
from jax import lax
from jax.experimental import pallas as pl
from jax.experimental.pallas import tpu as pltpu

F32 = jnp.float32
BF16 = jnp.bfloat16

V7X_LANES = 128
V7X_SUBLANES = 8
V7X_VMEM_LIMIT = 56 * 1024 * 1024

DEPTH = 2
ALPHA = (2.0 * DEPTH) ** 0.25
LN_EPS = 1e-5
RMS_EPS = 1e-6
ROPE_THETA = 10000.0

S5_GROUP = 16
S5_STATE = 64
S5_BLOCK_GROUPS = 16

MLA_HEADS = 8
QK_NOPE = 128
QK_ROPE = 64
V_DIM = 128
N_EXPERTS = 8
TOP_K = 2


def _const_spec(shape):
    zeros = (0,) * len(shape)
    return pl.BlockSpec(shape, lambda *_: zeros, pipeline_mode=pl.Buffered(1))


def _layer_norm(h, g, b):
    mu = jnp.mean(h, axis=-1, keepdims=True)
    c = h - mu
    var = jnp.mean(c * c, axis=-1, keepdims=True)
    return c * lax.rsqrt(var + LN_EPS) * g + b


def _gelu_tanh(y):
    return 0.5 * y * (1.0 + jnp.tanh(math.sqrt(2.0 / math.pi) * (y + 0.044715 * (y * y * y))))


def _s5_kernel(x_hbm, bb_ref, cc_ref, are_ref, aim_ref, d_ref, wglu_ref, g_ref, b_ref,
               o_ref, xin_ref, bu_ref, st_ref, sem, *, n_batch, n_time, n_blocks):
    half = bb_ref.shape[2] // 2
    cb = bb_ref.shape[1]
    slab = 512
    d_model = o_ref.shape[1]
    i = pl.program_id(0)
    slot = i % 2

    def chunk_copy(step, b, to_slot):
        return pltpu.make_async_copy(
            x_hbm.at[b, pl.ds(step * n_time, n_time), :], xin_ref.at[to_slot, :, b, :],
            sem.at[to_slot])

    @pl.when(i == 0)
    def _():
        st_ref[...] = jnp.zeros_like(st_ref)
        for b in range(n_batch):
            chunk_copy(0, b, 0).start()

    @pl.when(i + 1 < pl.num_programs(0))
    def _():
        for b in range(n_batch):
            chunk_copy(i + 1, b, 1 - slot).start()

    for b in range(n_batch):
        chunk_copy(i, b, slot).wait()

    x = xin_ref[slot].reshape(n_time * n_batch, d_model)
    xb = x.astype(BF16)
    for k in range(n_blocks):
        bu_ref[:, k * 2 * half:(k + 1) * 2 * half] = jnp.dot(
            xb[:, k * cb:(k + 1) * cb], bb_ref[k], preferred_element_type=F32)

    for k in range(n_blocks):
        for j in range(half // slab):
            re0 = k * 2 * half + j * slab
            im0 = re0 + half
            a0 = k * half + j * slab
            for bh in range(n_batch // V7X_SUBLANES):
                r0 = bh * V7X_SUBLANES
                ar = jnp.broadcast_to(are_ref[:, a0:a0 + slab], (V7X_SUBLANES, slab))
                ai = jnp.broadcast_to(aim_ref[:, a0:a0 + slab], (V7X_SUBLANES, slab))
                s_re = st_ref[r0:r0 + V7X_SUBLANES, re0:re0 + slab]
                s_im = st_ref[r0:r0 + V7X_SUBLANES, im0:im0 + slab]

                for t in range(n_time):
                    rows = pl.ds(t * n_batch + r0, V7X_SUBLANES)
                    b_re = bu_ref[rows, re0:re0 + slab]
                    b_im = bu_ref[rows, im0:im0 + slab]
                    s_re, s_im = (ar * s_re - ai * s_im + b_re, ar * s_im + ai * s_re + b_im)
                    bu_ref[rows, re0:re0 + slab] = s_re
                    bu_ref[rows, im0:im0 + slab] = s_im
                st_ref[r0:r0 + V7X_SUBLANES, re0:re0 + slab] = s_re
                st_ref[r0:r0 + V7X_SUBLANES, im0:im0 + slab] = s_im

    ys = []
    for k in range(n_blocks):
        s_blk = bu_ref[:, k * 2 * half:(k + 1) * 2 * half].astype(BF16)
        ys.append(jnp.dot(s_blk, cc_ref[k], preferred_element_type=F32))
    y = jnp.concatenate(ys, axis=-1) + d_ref[...] * x
    act = _gelu_tanh(y).astype(BF16)
    z = jnp.dot(act, wglu_ref[...], preferred_element_type=F32)
    mix = z[:, :d_model] * jax.nn.sigmoid(z[:, d_model:])
    o_ref[...] = _layer_norm(ALPHA * x + mix, g_ref[...], b_ref[...])


def _s5_layer(x, bb, cc, a_re, a_im, d_skip, w_glu, ln_g, ln_b, *, n_time):
    n_batch, seq, d_model = x.shape
    n_rows = n_batch * seq
    n_blocks = bb.shape[0]
    rows = n_time * n_batch
    state_w = n_blocks * bb.shape[2]
    kern = functools.partial(_s5_kernel, n_batch=n_batch, n_time=n_time, n_blocks=n_blocks)
    return pl.pallas_call(
        kern,
        grid=(n_rows // rows,),
        in_specs=[
            pl.BlockSpec(memory_space=pl.ANY),
            _const_spec(bb.shape), _const_spec(cc.shape),
            _const_spec(a_re.shape), _const_spec(a_im.shape), _const_spec(d_skip.shape),
            _const_spec(w_glu.shape), _const_spec(ln_g.shape), _const_spec(ln_b.shape),
        ],
        out_specs=pl.BlockSpec((rows, d_model), lambda i: (i, 0)),
        out_shape=jax.ShapeDtypeStruct((n_rows, d_model), F32),
        scratch_shapes=[
            pltpu.VMEM((2, n_time, n_batch, d_model), F32),
            pltpu.VMEM((rows, state_w), F32), pltpu.VMEM((n_batch, state_w), F32),
            pltpu.SemaphoreType.DMA((2,)),
        ],
        compiler_params=pltpu.CompilerParams(
            dimension_semantics=("arbitrary",), vmem_limit_bytes=V7X_VMEM_LIMIT),
        name="s5_mixer",
    )(x, bb, cc, a_re, a_im, d_skip, w_glu, ln_g, ln_b)


def _ffn_kernel(x_ref, wg_ref, wu_ref, wo_ref, g_ref, b_ref, o_hbm, obuf_ref, sem, *, fc, n_batch):
    i = pl.program_id(0)
    last = pl.num_programs(0) - 1
    slot = i % 2
    n_time = x_ref.shape[0] // n_batch

    def out_copy(step, b, from_slot):
        return pltpu.make_async_copy(
            obuf_ref.at[from_slot, :, b, :], o_hbm.at[b, pl.ds(step * n_time, n_time), :],
            sem.at[from_slot])

    @pl.when(i >= 2)
    def _():
        for b in range(n_batch):
            out_copy(i - 2, b, slot).wait()

    x = x_ref[...]
    xb = x.astype(BF16)
    acc = None
    for j in range(wo_ref.shape[0] // fc):
        gate = jnp.dot(xb, wg_ref[:, j * fc:(j + 1) * fc], preferred_element_type=F32)
        up = jnp.dot(xb, wu_ref[:, j * fc:(j + 1) * fc], preferred_element_type=F32)
        h = (gate * jax.nn.sigmoid(gate) * up).astype(BF16)
        part = jnp.dot(h, wo_ref[j * fc:(j + 1) * fc, :], preferred_element_type=F32)
        acc = part if acc is None else acc + part
    out = _layer_norm(ALPHA * x + acc, g_ref[...], b_ref[...])
    obuf_ref[slot] = out.reshape(n_time, n_batch, out.shape[1])
    for b in range(n_batch):
        out_copy(i, b, slot).start()

    @pl.when((i == last) & (i >= 1))
    def _():
        for b in range(n_batch):
            out_copy(i - 1, b, 1 - slot).wait()

    @pl.when(i == last)
    def _():
        for b in range(n_batch):
            out_copy(i, b, slot).wait()


def _dense_ffn(x, w_in, w_out, ln_g, ln_b, *, n_batch, n_time, fc):
    n_rows, d_model = x.shape
    ffn = w_out.shape[0]
    tm = n_time * n_batch
    kern = functools.partial(_ffn_kernel, fc=fc, n_batch=n_batch)
    return pl.pallas_call(
        kern,
        grid=(n_rows // tm,),
        in_specs=[
            pl.BlockSpec((tm, d_model), lambda i: (i, 0)),
            pl.BlockSpec((d_model, ffn), lambda i: (0, 0), pipeline_mode=pl.Buffered(1)),
            pl.BlockSpec((d_model, ffn), lambda i: (0, 1), pipeline_mode=pl.Buffered(1)),
            _const_spec(w_out.shape), _const_spec(ln_g.shape), _const_spec(ln_b.shape),
        ],
        out_specs=pl.BlockSpec(memory_space=pl.ANY),
        out_shape=jax.ShapeDtypeStruct((n_batch, n_rows // n_batch, d_model), F32),
        scratch_shapes=[pltpu.VMEM((2, n_time, n_batch, d_model), F32),
                        pltpu.SemaphoreType.DMA((2,))],
        compiler_params=pltpu.CompilerParams(
            dimension_semantics=("arbitrary",), vmem_limit_bytes=V7X_VMEM_LIMIT),
        name="dense_ffn",
    )(x, w_in, w_in, w_out, ln_g, ln_b)


def _qkv_kernel(x_ref, wkva_ref, kvn_ref, wkvb_ref, wqa_ref, qn_ref, wqb_ref, cos_ref, sin_ref,
                q_ref, k_ref, v_ref, *, scale):
    xb = x_ref[0].astype(BF16)
    cosx = cos_ref[...]
    sinx = sin_ref[...]

    kva = jnp.dot(xb, wkva_ref[...], preferred_element_type=F32)
    ckv = kva[:, :V7X_LANES]
    ckv = ckv * lax.rsqrt(jnp.mean(ckv * ckv, axis=-1, keepdims=True) + RMS_EPS) * kvn_ref[...]
    k_rope = (kva[:, V7X_LANES:2 * V7X_LANES] * cosx
              + kva[:, 2 * V7X_LANES:3 * V7X_LANES] * sinx).astype(BF16)
    kv = jnp.dot(ckv.astype(BF16), wkvb_ref[...], preferred_element_type=F32)

    cq = jnp.dot(xb, wqa_ref[...], preferred_element_type=F32)
    cq = cq * lax.rsqrt(jnp.mean(cq * cq, axis=-1, keepdims=True) + RMS_EPS) * qn_ref[...]
    q = jnp.dot(cq.astype(BF16), wqb_ref[...], preferred_element_type=F32)

    for h in range(MLA_HEADS):
        kb = h * (QK_NOPE + V_DIM)
        k_ref[0, h, :, :QK_NOPE] = kv[:, kb:kb + QK_NOPE].astype(BF16)
        k_ref[0, h, :, QK_NOPE:] = k_rope
        v_ref[0, h] = kv[:, kb + QK_NOPE:kb + QK_NOPE + V_DIM].astype(BF16)
        qb = h * 3 * V7X_LANES
        q_ref[0, h, :, :QK_NOPE] = (q[:, qb:qb + QK_NOPE] * scale).astype(BF16)
        q_rope = (q[:, qb + V7X_LANES:qb + 2 * V7X_LANES] * cosx
                  + q[:, qb + 2 * V7X_LANES:qb + 3 * V7X_LANES] * sinx)
        q_ref[0, h, :, QK_NOPE:] = (q_rope * scale).astype(BF16)


def _qkv_proj(x, wkva, kvn, wkvb, wqa, qn, wqb, cosx, sinx, *, ts):
    n_batch, seq, d_model = x.shape
    scale = 1.0 / math.sqrt(QK_NOPE + QK_ROPE)
    dk = QK_NOPE + V7X_LANES
    kern = functools.partial(_qkv_kernel, scale=scale)
    return pl.pallas_call(
        kern,
        grid=(n_batch, seq // ts),
        in_specs=[
            pl.BlockSpec((1, ts, d_model), lambda b, s: (b, s, 0)),
            _const_spec(wkva.shape), _const_spec(kvn.shape), _const_spec(wkvb.shape),
            _const_spec(wqa.shape), _const_spec(qn.shape), _const_spec(wqb.shape),
            pl.BlockSpec((ts, V7X_LANES), lambda b, s: (s, 0)),
            pl.BlockSpec((ts, V7X_LANES), lambda b, s: (s, 0)),
        ],
        out_specs=[
            pl.BlockSpec((1, MLA_HEADS, ts, dk), lambda b, s: (b, 0, s, 0)),
            pl.BlockSpec((1, MLA_HEADS, ts, dk), lambda b, s: (b, 0, s, 0)),
            pl.BlockSpec((1, MLA_HEADS, ts, V_DIM), lambda b, s: (b, 0, s, 0)),
        ],
        out_shape=[
            jax.ShapeDtypeStruct((n_batch, MLA_HEADS, seq, dk), BF16),
            jax.ShapeDtypeStruct((n_batch, MLA_HEADS, seq, dk), BF16),
            jax.ShapeDtypeStruct((n_batch, MLA_HEADS, seq, V_DIM), BF16),
        ],
        compiler_params=pltpu.CompilerParams(
            dimension_semantics=("parallel", "parallel"), vmem_limit_bytes=V7X_VMEM_LIMIT),
        name="qkv_proj",
    )(x, wkva, kvn, wkvb, wqa, qn, wqb, cosx, sinx)


def _attn_kernel(q_ref, k_ref, v_ref, w_ref, o_ref, wb_ref, *, tq):
    wb_ref[...] = w_ref[...].astype(BF16)
    seq = q_ref.shape[2]
    n_tiles = seq // tq
    row = lax.broadcasted_iota(jnp.int32, (tq, tq), 0)
    col = lax.broadcasted_iota(jnp.int32, (tq, tq), 1)
    causal = col <= row
    for qi in range(n_tiles):
        q = q_ref[0, 0, qi * tq:(qi + 1) * tq, :]
        m = l = acc = None
        for kj in range(qi + 1):
            k = k_ref[0, 0, kj * tq:(kj + 1) * tq, :]
            v = v_ref[0, 0, kj * tq:(kj + 1) * tq, :]
            s = lax.dot_general(q, k, (((1,), (1,)), ((), ())), preferred_element_type=F32)
            if kj == qi:
                s = jnp.where(causal, s, -jnp.inf)
            m_new = jnp.max(s, axis=-1, keepdims=True)
            if kj > 0:
                m_new = jnp.maximum(m, m_new)
            p = jnp.exp(s - m_new)
            pv = jnp.dot(p.astype(BF16), v, preferred_element_type=F32)
            if kj == 0:
                l = jnp.sum(p, axis=-1, keepdims=True)
                acc = pv
            else:
                corr = jnp.exp(m - m_new)
                l = corr * l + jnp.sum(p, axis=-1, keepdims=True)
                acc = corr * acc + pv
            m = m_new
        o_ref[0, qi * tq:(qi + 1) * tq, :] = (acc / l).astype(o_ref.dtype)


def _cast_slab_rows(w2d, n_steps):
    rows = w2d.shape[0] // n_steps
    assert rows * n_steps == w2d.shape[0] and rows % (2 * V7X_SUBLANES) == 0
    return rows


def _attention(q, k, v, w_cast, *, tq):
    n_batch, n_heads, seq, dk = q.shape
    dv = v.shape[3]
    w_rows = _cast_slab_rows(w_cast, n_batch * n_heads)
    w_spec = pl.BlockSpec((w_rows, w_cast.shape[1]), lambda b, h: (b * n_heads + h, 0))
    kern = functools.partial(_attn_kernel, tq=tq)
    return pl.pallas_call(
        kern,
        grid=(n_batch, n_heads),
        in_specs=[
            pl.BlockSpec((1, 1, seq, dk), lambda b, h: (b, h, 0, 0)),
            pl.BlockSpec((1, 1, seq, dk), lambda b, h: (b, h, 0, 0)),
            pl.BlockSpec((1, 1, seq, dv), lambda b, h: (b, h, 0, 0)),
            w_spec,
        ],
        out_specs=[pl.BlockSpec((1, seq, dv), lambda b, h: (b, 0, h)), w_spec],
        out_shape=[jax.ShapeDtypeStruct((n_batch, seq, n_heads * dv), BF16),
                   jax.ShapeDtypeStruct(w_cast.shape, BF16)],
        compiler_params=pltpu.CompilerParams(
            dimension_semantics=("parallel", "parallel"), vmem_limit_bytes=V7X_VMEM_LIMIT),
        name="mla_attention",
    )(q, k, v, w_cast)


def _oproj_router_kernel(o_ref, x_ref, wo_ref, g_ref, b_ref, wrh_ref, wrl_ref, w_ref,
                         y_ref, info_ref, wb_ref):
    wb_ref[...] = w_ref[...].astype(BF16)
    half = info_ref.shape[0] // 2
    for r in range(2):
        rows = pl.ds(r * half, half)
        y, info = _oproj_router_rows(o_ref[0, rows, :], x_ref[0, rows, :], wo_ref, g_ref, b_ref,
                                     wrh_ref, wrl_ref)
        for c in range(V7X_SUBLANES):
            y_ref[pl.ds(r * half * V7X_SUBLANES + c, half, stride=V7X_SUBLANES), :] = (
                y[:, c * V7X_LANES:(c + 1) * V7X_LANES])
        info_ref[rows, :] = info


def _oproj_router_rows(o, x, wo_ref, g_ref, b_ref, wrh_ref, wrl_ref):
    mix = jnp.dot(o, wo_ref[...], preferred_element_type=F32)
    y = _layer_norm(ALPHA * x + mix, g_ref[...], b_ref[...])

    y_hi = y.astype(BF16)
    y_lo = (y - y_hi.astype(F32)).astype(BF16)
    logits = (jnp.dot(y_hi, wrh_ref[...], preferred_element_type=F32)
              + jnp.dot(y_lo, wrh_ref[...], preferred_element_type=F32)
              + jnp.dot(y_hi, wrl_ref[...], preferred_element_type=F32))
    lane = lax.broadcasted_iota(jnp.int32, logits.shape, 1)
    logits = jnp.where(lane < N_EXPERTS, logits, -jnp.inf)
    m1 = jnp.max(logits, axis=-1, keepdims=True)
    i1 = jnp.min(jnp.where(logits == m1, lane, V7X_LANES), axis=-1, keepdims=True)
    rest = jnp.where(lane == i1, -jnp.inf, logits)
    m2 = jnp.max(rest, axis=-1, keepdims=True)
    i2 = jnp.min(jnp.where(rest == m2, lane, V7X_LANES), axis=-1, keepdims=True)
    e2 = jnp.exp(m2 - m1)
    den = 1.0 + e2
    info = jnp.where(lane == 0, i1.astype(F32),
                     jnp.where(lane == 1, i2.astype(F32),
                               jnp.where(lane == 2, 1.0 / den,
                                         jnp.where(lane == 3, e2 / den, 0.0))))
    return y, info


def _oproj_router(o, x, wo, ln_g, ln_b, wr_hi, wr_lo, w_cast, *, ts):
    n_batch, seq, d_attn = o.shape
    d_model = x.shape[2]
    assert d_model == V7X_SUBLANES * V7X_LANES
    nst = seq // ts
    w_rows = _cast_slab_rows(w_cast, n_batch * nst)
    w_spec = pl.BlockSpec((w_rows, w_cast.shape[1]), lambda b, s: (b * nst + s, 0))
    return pl.pallas_call(
        _oproj_router_kernel,
        grid=(n_batch, nst),
        in_specs=[
            pl.BlockSpec((1, ts, d_attn), lambda b, s: (b, s, 0)),
            pl.BlockSpec((1, ts, d_model), lambda b, s: (b, s, 0)),
            _const_spec(wo.shape), _const_spec(ln_g.shape), _const_spec(ln_b.shape),
            _const_spec(wr_hi.shape), _const_spec(wr_lo.shape),
            w_spec,
        ],
        out_specs=[
            pl.BlockSpec((ts * V7X_SUBLANES, V7X_LANES), lambda b, s: (b * nst + s, 0)),
            pl.BlockSpec((ts, V7X_LANES), lambda b, s: (b * nst + s, 0)),
            w_spec,
        ],
        out_shape=[
            jax.ShapeDtypeStruct((n_batch * seq * V7X_SUBLANES, V7X_LANES), F32),
            jax.ShapeDtypeStruct((n_batch * seq, V7X_LANES), F32),
            jax.ShapeDtypeStruct(w_cast.shape, BF16),
        ],
        compiler_params=pltpu.CompilerParams(
            dimension_semantics=("parallel", "parallel"), vmem_limit_bytes=V7X_VMEM_LIMIT),
        name="oproj_router",
    )(o, x, wo, ln_g, ln_b, wr_hi, wr_lo, w_cast)


def _moe_kernel(te_ref, nu_ref, tok_ref, dst_ref, x_hbm, wg_ref, wu_ref, wo_ref, y_hbm,
                xs_ref, ob_ref, xb_ref, acc_ref, gsem, ssem, *, tm, nf):
    del te_ref
    s = pl.program_id(0)
    f = pl.program_id(1)
    gslot = s % 2
    cslot = 1 - gslot
    valid = (s >= 1) & (s <= nu_ref[0])
    sub, lanes = V7X_SUBLANES, V7X_LANES

    def wait_tile_rows(buf_ref, sem):
        slot_rows = tm * sub
        pltpu.make_async_copy(
            buf_ref.at[pl.ds(0, slot_rows)], buf_ref.at[pl.ds(slot_rows, slot_rows)], sem).wait()

    @pl.when((s == 0) & (f == 0))
    def _():
        ob_ref[...] = jnp.zeros_like(ob_ref)

    @pl.when((s >= 1) & (f == 0))
    def _():
        wait_tile_rows(xs_ref, gsem)
        for c in range(sub):
            xb_ref[:, c * lanes:(c + 1) * lanes] = (
                xs_ref[pl.ds(cslot * tm * sub + c, tm, stride=sub), :].astype(BF16))

    for parity in range(2):
        @pl.when((f == 0) & (gslot == parity))
        def _(parity=parity):
            for row in range(tm):
                tok = tok_ref[0, 0, row]
                dst = dst_ref[0, 0, row]
                vmem_rows = pl.ds((parity * tm + row) * sub, sub)
                pltpu.make_async_copy(
                    x_hbm.at[tok], xs_ref.at[vmem_rows], gsem).start(priority=row % 2)
                pltpu.make_async_copy(
                    ob_ref.at[vmem_rows], y_hbm.at[dst], ssem).start(priority=row % 2)

    @pl.when(valid)
    def _():
        xb = xb_ref[...]
        gate = jnp.dot(xb, wg_ref[0], preferred_element_type=F32)
        up = jnp.dot(xb, wu_ref[0], preferred_element_type=F32)
        h = (gate * jax.nn.sigmoid(gate) * up).astype(BF16)
        part = jnp.dot(h, wo_ref[0], preferred_element_type=F32)

        def store_result(res):
            for c in range(sub):
                ob_ref[pl.ds(cslot * tm * sub + c, tm, stride=sub), :] = (
                    res[:, c * lanes:(c + 1) * lanes])

        if nf == 1:
            store_result(part)
        else:
            @pl.when(f == 0)
            def _():
                acc_ref[...] = part

            @pl.when((f > 0) & (f < nf - 1))
            def _():
                acc_ref[...] += part

            @pl.when(f == nf - 1)
            def _():
                store_result(acc_ref[...] + part)

    @pl.when(f == nf - 1)
    def _():
        wait_tile_rows(ob_ref, ssem)

    @pl.when((s == pl.num_programs(0) - 1) & (f == nf - 1))
    def _():
        wait_tile_rows(xs_ref, gsem)


def _moe_experts(tile_expert, n_used, tok_idx, dst_idx, x_tiles, w_in, w_out, *, tm, fc):
    d_model = x_tiles.shape[1] * x_tiles.shape[2]
    edim = w_out.shape[1]
    nf = edim // fc
    n_tiles = tok_idx.shape[0]
    n_rows = n_tiles * tm
    n_steps = n_tiles + 2

    def expert_chunk(s, f, te, nu):
        tile = jnp.clip(s - 1, 0, nu[0] - 1)
        chunk = jnp.where(s > nu[0], nf - 1, jnp.where(s < 1, 0, f))
        return te[tile], chunk

    def w_in_map(half):
        def index(s, f, te, nu):
            e, chunk = expert_chunk(s, f, te, nu)
            return e, 0, chunk + half * nf
        return index

    def w_out_map(s, f, te, nu):
        e, chunk = expert_chunk(s, f, te, nu)
        return e, chunk, 0

    grid_spec = pltpu.PrefetchScalarGridSpec(
        num_scalar_prefetch=2,
        grid=(n_steps, nf),
        in_specs=[
            pl.BlockSpec((1, 1, tm), lambda s, f, te, nu: (jnp.minimum(s, n_tiles - 1), 0, 0),
                         memory_space=pltpu.SMEM),
            pl.BlockSpec((1, 1, tm), lambda s, f, te, nu: (s, 0, 0), memory_space=pltpu.SMEM),
            pl.BlockSpec(memory_space=pl.ANY),
            pl.BlockSpec((1, d_model, fc), w_in_map(0)),
            pl.BlockSpec((1, d_model, fc), w_in_map(1)),
            pl.BlockSpec((1, fc, d_model), w_out_map),
        ],
        out_specs=pl.BlockSpec(memory_space=pl.ANY),
        scratch_shapes=[
            pltpu.VMEM((2 * tm * V7X_SUBLANES, V7X_LANES), F32),
            pltpu.VMEM((2 * tm * V7X_SUBLANES, V7X_LANES), F32),
            pltpu.VMEM((tm, d_model), BF16), pltpu.VMEM((tm, d_model), F32),
            pltpu.SemaphoreType.DMA(()), pltpu.SemaphoreType.DMA(()),
        ],
    )
    warm = n_rows + jnp.arange(tm, dtype=jnp.int32)[None, :]
    dst_by_step = jnp.concatenate([warm, warm, dst_idx], axis=0)
    kern = functools.partial(_moe_kernel, tm=tm, nf=nf)
    return pl.pallas_call(
        kern,
        grid_spec=grid_spec,
        out_shape=jax.ShapeDtypeStruct((n_rows + tm, V7X_SUBLANES, V7X_LANES), F32),
        compiler_params=pltpu.CompilerParams(
            dimension_semantics=("arbitrary", "arbitrary"), vmem_limit_bytes=V7X_VMEM_LIMIT),
        name="moe_experts",
    )(tile_expert, n_used, tok_idx.reshape(n_tiles, 1, tm), dst_by_step.reshape(n_steps, 1, tm),
      x_tiles, w_in, w_in, w_out)


def _combine_kernel(x_ref, info_ref, y0_ref, y1_ref, g_ref, b_ref, o_ref):
    tc, d_model = o_ref.shape
    sub, lanes = V7X_SUBLANES, V7X_LANES
    info = info_ref[...]
    g0, g1 = info[:, 2:3], info[:, 3:4]
    pieces = []
    for c in range(sub):
        rows = pl.ds(c, tc, stride=sub)
        pieces.append(ALPHA * x_ref[rows, :] + (y0_ref[rows, :] * g0 + y1_ref[rows, :] * g1))
    mu = jnp.sum(sum(pieces), axis=-1, keepdims=True) / d_model
    cen = [p - mu for p in pieces]
    var = jnp.sum(sum(c * c for c in cen), axis=-1, keepdims=True) / d_model
    rstd = lax.rsqrt(var + LN_EPS)
    for c in range(sub):
        cols = slice(c * lanes, (c + 1) * lanes)
        o_ref[:, cols] = cen[c] * rstd * g_ref[:, cols] + b_ref[:, cols]


def _combine(x_tiles, info, y_tiles, ln_g, ln_b, *, tc):
    n_tok = info.shape[0]
    d_model = V7X_SUBLANES * V7X_LANES
    nt = n_tok // tc
    tile_block = (tc * V7X_SUBLANES, V7X_LANES)
    return pl.pallas_call(
        _combine_kernel,
        grid=(nt,),
        in_specs=[
            pl.BlockSpec(tile_block, lambda i: (i, 0)),
            pl.BlockSpec((tc, V7X_LANES), lambda i: (i, 0)),
            pl.BlockSpec(tile_block, lambda i: (i, 0)),
            pl.BlockSpec(tile_block, lambda i: (i + nt, 0)),
            _const_spec(ln_g.shape), _const_spec(ln_b.shape),
        ],
        out_specs=pl.BlockSpec((tc, d_model), lambda i: (i, 0)),
        out_shape=jax.ShapeDtypeStruct((n_tok, d_model), F32),
        compiler_params=pltpu.CompilerParams(
            dimension_semantics=("parallel",), vmem_limit_bytes=V7X_VMEM_LIMIT),
        name="moe_combine",
    )(x_tiles, info, y_tiles, y_tiles, ln_g, ln_b)


def _s5_params(lam_re, lam_im, log_step, b_re, b_im, c_re, c_im):
    n_groups, n_state = lam_re.shape
    dt = jnp.exp(log_step)[:, None]
    mag = jnp.exp(lam_re * dt)
    lb_re = mag * jnp.cos(lam_im * dt)
    lb_im = mag * jnp.sin(lam_im * dt)
    den = lam_re * lam_re + lam_im * lam_im
    f_re = ((lb_re - 1.0) * lam_re + lb_im * lam_im) / den
    f_im = (lb_im * lam_re - (lb_re - 1.0) * lam_im) / den
    bb_re = f_re[..., None] * b_re - f_im[..., None] * b_im
    bb_im = f_re[..., None] * b_im + f_im[..., None] * b_re
    gpb = S5_BLOCK_GROUPS
    n_blocks = n_groups // gpb
    eye = jnp.eye(gpb, dtype=F32)

    def in_blocks(w):
        w = w.transpose(0, 2, 1).reshape(n_blocks, gpb, S5_GROUP, n_state)
        return jnp.einsum('kgcp,gh->kgchp', w, eye).reshape(n_blocks, gpb * S5_GROUP, gpb * n_state)

    def out_blocks(w):
        w = w.reshape(n_blocks, gpb, S5_GROUP, n_state)
        return jnp.einsum('kgcp,gh->kgphc', w, eye).reshape(n_blocks, gpb * n_state, gpb * S5_GROUP)

    bb = jnp.concatenate([in_blocks(bb_re), in_blocks(bb_im)], axis=-1).astype(BF16)
    cc = jnp.concatenate([out_blocks(c_re), out_blocks(-c_im)], axis=1).astype(BF16)
    return bb, cc, lb_re.reshape(1, -1), lb_im.reshape(1, -1)


def _rope_lane_tables(seq):
    pos = jnp.arange(seq, dtype=F32)
    inv_freq = ROPE_THETA ** (-jnp.arange(0, QK_ROPE, 2, dtype=F32) / QK_ROPE)
    ang = pos[:, None] * inv_freq[None, :]
    cos, sin = jnp.cos(ang), jnp.sin(ang)
    pad = jnp.zeros((seq, V7X_LANES - QK_ROPE), F32)
    return (jnp.concatenate([cos, cos, pad], axis=-1),
            jnp.concatenate([-sin, sin, pad], axis=-1))


def _rope_weight_blocks(w_rope):
    half = QK_ROPE // 2
    t1, t2 = w_rope[:, :half], w_rope[:, half:]
    pad = jnp.zeros((w_rope.shape[0], V7X_LANES - QK_ROPE), w_rope.dtype)
    return jnp.concatenate([t1, t2, pad, t2, t1, pad], axis=-1)


def _routing(info, *, tm):
    n_tok = info.shape[0]
    n_assign = n_tok * TOP_K
    flat_e = info[:, :TOP_K].astype(jnp.int32).reshape(-1)
    experts = jnp.arange(N_EXPERTS, dtype=jnp.int32)
    counts = jnp.sum((flat_e[:, None] == experts[None, :]).astype(jnp.int32), axis=0)
    pad = (-counts) % tm
    filler = jnp.arange(N_EXPERTS * tm, dtype=jnp.int32)
    filler_key = jnp.where(filler % tm < pad[filler // tm], filler // tm, N_EXPERTS)
    keys = jnp.concatenate([flat_e, filler_key])
    ids = jnp.arange(keys.shape[0], dtype=jnp.int32)
    sorted_keys, order = lax.sort((keys, ids), num_keys=1, is_stable=True)
    is_real = order < n_assign
    tok_idx = jnp.where(is_real, order // TOP_K, 0)
    dst_idx = jnp.where(is_real, (order % TOP_K) * n_tok + order // TOP_K, order)
    tile_key = sorted_keys[::tm]
    tile_expert = jnp.minimum(tile_key, N_EXPERTS - 1)
    n_used = jnp.sum((tile_key < N_EXPERTS).astype(jnp.int32)).reshape(1)
    return tok_idx.reshape(-1, tm), dst_idx.reshape(-1, tm), tile_expert, n_used


def kernel(x, s5_lam_re, s5_lam_im, s5_log_step, s5_b_re, s5_b_im, s5_c_re, s5_c_im, s5_d, s5_w_glu,
           mla_q_w_a, mla_q_norm, mla_q_w_b, mla_o_w, kv_w_a, kv_norm, kv_w_b, ffn_w_in, ffn_w_out,
           moe_router, moe_w_in, moe_w_out, ln_g, ln_b):
    n_batch, seq, d_model = x.shape
    n_tok = n_batch * seq
    row = lambda v: v.reshape(1, -1).astype(F32)

    bb, cc, a_re, a_im = _s5_params(s5_lam_re[0], s5_lam_im[0], s5_log_step[0], s5_b_re[0], s5_b_im[0],
                                    s5_c_re[0], s5_c_im[0])
    n_time = min(32, seq)
    x1 = _s5_layer(x, bb, cc, a_re, a_im, row(s5_d[0]), s5_w_glu[0].astype(BF16),
                   row(ln_g[0, 0]), row(ln_b[0, 0]), n_time=n_time)
    x2 = _dense_ffn(x1, ffn_w_in[0].astype(BF16), ffn_w_out[0].astype(BF16),
                    row(ln_g[0, 1]), row(ln_b[0, 1]), n_batch=n_batch, n_time=n_time,
                    fc=ffn_w_out.shape[1] // 2)

    kv_lora = kv_norm.shape[0]
    wkva = jnp.concatenate([kv_w_a[:, :kv_lora], _rope_weight_blocks(kv_w_a[:, kv_lora:])],
                           axis=-1).astype(BF16)
    wq = mla_q_w_b[0].reshape(-1, MLA_HEADS, QK_NOPE + QK_ROPE)
    wqb = jnp.concatenate(
        [jnp.concatenate([wq[:, h, :QK_NOPE], _rope_weight_blocks(wq[:, h, QK_NOPE:])], axis=-1)
         for h in range(MLA_HEADS)], axis=-1).astype(BF16)
    cosx, sinx = _rope_lane_tables(seq)
    ts = min(512, seq)
    q, k, v = _qkv_proj(x2, wkva, row(kv_norm), kv_w_b.astype(BF16), mla_q_w_a[0].astype(BF16),
                        row(mla_q_norm[0]), wqb, cosx, sinx, ts=ts)
    w_in, w_out = moe_w_in[0], moe_w_out[0]
    o, w_in_bf = _attention(q, k, v, w_in.reshape(-1, w_in.shape[2]), tq=min(512, seq))
    wr = jnp.pad(moe_router[0], ((0, 0), (0, V7X_LANES - N_EXPERTS)))
    wr_hi = wr.astype(BF16)
    wr_lo = (wr - wr_hi.astype(F32)).astype(BF16)
    x3, info, w_out_bf = _oproj_router(o, x2, mla_o_w[0].astype(BF16), row(ln_g[1, 0]),
                                       row(ln_b[1, 0]), wr_hi, wr_lo,
                                       w_out.reshape(-1, w_out.shape[2]), ts=ts)

    tok_idx, dst_idx, tile_expert, n_used = _routing(info, tm=512)
    ys = _moe_experts(tile_expert, n_used, tok_idx, dst_idx,
                      x3.reshape(n_tok, V7X_SUBLANES, V7X_LANES), w_in_bf.reshape(w_in.shape),
                      w_out_bf.reshape(w_out.shape), tm=512, fc=w_out.shape[1] // 2)
    out = _combine(x3, info, ys.reshape(-1, V7X_LANES), row(ln_g[1, 1]), row(ln_b[1, 1]),
                   tc=min(512, n_tok))
    return out.reshape(n_batch, seq, d_model)
```

```python
import functools
import math

import jax
import jax.numpy as jnp
from jax import lax
from jax.experimental import pallas as pl
from jax.experimental.pallas import tpu as pltpu

F32 = jnp.float32
BF16 = jnp.bfloat16

V7X_LANES = 128
V7X_SUBLANES = 8
V7X_VMEM_LIMIT = 56 * 1024 * 1024

DEPTH = 2
ALPHA = (2.0 * DEPTH) ** 0.25
LN_EPS = 1e-5
RMS_EPS = 1e-6
ROPE_THETA = 10000.0

S5_GROUP = 16
S5_STATE = 64
S5_BLOCK_GROUPS = 16

MLA_HEADS = 8
QK_NOPE = 128
QK_ROPE = 64
V_DIM = 128
N_EXPERTS = 8
TOP_K = 2


def _const_spec(shape):
    zeros = (0,) * len(shape)
    return pl.BlockSpec(shape, lambda *_: zeros, pipeline_mode=pl.Buffered(1))


def _layer_norm(h, g, b):
    mu = jnp.mean(h, axis=-1, keepdims=True)
    c = h - mu
    var = jnp.mean(c * c, axis=-1, keepdims=True)
    return c * lax.rsqrt(var + LN_EPS) * g + b


def _gelu_tanh(y):
    return 0.5 * y * (1.0 + jnp.tanh(math.sqrt(2.0 / math.pi) * (y + 0.044715 * (y * y * y))))


def _s5_kernel(x_hbm, bb_ref, cc_ref, are_ref, aim_ref, d_ref, wglu_ref, g_ref, b_ref,
               o_ref, xin_ref, bu_ref, st_ref, sem, *, n_batch, n_time, n_blocks):
    half = bb_ref.shape[2] // 2
    cb = bb_ref.shape[1]
    slab = 512
    d_model = o_ref.shape[1]
    i = pl.program_id(0)
    slot = i % 2

    def chunk_copy(step, b, to_slot):
        return pltpu.make_async_copy(
            x_hbm.at[b, pl.ds(step * n_time, n_time), :], xin_ref.at[to_slot, :, b, :],
            sem.at[to_slot])

    @pl.when(i == 0)
    def _():
        st_ref[...] = jnp.zeros_like(st_ref)
        for b in range(n_batch):
            chunk_copy(0, b, 0).start()

    @pl.when(i + 1 < pl.num_programs(0))
    def _():
        for b in range(n_batch):
            chunk_copy(i + 1, b, 1 - slot).start()

    for b in range(n_batch):
        chunk_copy(i, b, slot).wait()

    x = xin_ref[slot].reshape(n_time * n_batch, d_model)
    xb = x.astype(BF16)
    for k in range(n_blocks):
        bu_ref[:, k * 2 * half:(k + 1) * 2 * half] = jnp.dot(
            xb[:, k * cb:(k + 1) * cb], bb_ref[k], preferred_element_type=F32)

    for k in range(n_blocks):
        for j in range(half // slab):
            re0 = k * 2 * half + j * slab
            im0 = re0 + half
            a0 = k * half + j * slab
            for bh in range(n_batch // V7X_SUBLANES):
                r0 = bh * V7X_SUBLANES
                ar = jnp.broadcast_to(are_ref[:, a0:a0 + slab], (V7X_SUBLANES, slab))
                ai = jnp.broadcast_to(aim_ref[:, a0:a0 + slab], (V7X_SUBLANES, slab))
                s_re = st_ref[r0:r0 + V7X_SUBLANES, re0:re0 + slab]
                s_im = st_ref[r0:r0 + V7X_SUBLANES, im0:im0 + slab]

                for t in range(n_time):
                    rows = pl.ds(t * n_batch + r0, V7X_SUBLANES)
                    b_re = bu_ref[rows, re0:re0 + slab]
                    b_im = bu_ref[rows, im0:im0 + slab]
                    s_re, s_im = (ar * s_re - ai * s_im + b_re, ar * s_im + ai * s_re + b_im)
                    bu_ref[rows, re0:re0 + slab] = s_re
                    bu_ref[rows, im0:im0 + slab] = s_im
                st_ref[r0:r0 + V7X_SUBLANES, re0:re0 + slab] = s_re
                st_ref[r0:r0 + V7X_SUBLANES, im0:im0 + slab] = s_im

    ys = []
    for k in range(n_blocks):
        s_blk = bu_ref[:, k * 2 * half:(k + 1) * 2 * half].astype(BF16)
        ys.append(jnp.dot(s_blk, cc_ref[k], preferred_element_type=F32))
    y = jnp.concatenate(ys, axis=-1) + d_ref[...] * x
    act = _gelu_tanh(y).astype(BF16)
    z = jnp.dot(act, wglu_ref[...], preferred_element_type=F32)
    mix = z[:, :d_model] * jax.nn.sigmoid(z[:, d_model:])
    o_ref[...] = _layer_norm(ALPHA * x + mix, g_ref[...], b_ref[...])


def _s5_layer(x, bb, cc, a_re, a_im, d_skip, w_glu, ln_g, ln_b, *, n_time):
    n_batch, seq, d_model = x.shape
    n_rows = n_batch * seq
    n_blocks = bb.shape[0]
    rows = n_time * n_batch
    state_w = n_blocks * bb.shape[2]
    kern = functools.partial(_s5_kernel, n_batch=n_batch, n_time=n_time, n_blocks=n_blocks)
    return pl.pallas_call(
        kern,
        grid=(n_rows // rows,),
        in_specs=[
            pl.BlockSpec(memory_space=pl.ANY),
            _const_spec(bb.shape), _const_spec(cc.shape),
            _const_spec(a_re.shape), _const_spec(a_im.shape), _const_spec(d_skip.shape),
            _const_spec(w_glu.shape), _const_spec(ln_g.shape), _const_spec(ln_b.shape),
        ],
        out_specs=pl.BlockSpec((rows, d_model), lambda i: (i, 0)),
        out_shape=jax.ShapeDtypeStruct((n_rows, d_model), F32),
        scratch_shapes=[
            pltpu.VMEM((2, n_time, n_batch, d_model), F32),
            pltpu.VMEM((rows, state_w), F32), pltpu.VMEM((n_batch, state_w), F32),
            pltpu.SemaphoreType.DMA((2,)),
        ],
        compiler_params=pltpu.CompilerParams(
            dimension_semantics=("arbitrary",), vmem_limit_bytes=V7X_VMEM_LIMIT),
        name="s5_mixer",
    )(x, bb, cc, a_re, a_im, d_skip, w_glu, ln_g, ln_b)


def _ffn_kernel(x_ref, wg_ref, wu_ref, wo_ref, g_ref, b_ref, o_hbm, obuf_ref, sem, *, fc, n_batch):
    i = pl.program_id(0)
    last = pl.num_programs(0) - 1
    slot = i % 2
    n_time = x_ref.shape[0] // n_batch

    def out_copy(step, b, from_slot):
        return pltpu.make_async_copy(
            obuf_ref.at[from_slot, :, b, :], o_hbm.at[b, pl.ds(step * n_time, n_time), :],
            sem.at[from_slot])

    @pl.when(i >= 2)
    def _():
        for b in range(n_batch):
            out_copy(i - 2, b, slot).wait()

    x = x_ref[...]
    xb = x.astype(BF16)
    acc = None
    for j in range(wo_ref.shape[0] // fc):
        gate = jnp.dot(xb, wg_ref[:, j * fc:(j + 1) * fc], preferred_element_type=F32)
        up = jnp.dot(xb, wu_ref[:, j * fc:(j + 1) * fc], preferred_element_type=F32)
        h = (gate * jax.nn.sigmoid(gate) * up).astype(BF16)
        part = jnp.dot(h, wo_ref[j * fc:(j + 1) * fc, :], preferred_element_type=F32)
        acc = part if acc is None else acc + part
    out = _layer_norm(ALPHA * x + acc, g_ref[...], b_ref[...])
    obuf_ref[slot] = out.reshape(n_time, n_batch, out.shape[1])
    for b in range(n_batch):
        out_copy(i, b, slot).start()

    @pl.when((i == last) & (i >= 1))
    def _():
        for b in range(n_batch):
            out_copy(i - 1, b, 1 - slot).wait()

    @pl.when(i == last)
    def _():
        for b in range(n_batch):
            out_copy(i, b, slot).wait()


def _dense_ffn(x, w_in, w_out, ln_g, ln_b, *, n_batch, n_time, fc):
    n_rows, d_model = x.shape
    ffn = w_out.shape[0]
    tm = n_time * n_batch
    kern = functools.partial(_ffn_kernel, fc=fc, n_batch=n_batch)
    return pl.pallas_call(
        kern,
        grid=(n_rows // tm,),
        in_specs=[
            pl.BlockSpec((tm, d_model), lambda i: (i, 0)),
            pl.BlockSpec((d_model, ffn), lambda i: (0, 0), pipeline_mode=pl.Buffered(1)),
            pl.BlockSpec((d_model, ffn), lambda i: (0, 1), pipeline_mode=pl.Buffered(1)),
            _const_spec(w_out.shape), _const_spec(ln_g.shape), _const_spec(ln_b.shape),
        ],
        out_specs=pl.BlockSpec(memory_space=pl.ANY),
        out_shape=jax.ShapeDtypeStruct((n_batch, n_rows // n_batch, d_model), F32),
        scratch_shapes=[pltpu.VMEM((2, n_time, n_batch, d_model), F32),
                        pltpu.SemaphoreType.DMA((2,))],
        compiler_params=pltpu.CompilerParams(
            dimension_semantics=("arbitrary",), vmem_limit_bytes=V7X_VMEM_LIMIT),
        name="dense_ffn",
    )(x, w_in, w_in, w_out, ln_g, ln_b)


def _qkv_kernel(x_ref, wkva_ref, kvn_ref, wkvb_ref, wqa_ref, qn_ref, wqb_ref, cos_ref, sin_ref,
                q_ref, k_ref, v_ref, *, scale):
    xb = x_ref[0].astype(BF16)
    cosx = cos_ref[...]
    sinx = sin_ref[...]

    kva = jnp.dot(xb, wkva_ref[...], preferred_element_type=F32)
    ckv = kva[:, :V7X_LANES]
    ckv = ckv * lax.rsqrt(jnp.mean(ckv * ckv, axis=-1, keepdims=True) + RMS_EPS) * kvn_ref[...]
    k_rope = (kva[:, V7X_LANES:2 * V7X_LANES] * cosx
              + kva[:, 2 * V7X_LANES:3 * V7X_LANES] * sinx).astype(BF16)
    kv = jnp.dot(ckv.astype(BF16), wkvb_ref[...], preferred_element_type=F32)

    cq = jnp.dot(xb, wqa_ref[...], preferred_element_type=F32)
    cq = cq * lax.rsqrt(jnp.mean(cq * cq, axis=-1, keepdims=True) + RMS_EPS) * qn_ref[...]
    q = jnp.dot(cq.astype(BF16), wqb_ref[...], preferred_element_type=F32)

    for h in range(MLA_HEADS):
        kb = h * (QK_NOPE + V_DIM)
        k_ref[0, h, :, :QK_NOPE] = kv[:, kb:kb + QK_NOPE].astype(BF16)
        k_ref[0, h, :, QK_NOPE:] = k_rope
        v_ref[0, h] = kv[:, kb + QK_NOPE:kb + QK_NOPE + V_DIM].astype(BF16)
        qb = h * 3 * V7X_LANES
        q_ref[0, h, :, :QK_NOPE] = (q[:, qb:qb + QK_NOPE] * scale).astype(BF16)
        q_rope = (q[:, qb + V7X_LANES:qb + 2 * V7X_LANES] * cosx
                  + q[:, qb + 2 * V7X_LANES:qb + 3 * V7X_LANES] * sinx)
        q_ref[0, h, :, QK_NOPE:] = (q_rope * scale).astype(BF16)


def _qkv_proj(x, wkva, kvn, wkvb, wqa, qn, wqb, cosx, sinx, *, ts):
    n_batch, seq, d_model = x.shape
    scale = 1.0 / math.sqrt(QK_NOPE + QK_ROPE)
    dk = QK_NOPE + V7X_LANES
    kern = functools.partial(_qkv_kernel, scale=scale)
    return pl.pallas_call(
        kern,
        grid=(n_batch, seq // ts),
        in_specs=[
            pl.BlockSpec((1, ts, d_model), lambda b, s: (b, s, 0)),
            _const_spec(wkva.shape), _const_spec(kvn.shape), _const_spec(wkvb.shape),
            _const_spec(wqa.shape), _const_spec(qn.shape), _const_spec(wqb.shape),
            pl.BlockSpec((ts, V7X_LANES), lambda b, s: (s, 0)),
            pl.BlockSpec((ts, V7X_LANES), lambda b, s: (s, 0)),
        ],
        out_specs=[
            pl.BlockSpec((1, MLA_HEADS, ts, dk), lambda b, s: (b, 0, s, 0)),
            pl.BlockSpec((1, MLA_HEADS, ts, dk), lambda b, s: (b, 0, s, 0)),
            pl.BlockSpec((1, MLA_HEADS, ts, V_DIM), lambda b, s: (b, 0, s, 0)),
        ],
        out_shape=[
            jax.ShapeDtypeStruct((n_batch, MLA_HEADS, seq, dk), BF16),
            jax.ShapeDtypeStruct((n_batch, MLA_HEADS, seq, dk), BF16),
            jax.ShapeDtypeStruct((n_batch, MLA_HEADS, seq, V_DIM), BF16),
        ],
        compiler_params=pltpu.CompilerParams(
            dimension_semantics=("parallel", "parallel"), vmem_limit_bytes=V7X_VMEM_LIMIT),
        name="qkv_proj",
    )(x, wkva, kvn, wkvb, wqa, qn, wqb, cosx, sinx)


def _attn_kernel(q_ref, k_ref, v_ref, w_ref, o_ref, wb_ref, *, tq):
    wb_ref[...] = w_ref[...].astype(BF16)
    seq = q_ref.shape[2]
    n_tiles = seq // tq
    row = lax.broadcasted_iota(jnp.int32, (tq, tq), 0)
    col = lax.broadcasted_iota(jnp.int32, (tq, tq), 1)
    causal = col <= row
    for qi in range(n_tiles):
        q = q_ref[0, 0, qi * tq:(qi + 1) * tq, :]
        m = l = acc = None
        for kj in range(qi + 1):
            k = k_ref[0, 0, kj * tq:(kj + 1) * tq, :]
            v = v_ref[0, 0, kj * tq:(kj + 1) * tq, :]
            s = lax.dot_general(q, k, (((1,), (1,)), ((), ())), preferred_element_type=F32)
            if kj == qi:
                s = jnp.where(causal, s, -jnp.inf)
            m_new = jnp.max(s, axis=-1, keepdims=True)
            if kj > 0:
                m_new = jnp.maximum(m, m_new)
            p = jnp.exp(s - m_new)
            pv = jnp.dot(p.astype(BF16), v, preferred_element_type=F32)
            if kj == 0:
                l = jnp.sum(p, axis=-1, keepdims=True)
                acc = pv
            else:
                corr = jnp.exp(m - m_new)
                l = corr * l + jnp.sum(p, axis=-1, keepdims=True)
                acc = corr * acc + pv
            m = m_new
        o_ref[0, qi * tq:(qi + 1) * tq, :] = (acc / l).astype(o_ref.dtype)


def _cast_slab_rows(w2d, n_steps):
    rows = w2d.shape[0] // n_steps
    assert rows * n_steps == w2d.shape[0] and rows % (2 * V7X_SUBLANES) == 0
    return rows


def _attention(q, k, v, w_cast, *, tq):
    n_batch, n_heads, seq, dk = q.shape
    dv = v.shape[3]
    w_rows = _cast_slab_rows(w_cast, n_batch * n_heads)
    w_spec = pl.BlockSpec((w_rows, w_cast.shape[1]), lambda b, h: (b * n_heads + h, 0))
    kern = functools.partial(_attn_kernel, tq=tq)
    return pl.pallas_call(
        kern,
        grid=(n_batch, n_heads),
        in_specs=[
            pl.BlockSpec((1, 1, seq, dk), lambda b, h: (b, h, 0, 0)),
            pl.BlockSpec((1, 1, seq, dk), lambda b, h: (b, h, 0, 0)),
            pl.BlockSpec((1, 1, seq, dv), lambda b, h: (b, h, 0, 0)),
            w_spec,
        ],
        out_specs=[pl.BlockSpec((1, seq, dv), lambda b, h: (b, 0, h)), w_spec],
        out_shape=[jax.ShapeDtypeStruct((n_batch, seq, n_heads * dv), BF16),
                   jax.ShapeDtypeStruct(w_cast.shape, BF16)],
        compiler_params=pltpu.CompilerParams(
            dimension_semantics=("parallel", "parallel"), vmem_limit_bytes=V7X_VMEM_LIMIT),
        name="mla_attention",
    )(q, k, v, w_cast)


def _oproj_router_kernel(o_ref, x_ref, wo_ref, g_ref, b_ref, wrh_ref, wrl_ref, w_ref,
                         y_ref, info_ref, wb_ref):
    wb_ref[...] = w_ref[...].astype(BF16)
    half = info_ref.shape[0] // 2
    for r in range(2):
        rows = pl.ds(r * half, half)
        y, info = _oproj_router_rows(o_ref[0, rows, :], x_ref[0, rows, :], wo_ref, g_ref, b_ref,
                                     wrh_ref, wrl_ref)
        for c in range(V7X_SUBLANES):
            y_ref[pl.ds(r * half * V7X_SUBLANES + c, half, stride=V7X_SUBLANES), :] = (
                y[:, c * V7X_LANES:(c + 1) * V7X_LANES])
        info_ref[rows, :] = info


def _oproj_router_rows(o, x, wo_ref, g_ref, b_ref, wrh_ref, wrl_ref):
    mix = jnp.dot(o, wo_ref[...], preferred_element_type=F32)
    y = _layer_norm(ALPHA * x + mix, g_ref[...], b_ref[...])

    y_hi = y.astype(BF16)
    y_lo = (y - y_hi.astype(F32)).astype(BF16)
    logits = (jnp.dot(y_hi, wrh_ref[...], preferred_element_type=F32)
              + jnp.dot(y_lo, wrh_ref[...], preferred_element_type=F32)
              + jnp.dot(y_hi, wrl_ref[...], preferred_element_type=F32))
    lane = lax.broadcasted_iota(jnp.int32, logits.shape, 1)
    logits = jnp.where(lane < N_EXPERTS, logits, -jnp.inf)
    m1 = jnp.max(logits, axis=-1, keepdims=True)
    i1 = jnp.min(jnp.where(logits == m1, lane, V7X_LANES), axis=-1, keepdims=True)
    rest = jnp.where(lane == i1, -jnp.inf, logits)
    m2 = jnp.max(rest, axis=-1, keepdims=True)
    i2 = jnp.min(jnp.where(rest == m2, lane, V7X_LANES), axis=-1, keepdims=True)
    e2 = jnp.exp(m2 - m1)
    den = 1.0 + e2
    info = jnp.where(lane == 0, i1.astype(F32),
                     jnp.where(lane == 1, i2.astype(F32),
                               jnp.where(lane == 2, 1.0 / den,
                                         jnp.where(lane == 3, e2 / den, 0.0))))
    return y, info


def _oproj_router(o, x, wo, ln_g, ln_b, wr_hi, wr_lo, w_cast, *, ts):
    n_batch, seq, d_attn = o.shape
    d_model = x.shape[2]
    assert d_model == V7X_SUBLANES * V7X_LANES
    nst = seq // ts
    w_rows = _cast_slab_rows(w_cast, n_batch * nst)
    w_spec = pl.BlockSpec((w_rows, w_cast.shape[1]), lambda b, s: (b * nst + s, 0))
    return pl.pallas_call(
        _oproj_router_kernel,
        grid=(n_batch, nst),
        in_specs=[
            pl.BlockSpec((1, ts, d_attn), lambda b, s: (b, s, 0)),
            pl.BlockSpec((1, ts, d_model), lambda b, s: (b, s, 0)),
            _const_spec(wo.shape), _const_spec(ln_g.shape), _const_spec(ln_b.shape),
            _const_spec(wr_hi.shape), _const_spec(wr_lo.shape),
            w_spec,
        ],
        out_specs=[
            pl.BlockSpec((ts * V7X_SUBLANES, V7X_LANES), lambda b, s: (b * nst + s, 0)),
            pl.BlockSpec((ts, V7X_LANES), lambda b, s: (b * nst + s, 0)),
            w_spec,
        ],
        out_shape=[
            jax.ShapeDtypeStruct((n_batch * seq * V7X_SUBLANES, V7X_LANES), F32),
            jax.ShapeDtypeStruct((n_batch * seq, V7X_LANES), F32),
            jax.ShapeDtypeStruct(w_cast.shape, BF16),
        ],
        compiler_params=pltpu.CompilerParams(
            dimension_semantics=("parallel", "parallel"), vmem_limit_bytes=V7X_VMEM_LIMIT),
        name="oproj_router",
    )(o, x, wo, ln_g, ln_b, wr_hi, wr_lo, w_cast)


def _moe_kernel(te_ref, nu_ref, tok_ref, dst_ref, x_hbm, wg_ref, wu_ref, wo_ref, y_hbm,
                xs_ref, ob_ref, xb_ref, acc_ref, gsem, ssem, *, tm, nf):
    del te_ref
    s = pl.program_id(0)
    f = pl.program_id(1)
    gslot = s % 2
    cslot = 1 - gslot
    valid = (s >= 1) & (s <= nu_ref[0])
    sub, lanes = V7X_SUBLANES, V7X_LANES

    def wait_tile_rows(buf_ref, sem):
        slot_rows = tm * sub
        pltpu.make_async_copy(
            buf_ref.at[pl.ds(0, slot_rows)], buf_ref.at[pl.ds(slot_rows, slot_rows)], sem).wait()

    @pl.when((s == 0) & (f == 0))
    def _():
        ob_ref[...] = jnp.zeros_like(ob_ref)

    @pl.when((s >= 1) & (f == 0))
    def _():
        wait_tile_rows(xs_ref, gsem)

    for parity in range(2):
        @pl.when((f == 0) & (gslot == parity))
        def _(parity=parity):
            for row in range(tm):
                tok = tok_ref[0, 0, row]
                dst = dst_ref[0, 0, row]
                vmem_rows = pl.ds((parity * tm + row) * sub, sub)
                pltpu.make_async_copy(
                    x_hbm.at[tok], xs_ref.at[vmem_rows], gsem).start(priority=row % 2)
                pltpu.make_async_copy(
                    ob_ref.at[vmem_rows], y_hbm.at[dst], ssem).start(priority=row % 2)

    def gathered_rows_bf16():
        return jnp.concatenate(
            [xs_ref[pl.ds(cslot * tm * sub + c, tm, stride=sub), :].astype(BF16)
             for c in range(sub)], axis=-1)

    def store_result(res):
        for c in range(sub):
            ob_ref[pl.ds(cslot * tm * sub + c, tm, stride=sub), :] = (
                res[:, c * lanes:(c + 1) * lanes])

    for chunk in range(nf):
        @pl.when(valid & (f == chunk))
        def _(chunk=chunk):
            if chunk == 0:
                xb = gathered_rows_bf16()
                if nf > 1:
                    xb_ref[...] = xb
            else:
                xb = xb_ref[...]
            gate = jnp.dot(xb, wg_ref[0], preferred_element_type=F32)
            up = jnp.dot(xb, wu_ref[0], preferred_element_type=F32)
            h = (gate * jax.nn.sigmoid(gate) * up).astype(BF16)
            part = jnp.dot(h, wo_ref[0], preferred_element_type=F32)
            if nf == 1:
                store_result(part)
            elif chunk == 0:
                acc_ref[...] = part
            elif chunk < nf - 1:
                acc_ref[...] += part
            else:
                store_result(acc_ref[...] + part)

    @pl.when(f == nf - 1)
    def _():
        wait_tile_rows(ob_ref, ssem)

    @pl.when((s == pl.num_programs(0) - 1) & (f == nf - 1))
    def _():
        wait_tile_rows(xs_ref, gsem)


def _moe_experts(tile_expert, n_used, tok_idx, dst_idx, x_tiles, w_in, w_out, *, tm, fc):
    d_model = x_tiles.shape[1] * x_tiles.shape[2]
    edim = w_out.shape[1]
    nf = edim // fc
    n_tiles = tok_idx.shape[0]
    n_rows = n_tiles * tm
    n_steps = n_tiles + 2

    def expert_chunk(s, f, te, nu):
        tile = jnp.clip(s - 1, 0, nu[0] - 1)
        chunk = jnp.where(s > nu[0], nf - 1, jnp.where(s < 1, 0, f))
        return te[tile], chunk

    def w_in_map(half):
        def index(s, f, te, nu):
            e, chunk = expert_chunk(s, f, te, nu)
            return e, 0, chunk + half * nf
        return index

    def w_out_map(s, f, te, nu):
        e, chunk = expert_chunk(s, f, te, nu)
        return e, chunk, 0

    grid_spec = pltpu.PrefetchScalarGridSpec(
        num_scalar_prefetch=2,
        grid=(n_steps, nf),
        in_specs=[
            pl.BlockSpec((1, 1, tm), lambda s, f, te, nu: (jnp.minimum(s, n_tiles - 1), 0, 0),
                         memory_space=pltpu.SMEM),
            pl.BlockSpec((1, 1, tm), lambda s, f, te, nu: (s, 0, 0), memory_space=pltpu.SMEM),
            pl.BlockSpec(memory_space=pl.ANY),
            pl.BlockSpec((1, d_model, fc), w_in_map(0)),
            pl.BlockSpec((1, d_model, fc), w_in_map(1)),
            pl.BlockSpec((1, fc, d_model), w_out_map),
        ],
        out_specs=pl.BlockSpec(memory_space=pl.ANY),
        scratch_shapes=[
            pltpu.VMEM((2 * tm * V7X_SUBLANES, V7X_LANES), F32),
            pltpu.VMEM((2 * tm * V7X_SUBLANES, V7X_LANES), F32),
            pltpu.VMEM((tm, d_model), BF16), pltpu.VMEM((tm, d_model), F32),
            pltpu.SemaphoreType.DMA(()), pltpu.SemaphoreType.DMA(()),
        ],
    )
    warm = n_rows + jnp.arange(tm, dtype=jnp.int32)[None, :]
    dst_by_step = jnp.concatenate([warm, warm, dst_idx], axis=0)
    kern = functools.partial(_moe_kernel, tm=tm, nf=nf)
    return pl.pallas_call(
        kern,
        grid_spec=grid_spec,
        out_shape=jax.ShapeDtypeStruct((n_rows + tm, V7X_SUBLANES, V7X_LANES), F32),
        compiler_params=pltpu.CompilerParams(
            dimension_semantics=("arbitrary", "arbitrary"), vmem_limit_bytes=V7X_VMEM_LIMIT),
        name="moe_experts",
    )(tile_expert, n_used, tok_idx.reshape(n_tiles, 1, tm), dst_by_step.reshape(n_steps, 1, tm),
      x_tiles, w_in, w_in, w_out)


def _combine_kernel(x_ref, info_ref, y0_ref, y1_ref, g_ref, b_ref, o_ref):
    tc, d_model = o_ref.shape
    sub, lanes = V7X_SUBLANES, V7X_LANES
    info = info_ref[...]
    g0, g1 = info[:, 2:3], info[:, 3:4]
    pieces = []
    for c in range(sub):
        rows = pl.ds(c, tc, stride=sub)
        pieces.append(ALPHA * x_ref[rows, :] + (y0_ref[rows, :] * g0 + y1_ref[rows, :] * g1))
    mu = jnp.sum(sum(pieces), axis=-1, keepdims=True) / d_model
    cen = [p - mu for p in pieces]
    var = jnp.sum(sum(c * c for c in cen), axis=-1, keepdims=True) / d_model
    rstd = lax.rsqrt(var + LN_EPS)
    for c in range(sub):
        cols = slice(c * lanes, (c + 1) * lanes)
        o_ref[:, cols] = cen[c] * rstd * g_ref[:, cols] + b_ref[:, cols]


def _combine(x_tiles, info, y_tiles, ln_g, ln_b, *, tc):
    n_tok = info.shape[0]
    d_model = V7X_SUBLANES * V7X_LANES
    nt = n_tok // tc
    tile_block = (tc * V7X_SUBLANES, V7X_LANES)
    return pl.pallas_call(
        _combine_kernel,
        grid=(nt,),
        in_specs=[
            pl.BlockSpec(tile_block, lambda i: (i, 0)),
            pl.BlockSpec((tc, V7X_LANES), lambda i: (i, 0)),
            pl.BlockSpec(tile_block, lambda i: (i, 0)),
            pl.BlockSpec(tile_block, lambda i: (i + nt, 0)),
            _const_spec(ln_g.shape), _const_spec(ln_b.shape),
        ],
        out_specs=pl.BlockSpec((tc, d_model), lambda i: (i, 0)),
        out_shape=jax.ShapeDtypeStruct((n_tok, d_model), F32),
        compiler_params=pltpu.CompilerParams(
            dimension_semantics=("parallel",), vmem_limit_bytes=V7X_VMEM_LIMIT),
        name="moe_combine",
    )(x_tiles, info, y_tiles, y_tiles, ln_g, ln_b)


def _s5_params(lam_re, lam_im, log_step, b_re, b_im, c_re, c_im):
    n_groups, n_state = lam_re.shape
    dt = jnp.exp(log_step)[:, None]
    mag = jnp.exp(lam_re * dt)
    lb_re = mag * jnp.cos(lam_im * dt)
    lb_im = mag * jnp.sin(lam_im * dt)
    den = lam_re * lam_re + lam_im * lam_im
    f_re = ((lb_re - 1.0) * lam_re + lb_im * lam_im) / den
    f_im = (lb_im * lam_re - (lb_re - 1.0) * lam_im) / den
    bb_re = f_re[..., None] * b_re - f_im[..., None] * b_im
    bb_im = f_re[..., None] * b_im + f_im[..., None] * b_re
    gpb = S5_BLOCK_GROUPS
    n_blocks = n_groups // gpb
    eye = jnp.eye(gpb, dtype=F32)

    def in_blocks(w):
        w = w.transpose(0, 2, 1).reshape(n_blocks, gpb, S5_GROUP, n_state)
        return jnp.einsum('kgcp,gh->kgchp', w, eye).reshape(n_blocks, gpb * S5_GROUP, gpb * n_state)

    def out_blocks(w):
        w = w.reshape(n_blocks, gpb, S5_GROUP, n_state)
        return jnp.einsum('kgcp,gh->kgphc', w, eye).reshape(n_blocks, gpb * n_state, gpb * S5_GROUP)

    bb = jnp.concatenate([in_blocks(bb_re), in_blocks(bb_im)], axis=-1).astype(BF16)
    cc = jnp.concatenate([out_blocks(c_re), out_blocks(-c_im)], axis=1).astype(BF16)
    return bb, cc, lb_re.reshape(1, -1), lb_im.reshape(1, -1)


def _rope_lane_tables(seq):
    pos = jnp.arange(seq, dtype=F32)
    inv_freq = ROPE_THETA ** (-jnp.arange(0, QK_ROPE, 2, dtype=F32) / QK_ROPE)
    ang = pos[:, None] * inv_freq[None, :]
    cos, sin = jnp.cos(ang), jnp.sin(ang)
    pad = jnp.zeros((seq, V7X_LANES - QK_ROPE), F32)
    return (jnp.concatenate([cos, cos, pad], axis=-1),
            jnp.concatenate([-sin, sin, pad], axis=-1))


def _rope_weight_blocks(w_rope):
    half = QK_ROPE // 2
    t1, t2 = w_rope[:, :half], w_rope[:, half:]
    pad = jnp.zeros((w_rope.shape[0], V7X_LANES - QK_ROPE), w_rope.dtype)
    return jnp.concatenate([t1, t2, pad, t2, t1, pad], axis=-1)


def _routing(info, *, tm):
    n_tok = info.shape[0]
    n_assign = n_tok * TOP_K
    flat_e = info[:, :TOP_K].astype(jnp.int32).reshape(-1)
    experts = jnp.arange(N_EXPERTS, dtype=jnp.int32)
    counts = jnp.sum((flat_e[:, None] == experts[None, :]).astype(jnp.int32), axis=0)
    pad = (-counts) % tm
    filler = jnp.arange(N_EXPERTS * tm, dtype=jnp.int32)
    filler_key = jnp.where(filler % tm < pad[filler // tm], filler // tm, N_EXPERTS)
    keys = jnp.concatenate([flat_e, filler_key])
    ids = jnp.arange(keys.shape[0], dtype=jnp.int32)
    sorted_keys, order = lax.sort((keys, ids), num_keys=1, is_stable=True)
    is_real = order < n_assign
    tok_idx = jnp.where(is_real, order // TOP_K, 0)
    dst_idx = jnp.where(is_real, (order % TOP_K) * n_tok + order // TOP_K, order)
    tile_key = sorted_keys[::tm]
    tile_expert = jnp.minimum(tile_key, N_EXPERTS - 1)
    n_used = jnp.sum((tile_key < N_EXPERTS).astype(jnp.int32)).reshape(1)
    return tok_idx.reshape(-1, tm), dst_idx.reshape(-1, tm), tile_expert, n_used


def kernel(x, s5_lam_re, s5_lam_im, s5_log_step, s5_b_re, s5_b_im, s5_c_re, s5_c_im, s5_d, s5_w_glu,
           mla_q_w_a, mla_q_norm, mla_q_w_b, mla_o_w, kv_w_a, kv_norm, kv_w_b, ffn_w_in, ffn_w_out,
           moe_router, moe_w_in, moe_w_out, ln_g, ln_b):
    n_batch, seq, d_model = x.shape
    n_tok = n_batch * seq
    row = lambda v: v.reshape(1, -1).astype(F32)

    bb, cc, a_re, a_im = _s5_params(s5_lam_re[0], s5_lam_im[0], s5_log_step[0], s5_b_re[0], s5_b_im[0],
                                    s5_c_re[0], s5_c_im[0])
    n_time = min(32, seq)
    x1 = _s5_layer(x, bb, cc, a_re, a_im, row(s5_d[0]), s5_w_glu[0].astype(BF16),
                   row(ln_g[0, 0]), row(ln_b[0, 0]), n_time=n_time)
    x2 = _dense_ffn(x1, ffn_w_in[0].astype(BF16), ffn_w_out[0].astype(BF16),
                    row(ln_g[0, 1]), row(ln_b[0, 1]), n_batch=n_batch, n_time=n_time,
                    fc=ffn_w_out.shape[1] // 2)

    kv_lora = kv_norm.shape[0]
    wkva = jnp.concatenate([kv_w_a[:, :kv_lora], _rope_weight_blocks(kv_w_a[:, kv_lora:])],
                           axis=-1).astype(BF16)
    wq = mla_q_w_b[0].reshape(-1, MLA_HEADS, QK_NOPE + QK_ROPE)
    wqb = jnp.concatenate(
        [jnp.concatenate([wq[:, h, :QK_NOPE], _rope_weight_blocks(wq[:, h, QK_NOPE:])], axis=-1)
         for h in range(MLA_HEADS)], axis=-1).astype(BF16)
    cosx, sinx = _rope_lane_tables(seq)
    ts = min(512, seq)
    q, k, v = _qkv_proj(x2, wkva, row(kv_norm), kv_w_b.astype(BF16), mla_q_w_a[0].astype(BF16),
                        row(mla_q_norm[0]), wqb, cosx, sinx, ts=ts)
    w_in, w_out = moe_w_in[0], moe_w_out[0]
    o, w_in_bf = _attention(q, k, v, w_in.reshape(-1, w_in.shape[2]), tq=min(512, seq))
    wr = jnp.pad(moe_router[0], ((0, 0), (0, V7X_LANES - N_EXPERTS)))
    wr_hi = wr.astype(BF16)
    wr_lo = (wr - wr_hi.astype(F32)).astype(BF16)
    x3, info, w_out_bf = _oproj_router(o, x2, mla_o_w[0].astype(BF16), row(ln_g[1, 0]),
                                       row(ln_b[1, 0]), wr_hi, wr_lo,
                                       w_out.reshape(-1, w_out.shape[2]), ts=ts)

    tok_idx, dst_idx, tile_expert, n_used = _routing(info, tm=512)
    ys = _moe_experts(tile_expert, n_used, tok_idx, dst_idx,
                      x3.reshape(n_tok, V7X_SUBLANES, V7X_LANES), w_in_bf.reshape(w_in.shape),
                      w_out_bf.reshape(w_out.shape), tm=512, fc=w_out.shape[1] // 2)
    out = _combine(x3, info, ys.reshape(-1, V7X_LANES), row(ln_g[1, 1]), row(ln_b[1, 1]),
                   tc=min(512, n_tok))
    return out.reshape(n_batch, seq, d_model)
```

```python
import functools
import math

import jax
import jax.numpy as jnp
from jax import lax
from jax.experimental import pallas as pl
from jax.experimental.pallas import tpu as pltpu

F32 = jnp.float32
BF16 = jnp.bfloat16

V7X_LANES = 128
V7X_SUBLANES = 8
V7X_VMEM_LIMIT = 56 * 1024 * 1024

DEPTH = 2
ALPHA = (2.0 * DEPTH) ** 0.25
LN_EPS = 1e-5
RMS_EPS = 1e-6
ROPE_THETA = 10000.0

S5_GROUP = 16
S5_STATE = 64
S5_BLOCK_GROUPS = 16

MLA_HEADS = 8
QK_NOPE = 128
QK_ROPE = 64
V_DIM = 128
N_EXPERTS = 8
TOP_K = 2


def _const_spec(shape):
    zeros = (0,) * len(shape)
    return pl.BlockSpec(shape, lambda *_: zeros, pipeline_mode=pl.Buffered(1))


def _layer_norm(h, g, b):
    mu = jnp.mean(h, axis=-1, keepdims=True)
    c = h - mu
    var = jnp.mean(c * c, axis=-1, keepdims=True)
    return c * lax.rsqrt(var + LN_EPS) * g + b


def _gelu_tanh(y):
    return 0.5 * y * (1.0 + jnp.tanh(math.sqrt(2.0 / math.pi) * (y + 0.044715 * (y * y * y))))


def _s5_kernel(x_hbm, bb_ref, cc_ref, are_ref, aim_ref, d_ref, wglu_ref, g_ref, b_ref,
               o_ref, xin_ref, bu_ref, st_ref, sem, *, n_batch, n_time, n_blocks):
    half = bb_ref.shape[2] // 2
    cb = bb_ref.shape[1]
    slab = 512
    d_model = o_ref.shape[1]
    i = pl.program_id(0)
    slot = i % 2

    def chunk_copy(step, b, to_slot):
        return pltpu.make_async_copy(
            x_hbm.at[b, pl.ds(step * n_time, n_time), :], xin_ref.at[to_slot, :, b, :],
            sem.at[to_slot])

    @pl.when(i == 0)
    def _():
        st_ref[...] = jnp.zeros_like(st_ref)
        for b in range(n_batch):
            chunk_copy(0, b, 0).start()

    @pl.when(i + 1 < pl.num_programs(0))
    def _():
        for b in range(n_batch):
            chunk_copy(i + 1, b, 1 - slot).start()

    for b in range(n_batch):
        chunk_copy(i, b, slot).wait()

    x = xin_ref[slot].reshape(n_time * n_batch, d_model)
    xb = x.astype(BF16)
    for k in range(n_blocks):
        bu_ref[:, k * 2 * half:(k + 1) * 2 * half] = jnp.dot(
            xb[:, k * cb:(k + 1) * cb], bb_ref[k], preferred_element_type=F32)

    for k in range(n_blocks):
        for j in range(half // slab):
            re0 = k * 2 * half + j * slab
            im0 = re0 + half
            a0 = k * half + j * slab
            for bh in range(n_batch // V7X_SUBLANES):
                r0 = bh * V7X_SUBLANES
                ar = jnp.broadcast_to(are_ref[:, a0:a0 + slab], (V7X_SUBLANES, slab))
                ai = jnp.broadcast_to(aim_ref[:, a0:a0 + slab], (V7X_SUBLANES, slab))
                s_re = st_ref[r0:r0 + V7X_SUBLANES, re0:re0 + slab]
                s_im = st_ref[r0:r0 + V7X_SUBLANES, im0:im0 + slab]

                for t in range(n_time):
                    rows = pl.ds(t * n_batch + r0, V7X_SUBLANES)
                    b_re = bu_ref[rows, re0:re0 + slab]
                    b_im = bu_ref[rows, im0:im0 + slab]
                    s_re, s_im = (ar * s_re - ai * s_im + b_re, ar * s_im + ai * s_re + b_im)
                    bu_ref[rows, re0:re0 + slab] = s_re
                    bu_ref[rows, im0:im0 + slab] = s_im
                st_ref[r0:r0 + V7X_SUBLANES, re0:re0 + slab] = s_re
                st_ref[r0:r0 + V7X_SUBLANES, im0:im0 + slab] = s_im

    ys = []
    for k in range(n_blocks):
        s_blk = bu_ref[:, k * 2 * half:(k + 1) * 2 * half].astype(BF16)
        ys.append(jnp.dot(s_blk, cc_ref[k], preferred_element_type=F32))
    y = jnp.concatenate(ys, axis=-1) + d_ref[...] * x
    act = _gelu_tanh(y).astype(BF16)
    z = jnp.dot(act, wglu_ref[...], preferred_element_type=F32)
    mix = z[:, :d_model] * jax.nn.sigmoid(z[:, d_model:])
    o_ref[...] = _layer_norm(ALPHA * x + mix, g_ref[...], b_ref[...])


def _s5_layer(x, bb, cc, a_re, a_im, d_skip, w_glu, ln_g, ln_b, *, n_time):
    n_batch, seq, d_model = x.shape
    n_rows = n_batch * seq
    n_blocks = bb.shape[0]
    rows = n_time * n_batch
    state_w = n_blocks * bb.shape[2]
    kern = functools.partial(_s5_kernel, n_batch=n_batch, n_time=n_time, n_blocks=n_blocks)
    return pl.pallas_call(
        kern,
        grid=(n_rows // rows,),
        in_specs=[
            pl.BlockSpec(memory_space=pl.ANY),
            _const_spec(bb.shape), _const_spec(cc.shape),
            _const_spec(a_re.shape), _const_spec(a_im.shape), _const_spec(d_skip.shape),
            _const_spec(w_glu.shape), _const_spec(ln_g.shape), _const_spec(ln_b.shape),
        ],
        out_specs=pl.BlockSpec((rows, d_model), lambda i: (i, 0)),
        out_shape=jax.ShapeDtypeStruct((n_rows, d_model), F32),
        scratch_shapes=[
            pltpu.VMEM((2, n_time, n_batch, d_model), F32),
            pltpu.VMEM((rows, state_w), F32), pltpu.VMEM((n_batch, state_w), F32),
            pltpu.SemaphoreType.DMA((2,)),
        ],
        compiler_params=pltpu.CompilerParams(
            dimension_semantics=("arbitrary",), vmem_limit_bytes=V7X_VMEM_LIMIT),
        name="s5_mixer",
    )(x, bb, cc, a_re, a_im, d_skip, w_glu, ln_g, ln_b)


def _ffn_kernel(x_ref, wg_ref, wu_ref, wo_ref, g_ref, b_ref, o_hbm, obuf_ref, sem, *, fc, n_batch):
    i = pl.program_id(0)
    last = pl.num_programs(0) - 1
    slot = i % 2
    n_time = x_ref.shape[0] // n_batch

    def out_copy(step, b, from_slot):
        return pltpu.make_async_copy(
            obuf_ref.at[from_slot, :, b, :], o_hbm.at[b, pl.ds(step * n_time, n_time), :],
            sem.at[from_slot])

    @pl.when(i >= 2)
    def _():
        for b in range(n_batch):
            out_copy(i - 2, b, slot).wait()

    x = x_ref[...]
    xb = x.astype(BF16)
    acc = None
    for j in range(wo_ref.shape[0] // fc):
        gate = jnp.dot(xb, wg_ref[:, j * fc:(j + 1) * fc], preferred_element_type=F32)
        up = jnp.dot(xb, wu_ref[:, j * fc:(j + 1) * fc], preferred_element_type=F32)
        h = (gate * jax.nn.sigmoid(gate) * up).astype(BF16)
        part = jnp.dot(h, wo_ref[j * fc:(j + 1) * fc, :], preferred_element_type=F32)
        acc = part if acc is None else acc + part
    out = _layer_norm(ALPHA * x + acc, g_ref[...], b_ref[...])
    obuf_ref[slot] = out.reshape(n_time, n_batch, out.shape[1])
    for b in range(n_batch):
        out_copy(i, b, slot).start()

    @pl.when((i == last) & (i >= 1))
    def _():
        for b in range(n_batch):
            out_copy(i - 1, b, 1 - slot).wait()

    @pl.when(i == last)
    def _():
        for b in range(n_batch):
            out_copy(i, b, slot).wait()


def _dense_ffn(x, w_in, w_out, ln_g, ln_b, *, n_batch, n_time, fc):
    n_rows, d_model = x.shape
    ffn = w_out.shape[0]
    tm = n_time * n_batch
    kern = functools.partial(_ffn_kernel, fc=fc, n_batch=n_batch)
    return pl.pallas_call(
        kern,
        grid=(n_rows // tm,),
        in_specs=[
            pl.BlockSpec((tm, d_model), lambda i: (i, 0)),
            pl.BlockSpec((d_model, ffn), lambda i: (0, 0), pipeline_mode=pl.Buffered(1)),
            pl.BlockSpec((d_model, ffn), lambda i: (0, 1), pipeline_mode=pl.Buffered(1)),
            _const_spec(w_out.shape), _const_spec(ln_g.shape), _const_spec(ln_b.shape),
        ],
        out_specs=pl.BlockSpec(memory_space=pl.ANY),
        out_shape=jax.ShapeDtypeStruct((n_batch, n_rows // n_batch, d_model), F32),
        scratch_shapes=[pltpu.VMEM((2, n_time, n_batch, d_model), F32),
                        pltpu.SemaphoreType.DMA((2,))],
        compiler_params=pltpu.CompilerParams(
            dimension_semantics=("arbitrary",), vmem_limit_bytes=V7X_VMEM_LIMIT),
        name="dense_ffn",
    )(x, w_in, w_in, w_out, ln_g, ln_b)


def _qkv_kernel(x_ref, wkva_ref, kvn_ref, wkvb_ref, wqa_ref, qn_ref, wqb_ref, cos_ref, sin_ref,
                q_ref, k_ref, v_ref, *, scale):
    xb = x_ref[0].astype(BF16)
    cosx = cos_ref[...]
    sinx = sin_ref[...]

    kva = jnp.dot(xb, wkva_ref[...], preferred_element_type=F32)
    ckv = kva[:, :V7X_LANES]
    ckv = ckv * lax.rsqrt(jnp.mean(ckv * ckv, axis=-1, keepdims=True) + RMS_EPS) * kvn_ref[...]
    k_rope = (kva[:, V7X_LANES:2 * V7X_LANES] * cosx
              + kva[:, 2 * V7X_LANES:3 * V7X_LANES] * sinx).astype(BF16)
    kv = jnp.dot(ckv.astype(BF16), wkvb_ref[...], preferred_element_type=F32)

    cq = jnp.dot(xb, wqa_ref[...], preferred_element_type=F32)
    cq = cq * lax.rsqrt(jnp.mean(cq * cq, axis=-1, keepdims=True) + RMS_EPS) * qn_ref[...]
    q = jnp.dot(cq.astype(BF16), wqb_ref[...], preferred_element_type=F32)

    for h in range(MLA_HEADS):
        kb = h * (QK_NOPE + V_DIM)
        k_ref[0, h, :, :QK_NOPE] = kv[:, kb:kb + QK_NOPE].astype(BF16)
        k_ref[0, h, :, QK_NOPE:] = k_rope
        v_ref[0, h] = kv[:, kb + QK_NOPE:kb + QK_NOPE + V_DIM].astype(BF16)
        qb = h * 3 * V7X_LANES
        q_ref[0, h, :, :QK_NOPE] = (q[:, qb:qb + QK_NOPE] * scale).astype(BF16)
        q_rope = (q[:, qb + V7X_LANES:qb + 2 * V7X_LANES] * cosx
                  + q[:, qb + 2 * V7X_LANES:qb + 3 * V7X_LANES] * sinx)
        q_ref[0, h, :, QK_NOPE:] = (q_rope * scale).astype(BF16)


def _qkv_proj(x, wkva, kvn, wkvb, wqa, qn, wqb, cosx, sinx, *, ts):
    n_batch, seq, d_model = x.shape
    scale = 1.0 / math.sqrt(QK_NOPE + QK_ROPE)
    dk = QK_NOPE + V7X_LANES
    kern = functools.partial(_qkv_kernel, scale=scale)
    return pl.pallas_call(
        kern,
        grid=(n_batch, seq // ts),
        in_specs=[
            pl.BlockSpec((1, ts, d_model), lambda b, s: (b, s, 0)),
            _const_spec(wkva.shape), _const_spec(kvn.shape), _const_spec(wkvb.shape),
            _const_spec(wqa.shape), _const_spec(qn.shape), _const_spec(wqb.shape),
            pl.BlockSpec((ts, V7X_LANES), lambda b, s: (s, 0)),
            pl.BlockSpec((ts, V7X_LANES), lambda b, s: (s, 0)),
        ],
        out_specs=[
            pl.BlockSpec((1, MLA_HEADS, ts, dk), lambda b, s: (b, 0, s, 0)),
            pl.BlockSpec((1, MLA_HEADS, ts, dk), lambda b, s: (b, 0, s, 0)),
            pl.BlockSpec((1, MLA_HEADS, ts, V_DIM), lambda b, s: (b, 0, s, 0)),
        ],
        out_shape=[
            jax.ShapeDtypeStruct((n_batch, MLA_HEADS, seq, dk), BF16),
            jax.ShapeDtypeStruct((n_batch, MLA_HEADS, seq, dk), BF16),
            jax.ShapeDtypeStruct((n_batch, MLA_HEADS, seq, V_DIM), BF16),
        ],
        compiler_params=pltpu.CompilerParams(
            dimension_semantics=("parallel", "parallel"), vmem_limit_bytes=V7X_VMEM_LIMIT),
        name="qkv_proj",
    )(x, wkva, kvn, wkvb, wqa, qn, wqb, cosx, sinx)


def _attn_kernel(q_ref, k_ref, v_ref, w_ref, o_ref, wb_ref, *, tq):
    wb_ref[...] = w_ref[...].astype(BF16)
    seq = q_ref.shape[2]
    n_tiles = seq // tq
    row = lax.broadcasted_iota(jnp.int32, (tq, tq), 0)
    col = lax.broadcasted_iota(jnp.int32, (tq, tq), 1)
    causal = col <= row
    for qi in range(n_tiles):
        q = q_ref[0, 0, qi * tq:(qi + 1) * tq, :]
        m = l = acc = None
        for kj in range(qi + 1):
            k = k_ref[0, 0, kj * tq:(kj + 1) * tq, :]
            v = v_ref[0, 0, kj * tq:(kj + 1) * tq, :]
            s = lax.dot_general(q, k, (((1,), (1,)), ((), ())), preferred_element_type=F32)
            if kj == qi:
                s = jnp.where(causal, s, -jnp.inf)
            m_new = jnp.max(s, axis=-1, keepdims=True)
            if kj > 0:
                m_new = jnp.maximum(m, m_new)
            p = jnp.exp(s - m_new)
            pv = jnp.dot(p.astype(BF16), v, preferred_element_type=F32)
            if kj == 0:
                l = jnp.sum(p, axis=-1, keepdims=True)
                acc = pv
            else:
                corr = jnp.exp(m - m_new)
                l = corr * l + jnp.sum(p, axis=-1, keepdims=True)
                acc = corr * acc + pv
            m = m_new
        o_ref[0, qi * tq:(qi + 1) * tq, :] = (acc / l).astype(o_ref.dtype)


def _cast_slab_rows(w2d, n_steps):
    rows = w2d.shape[0] // n_steps
    assert rows * n_steps == w2d.shape[0] and rows % (2 * V7X_SUBLANES) == 0
    return rows


def _attention(q, k, v, w_cast, *, tq):
    n_batch, n_heads, seq, dk = q.shape
    dv = v.shape[3]
    w_rows = _cast_slab_rows(w_cast, n_batch * n_heads)
    w_spec = pl.BlockSpec((w_rows, w_cast.shape[1]), lambda b, h: (b * n_heads + h, 0))
    kern = functools.partial(_attn_kernel, tq=tq)
    return pl.pallas_call(
        kern,
        grid=(n_batch, n_heads),
        in_specs=[
            pl.BlockSpec((1, 1, seq, dk), lambda b, h: (b, h, 0, 0)),
            pl.BlockSpec((1, 1, seq, dk), lambda b, h: (b, h, 0, 0)),
            pl.BlockSpec((1, 1, seq, dv), lambda b, h: (b, h, 0, 0)),
            w_spec,
        ],
        out_specs=[pl.BlockSpec((1, seq, dv), lambda b, h: (b, 0, h)), w_spec],
        out_shape=[jax.ShapeDtypeStruct((n_batch, seq, n_heads * dv), BF16),
                   jax.ShapeDtypeStruct(w_cast.shape, BF16)],
        compiler_params=pltpu.CompilerParams(
            dimension_semantics=("parallel", "parallel"), vmem_limit_bytes=V7X_VMEM_LIMIT),
        name="mla_attention",
    )(q, k, v, w_cast)


def _oproj_router_kernel(o_ref, x_ref, wo_ref, g_ref, b_ref, wrh_ref, wrl_ref, w_ref,
                         y_ref, info_ref, wb_ref):
    wb_ref[...] = w_ref[...].astype(BF16)
    half = info_ref.shape[0] // 2
    for r in range(2):
        rows = pl.ds(r * half, half)
        y, info = _oproj_router_rows(o_ref[0, rows, :], x_ref[0, rows, :], wo_ref, g_ref, b_ref,
                                     wrh_ref, wrl_ref)
        for c in range(V7X_SUBLANES):
            y_ref[pl.ds(r * half * V7X_SUBLANES + c, half, stride=V7X_SUBLANES), :] = (
                y[:, c * V7X_LANES:(c + 1) * V7X_LANES])
        info_ref[rows, :] = info


def _oproj_router_rows(o, x, wo_ref, g_ref, b_ref, wrh_ref, wrl_ref):
    mix = jnp.dot(o, wo_ref[...], preferred_element_type=F32)
    y = _layer_norm(ALPHA * x + mix, g_ref[...], b_ref[...])

    y_hi = y.astype(BF16)
    y_lo = (y - y_hi.astype(F32)).astype(BF16)
    logits = (jnp.dot(y_hi, wrh_ref[...], preferred_element_type=F32)
              + jnp.dot(y_lo, wrh_ref[...], preferred_element_type=F32)
              + jnp.dot(y_hi, wrl_ref[...], preferred_element_type=F32))
    lane = lax.broadcasted_iota(jnp.int32, logits.shape, 1)
    logits = jnp.where(lane < N_EXPERTS, logits, -jnp.inf)
    m1 = jnp.max(logits, axis=-1, keepdims=True)
    i1 = jnp.min(jnp.where(logits == m1, lane, V7X_LANES), axis=-1, keepdims=True)
    rest = jnp.where(lane == i1, -jnp.inf, logits)
    m2 = jnp.max(rest, axis=-1, keepdims=True)
    i2 = jnp.min(jnp.where(rest == m2, lane, V7X_LANES), axis=-1, keepdims=True)
    e2 = jnp.exp(m2 - m1)
    den = 1.0 + e2
    info = jnp.where(lane == 0, i1.astype(F32),
                     jnp.where(lane == 1, i2.astype(F32),
                               jnp.where(lane == 2, 1.0 / den,
                                         jnp.where(lane == 3, e2 / den, 0.0))))
    return y, info


def _oproj_router(o, x, wo, ln_g, ln_b, wr_hi, wr_lo, w_cast, *, ts):
    n_batch, seq, d_attn = o.shape
    d_model = x.shape[2]
    assert d_model == V7X_SUBLANES * V7X_LANES
    nst = seq // ts
    w_rows = _cast_slab_rows(w_cast, n_batch * nst)
    w_spec = pl.BlockSpec((w_rows, w_cast.shape[1]), lambda b, s: (b * nst + s, 0))
    return pl.pallas_call(
        _oproj_router_kernel,
        grid=(n_batch, nst),
        in_specs=[
            pl.BlockSpec((1, ts, d_attn), lambda b, s: (b, s, 0)),
            pl.BlockSpec((1, ts, d_model), lambda b, s: (b, s, 0)),
            _const_spec(wo.shape), _const_spec(ln_g.shape), _const_spec(ln_b.shape),
            _const_spec(wr_hi.shape), _const_spec(wr_lo.shape),
            w_spec,
        ],
        out_specs=[
            pl.BlockSpec((ts * V7X_SUBLANES, V7X_LANES), lambda b, s: (b * nst + s, 0)),
            pl.BlockSpec((ts, V7X_LANES), lambda b, s: (b * nst + s, 0)),
            w_spec,
        ],
        out_shape=[
            jax.ShapeDtypeStruct((n_batch * seq * V7X_SUBLANES, V7X_LANES), F32),
            jax.ShapeDtypeStruct((n_batch * seq, V7X_LANES), F32),
            jax.ShapeDtypeStruct(w_cast.shape, BF16),
        ],
        compiler_params=pltpu.CompilerParams(
            dimension_semantics=("parallel", "parallel"), vmem_limit_bytes=V7X_VMEM_LIMIT),
        name="oproj_router",
    )(o, x, wo, ln_g, ln_b, wr_hi, wr_lo, w_cast)


def _moe_kernel(te_ref, nu_ref, tok_ref, dst_ref, x_hbm, wg_ref, wu_ref, wo_ref, y_hbm,
                xs_ref, ob_ref, xb_ref, acc_ref, gsem, ssem, *, tm, nf):
    del te_ref
    s = pl.program_id(0)
    f = pl.program_id(1)
    gslot = s % 2
    cslot = 1 - gslot
    valid = (s >= 1) & (s <= nu_ref[0])
    sub, lanes = V7X_SUBLANES, V7X_LANES

    def wait_tile_rows(buf_ref, sem):
        slot_rows = tm * sub
        pltpu.make_async_copy(
            buf_ref.at[pl.ds(0, slot_rows)], buf_ref.at[pl.ds(slot_rows, slot_rows)], sem).wait()

    @pl.when((s == 0) & (f == 0))
    def _():
        ob_ref[...] = jnp.zeros_like(ob_ref)

    @pl.when((s >= 1) & (f == 0))
    def _():
        wait_tile_rows(xs_ref, gsem)

    for parity in range(2):
        @pl.when((f == 0) & (gslot == parity))
        def _(parity=parity):
            for row in range(tm):
                tok = tok_ref[0, 0, row]
                dst = dst_ref[0, 0, row]
                vmem_rows = pl.ds((parity * tm + row) * sub, sub)
                pltpu.make_async_copy(
                    x_hbm.at[tok], xs_ref.at[vmem_rows], gsem).start(priority=1)
                pltpu.make_async_copy(
                    ob_ref.at[vmem_rows], y_hbm.at[dst], ssem).start(priority=1)

    def gathered_rows_bf16():
        return jnp.concatenate(
            [xs_ref[pl.ds(cslot * tm * sub + c, tm, stride=sub), :].astype(BF16)
             for c in range(sub)], axis=-1)

    def store_result(res):
        for c in range(sub):
            ob_ref[pl.ds(cslot * tm * sub + c, tm, stride=sub), :] = (
                res[:, c * lanes:(c + 1) * lanes])

    for chunk in range(nf):
        @pl.when(valid & (f == chunk))
        def _(chunk=chunk):
            if chunk == 0:
                xb = gathered_rows_bf16()
                if nf > 1:
                    xb_ref[...] = xb
            else:
                xb = xb_ref[...]
            gate = jnp.dot(xb, wg_ref[0], preferred_element_type=F32)
            up = jnp.dot(xb, wu_ref[0], preferred_element_type=F32)
            h = (gate * jax.nn.sigmoid(gate) * up).astype(BF16)
            part = jnp.dot(h, wo_ref[0], preferred_element_type=F32)
            if nf == 1:
                store_result(part)
            elif chunk == 0:
                acc_ref[...] = part
            elif chunk < nf - 1:
                acc_ref[...] += part
            else:
                store_result(acc_ref[...] + part)

    @pl.when(f == nf - 1)
    def _():
        wait_tile_rows(ob_ref, ssem)

    @pl.when((s == pl.num_programs(0) - 1) & (f == nf - 1))
    def _():
        wait_tile_rows(xs_ref, gsem)


def _moe_experts(tile_expert, n_used, tok_idx, dst_idx, x_tiles, w_in, w_out, *, tm, fc):
    d_model = x_tiles.shape[1] * x_tiles.shape[2]
    edim = w_out.shape[1]
    nf = edim // fc
    n_tiles = tok_idx.shape[0]
    n_rows = n_tiles * tm
    n_steps = n_tiles + 2

    def expert_chunk(s, f, te, nu):
        tile = jnp.clip(s - 1, 0, nu[0] - 1)
        chunk = jnp.where(s > nu[0], nf - 1, jnp.where(s < 1, 0, f))
        return te[tile], chunk

    def w_in_map(half):
        def index(s, f, te, nu):
            e, chunk = expert_chunk(s, f, te, nu)
            return e, 0, chunk + half * nf
        return index

    def w_out_map(s, f, te, nu):
        e, chunk = expert_chunk(s, f, te, nu)
        return e, chunk, 0

    grid_spec = pltpu.PrefetchScalarGridSpec(
        num_scalar_prefetch=2,
        grid=(n_steps, nf),
        in_specs=[
            pl.BlockSpec((1, 1, tm), lambda s, f, te, nu: (jnp.minimum(s, n_tiles - 1), 0, 0),
                         memory_space=pltpu.SMEM),
            pl.BlockSpec((1, 1, tm), lambda s, f, te, nu: (s, 0, 0), memory_space=pltpu.SMEM),
            pl.BlockSpec(memory_space=pl.ANY),
            pl.BlockSpec((1, d_model, fc), w_in_map(0)),
            pl.BlockSpec((1, d_model, fc), w_in_map(1)),
            pl.BlockSpec((1, fc, d_model), w_out_map),
        ],
        out_specs=pl.BlockSpec(memory_space=pl.ANY),
        scratch_shapes=[
            pltpu.VMEM((2 * tm * V7X_SUBLANES, V7X_LANES), F32),
            pltpu.VMEM((2 * tm * V7X_SUBLANES, V7X_LANES), F32),
            pltpu.VMEM((tm, d_model), BF16), pltpu.VMEM((tm, d_model), F32),
            pltpu.SemaphoreType.DMA(()), pltpu.SemaphoreType.DMA(()),
        ],
    )
    warm = n_rows + jnp.arange(tm, dtype=jnp.int32)[None, :]
    dst_by_step = jnp.concatenate([warm, warm, dst_idx], axis=0)
    kern = functools.partial(_moe_kernel, tm=tm, nf=nf)
    return pl.pallas_call(
        kern,
        grid_spec=grid_spec,
        out_shape=jax.ShapeDtypeStruct((n_rows + tm, V7X_SUBLANES, V7X_LANES), F32),
        compiler_params=pltpu.CompilerParams(
            dimension_semantics=("arbitrary", "arbitrary"), vmem_limit_bytes=V7X_VMEM_LIMIT),
        name="moe_experts",
    )(tile_expert, n_used, tok_idx.reshape(n_tiles, 1, tm), dst_by_step.reshape(n_steps, 1, tm),
      x_tiles, w_in, w_in, w_out)


def _combine_kernel(x_ref, info_ref, y0_ref, y1_ref, g_ref, b_ref, o_ref):
    tc, d_model = o_ref.shape
    sub, lanes = V7X_SUBLANES, V7X_LANES
    info = info_ref[...]
    g0, g1 = info[:, 2:3], info[:, 3:4]
    pieces = []
    for c in range(sub):
        rows = pl.ds(c, tc, stride=sub)
        pieces.append(ALPHA * x_ref[rows, :] + (y0_ref[rows, :] * g0 + y1_ref[rows, :] * g1))
    mu = jnp.sum(sum(pieces), axis=-1, keepdims=True) / d_model
    cen = [p - mu for p in pieces]
    var = jnp.sum(sum(c * c for c in cen), axis=-1, keepdims=True) / d_model
    rstd = lax.rsqrt(var + LN_EPS)
    for c in range(sub):
        cols = slice(c * lanes, (c + 1) * lanes)
        o_ref[:, cols] = cen[c] * rstd * g_ref[:, cols] + b_ref[:, cols]


def _combine(x_tiles, info, y_tiles, ln_g, ln_b, *, tc):
    n_tok = info.shape[0]
    d_model = V7X_SUBLANES * V7X_LANES
    nt = n_tok // tc
    tile_block = (tc * V7X_SUBLANES, V7X_LANES)
    return pl.pallas_call(
        _combine_kernel,
        grid=(nt,),
        in_specs=[
            pl.BlockSpec(tile_block, lambda i: (i, 0)),
            pl.BlockSpec((tc, V7X_LANES), lambda i: (i, 0)),
            pl.BlockSpec(tile_block, lambda i: (i, 0)),
            pl.BlockSpec(tile_block, lambda i: (i + nt, 0)),
            _const_spec(ln_g.shape), _const_spec(ln_b.shape),
        ],
        out_specs=pl.BlockSpec((tc, d_model), lambda i: (i, 0)),
        out_shape=jax.ShapeDtypeStruct((n_tok, d_model), F32),
        compiler_params=pltpu.CompilerParams(
            dimension_semantics=("parallel",), vmem_limit_bytes=V7X_VMEM_LIMIT),
        name="moe_combine",
    )(x_tiles, info, y_tiles, y_tiles, ln_g, ln_b)


def _s5_params(lam_re, lam_im, log_step, b_re, b_im, c_re, c_im):
    n_groups, n_state = lam_re.shape
    dt = jnp.exp(log_step)[:, None]
    mag = jnp.exp(lam_re * dt)
    lb_re = mag * jnp.cos(lam_im * dt)
    lb_im = mag * jnp.sin(lam_im * dt)
    den = lam_re * lam_re + lam_im * lam_im
    f_re = ((lb_re - 1.0) * lam_re + lb_im * lam_im) / den
    f_im = (lb_im * lam_re - (lb_re - 1.0) * lam_im) / den
    bb_re = f_re[..., None] * b_re - f_im[..., None] * b_im
    bb_im = f_re[..., None] * b_im + f_im[..., None] * b_re
    gpb = S5_BLOCK_GROUPS
    n_blocks = n_groups // gpb
    eye = jnp.eye(gpb, dtype=F32)

    def in_blocks(w):
        w = w.transpose(0, 2, 1).reshape(n_blocks, gpb, S5_GROUP, n_state)
        return jnp.einsum('kgcp,gh->kgchp', w, eye).reshape(n_blocks, gpb * S5_GROUP, gpb * n_state)

    def out_blocks(w):
        w = w.reshape(n_blocks, gpb, S5_GROUP, n_state)
        return jnp.einsum('kgcp,gh->kgphc', w, eye).reshape(n_blocks, gpb * n_state, gpb * S5_GROUP)

    bb = jnp.concatenate([in_blocks(bb_re), in_blocks(bb_im)], axis=-1).astype(BF16)
    cc = jnp.concatenate([out_blocks(c_re), out_blocks(-c_im)], axis=1).astype(BF16)
    return bb, cc, lb_re.reshape(1, -1), lb_im.reshape(1, -1)


def _rope_lane_tables(seq):
    pos = jnp.arange(seq, dtype=F32)
    inv_freq = ROPE_THETA ** (-jnp.arange(0, QK_ROPE, 2, dtype=F32) / QK_ROPE)
    ang = pos[:, None] * inv_freq[None, :]
    cos, sin = jnp.cos(ang), jnp.sin(ang)
    pad = jnp.zeros((seq, V7X_LANES - QK_ROPE), F32)
    return (jnp.concatenate([cos, cos, pad], axis=-1),
            jnp.concatenate([-sin, sin, pad], axis=-1))


def _rope_weight_blocks(w_rope):
    half = QK_ROPE // 2
    t1, t2 = w_rope[:, :half], w_rope[:, half:]
    pad = jnp.zeros((w_rope.shape[0], V7X_LANES - QK_ROPE), w_rope.dtype)
    return jnp.concatenate([t1, t2, pad, t2, t1, pad], axis=-1)


def _routing(info, *, tm):
    n_tok = info.shape[0]
    n_assign = n_tok * TOP_K
    flat_e = info[:, :TOP_K].astype(jnp.int32).reshape(-1)
    experts = jnp.arange(N_EXPERTS, dtype=jnp.int32)
    counts = jnp.sum((flat_e[:, None] == experts[None, :]).astype(jnp.int32), axis=0)
    pad = (-counts) % tm
    filler = jnp.arange(N_EXPERTS * tm, dtype=jnp.int32)
    filler_key = jnp.where(filler % tm < pad[filler // tm], filler // tm, N_EXPERTS)
    keys = jnp.concatenate([flat_e, filler_key])
    ids = jnp.arange(keys.shape[0], dtype=jnp.int32)
    sorted_keys, order = lax.sort((keys, ids), num_keys=1, is_stable=True)
    is_real = order < n_assign
    tok_idx = jnp.where(is_real, order // TOP_K, 0)
    dst_idx = jnp.where(is_real, (order % TOP_K) * n_tok + order // TOP_K, order)
    tile_key = sorted_keys[::tm]
    tile_expert = jnp.minimum(tile_key, N_EXPERTS - 1)
    n_used = jnp.sum((tile_key < N_EXPERTS).astype(jnp.int32)).reshape(1)
    return tok_idx.reshape(-1, tm), dst_idx.reshape(-1, tm), tile_expert, n_used


def kernel(x, s5_lam_re, s5_lam_im, s5_log_step, s5_b_re, s5_b_im, s5_c_re, s5_c_im, s5_d, s5_w_glu,
           mla_q_w_a, mla_q_norm, mla_q_w_b, mla_o_w, kv_w_a, kv_norm, kv_w_b, ffn_w_in, ffn_w_out,
           moe_router, moe_w_in, moe_w_out, ln_g, ln_b):
    n_batch, seq, d_model = x.shape
    n_tok = n_batch * seq
    row = lambda v: v.reshape(1, -1).astype(F32)

    bb, cc, a_re, a_im = _s5_params(s5_lam_re[0], s5_lam_im[0], s5_log_step[0], s5_b_re[0], s5_b_im[0],
                                    s5_c_re[0], s5_c_im[0])
    n_time = min(32, seq)
    x1 = _s5_layer(x, bb, cc, a_re, a_im, row(s5_d[0]), s5_w_glu[0].astype(BF16),
                   row(ln_g[0, 0]), row(ln_b[0, 0]), n_time=n_time)
    x2 = _dense_ffn(x1, ffn_w_in[0].astype(BF16), ffn_w_out[0].astype(BF16),
                    row(ln_g[0, 1]), row(ln_b[0, 1]), n_batch=n_batch, n_time=n_time,
                    fc=ffn_w_out.shape[1] // 2)

    kv_lora = kv_norm.shape[0]
    wkva = jnp.concatenate([kv_w_a[:, :kv_lora], _rope_weight_blocks(kv_w_a[:, kv_lora:])],
                           axis=-1).astype(BF16)
    wq = mla_q_w_b[0].reshape(-1, MLA_HEADS, QK_NOPE + QK_ROPE)
    wqb = jnp.concatenate(
        [jnp.concatenate([wq[:, h, :QK_NOPE], _rope_weight_blocks(wq[:, h, QK_NOPE:])], axis=-1)
         for h in range(MLA_HEADS)], axis=-1).astype(BF16)
    cosx, sinx = _rope_lane_tables(seq)
    ts = min(512, seq)
    q, k, v = _qkv_proj(x2, wkva, row(kv_norm), kv_w_b.astype(BF16), mla_q_w_a[0].astype(BF16),
                        row(mla_q_norm[0]), wqb, cosx, sinx, ts=ts)
    w_in, w_out = moe_w_in[0], moe_w_out[0]
    o, w_in_bf = _attention(q, k, v, w_in.reshape(-1, w_in.shape[2]), tq=min(512, seq))
    wr = jnp.pad(moe_router[0], ((0, 0), (0, V7X_LANES - N_EXPERTS)))
    wr_hi = wr.astype(BF16)
    wr_lo = (wr - wr_hi.astype(F32)).astype(BF16)
    x3, info, w_out_bf = _oproj_router(o, x2, mla_o_w[0].astype(BF16), row(ln_g[1, 0]),
                                       row(ln_b[1, 0]), wr_hi, wr_lo,
                                       w_out.reshape(-1, w_out.shape[2]), ts=ts)

    tok_idx, dst_idx, tile_expert, n_used = _routing(info, tm=512)
    ys = _moe_experts(tile_expert, n_used, tok_idx, dst_idx,
                      x3.reshape(n_tok, V7X_SUBLANES, V7X_LANES), w_in_bf.reshape(w_in.shape),
                      w_out_bf.reshape(w_out.shape), tm=512, fc=w_out.shape[1] // 2)
    out = _combine(x3, info, ys.reshape(-1, V7X_LANES), row(ln_g[1, 1]), row(ln_b[1, 1]),
                   tc=min(512, n_tok))
    return out.reshape(n_batch, seq, d_model)
```

```python
import functools
import math

import jax
import jax.numpy as jnp
from jax import lax
from jax.experimental import pallas as pl
from jax.experimental.pallas import tpu as pltpu

F32 = jnp.float32
BF16 = jnp.bfloat16

V7X_LANES = 128
V7X_SUBLANES = 8
V7X_VMEM_LIMIT = 56 * 1024 * 1024

DEPTH = 2
ALPHA = (2.0 * DEPTH) ** 0.25
LN_EPS = 1e-5
RMS_EPS = 1e-6
ROPE_THETA = 10000.0

S5_GROUP = 16
S5_STATE = 64
S5_BLOCK_GROUPS = 16

MLA_HEADS = 8
QK_NOPE = 128
QK_ROPE = 64
V_DIM = 128
N_EXPERTS = 8
TOP_K = 2


def _const_spec(shape):
    zeros = (0,) * len(shape)
    return pl.BlockSpec(shape, lambda *_: zeros, pipeline_mode=pl.Buffered(1))


def _layer_norm(h, g, b):
    mu = jnp.mean(h, axis=-1, keepdims=True)
    c = h - mu
    var = jnp.mean(c * c, axis=-1, keepdims=True)
    return c * lax.rsqrt(var + LN_EPS) * g + b


def _gelu_tanh(y):
    return 0.5 * y * (1.0 + jnp.tanh(math.sqrt(2.0 / math.pi) * (y + 0.044715 * (y * y * y))))


def _s5_kernel(x_hbm, bb_ref, cc_ref, are_ref, aim_ref, d_ref, wglu_ref, g_ref, b_ref,
               o_ref, xin_ref, bu_ref, st_ref, sem, *, n_batch, n_time, n_blocks):
    half = bb_ref.shape[2] // 2
    cb = bb_ref.shape[1]
    slab = 512
    d_model = o_ref.shape[1]
    i = pl.program_id(0)
    slot = i % 2

    def chunk_copy(step, b, to_slot):
        return pltpu.make_async_copy(
            x_hbm.at[b, pl.ds(step * n_time, n_time), :], xin_ref.at[to_slot, :, b, :],
            sem.at[to_slot])

    @pl.when(i == 0)
    def _():
        st_ref[...] = jnp.zeros_like(st_ref)
        for b in range(n_batch):
            chunk_copy(0, b, 0).start()

    @pl.when(i + 1 < pl.num_programs(0))
    def _():
        for b in range(n_batch):
            chunk_copy(i + 1, b, 1 - slot).start()

    for b in range(n_batch):
        chunk_copy(i, b, slot).wait()

    x = xin_ref[slot].reshape(n_time * n_batch, d_model)
    xb = x.astype(BF16)
    for k in range(n_blocks):
        bu_ref[:, k * 2 * half:(k + 1) * 2 * half] = jnp.dot(
            xb[:, k * cb:(k + 1) * cb], bb_ref[k], preferred_element_type=F32)

    for k in range(n_blocks):
        for j in range(half // slab):
            re0 = k * 2 * half + j * slab
            im0 = re0 + half
            a0 = k * half + j * slab
            for bh in range(n_batch // V7X_SUBLANES):
                r0 = bh * V7X_SUBLANES
                ar = jnp.broadcast_to(are_ref[:, a0:a0 + slab], (V7X_SUBLANES, slab))
                ai = jnp.broadcast_to(aim_ref[:, a0:a0 + slab], (V7X_SUBLANES, slab))
                s_re = st_ref[r0:r0 + V7X_SUBLANES, re0:re0 + slab]
                s_im = st_ref[r0:r0 + V7X_SUBLANES, im0:im0 + slab]

                for t in range(n_time):
                    rows = pl.ds(t * n_batch + r0, V7X_SUBLANES)
                    b_re = bu_ref[rows, re0:re0 + slab]
                    b_im = bu_ref[rows, im0:im0 + slab]
                    s_re, s_im = (ar * s_re - ai * s_im + b_re, ar * s_im + ai * s_re + b_im)
                    bu_ref[rows, re0:re0 + slab] = s_re
                    bu_ref[rows, im0:im0 + slab] = s_im
                st_ref[r0:r0 + V7X_SUBLANES, re0:re0 + slab] = s_re
                st_ref[r0:r0 + V7X_SUBLANES, im0:im0 + slab] = s_im

    ys = []
    for k in range(n_blocks):
        s_blk = bu_ref[:, k * 2 * half:(k + 1) * 2 * half].astype(BF16)
        ys.append(jnp.dot(s_blk, cc_ref[k], preferred_element_type=F32))
    y = jnp.concatenate(ys, axis=-1) + d_ref[...] * x
    act = _gelu_tanh(y).astype(BF16)
    z = jnp.dot(act, wglu_ref[...], preferred_element_type=F32)
    mix = z[:, :d_model] * jax.nn.sigmoid(z[:, d_model:])
    o_ref[...] = _layer_norm(ALPHA * x + mix, g_ref[...], b_ref[...])


def _s5_layer(x, bb, cc, a_re, a_im, d_skip, w_glu, ln_g, ln_b, *, n_time):
    n_batch, seq, d_model = x.shape
    n_rows = n_batch * seq
    n_blocks = bb.shape[0]
    rows = n_time * n_batch
    state_w = n_blocks * bb.shape[2]
    kern = functools.partial(_s5_kernel, n_batch=n_batch, n_time=n_time, n_blocks=n_blocks)
    return pl.pallas_call(
        kern,
        grid=(n_rows // rows,),
        in_specs=[
            pl.BlockSpec(memory_space=pl.ANY),
            _const_spec(bb.shape), _const_spec(cc.shape),
            _const_spec(a_re.shape), _const_spec(a_im.shape), _const_spec(d_skip.shape),
            _const_spec(w_glu.shape), _const_spec(ln_g.shape), _const_spec(ln_b.shape),
        ],
        out_specs=pl.BlockSpec((rows, d_model), lambda i: (i, 0)),
        out_shape=jax.ShapeDtypeStruct((n_rows, d_model), F32),
        scratch_shapes=[
            pltpu.VMEM((2, n_time, n_batch, d_model), F32),
            pltpu.VMEM((rows, state_w), F32), pltpu.VMEM((n_batch, state_w), F32),
            pltpu.SemaphoreType.DMA((2,)),
        ],
        compiler_params=pltpu.CompilerParams(
            dimension_semantics=("arbitrary",), vmem_limit_bytes=V7X_VMEM_LIMIT),
        name="s5_mixer",
    )(x, bb, cc, a_re, a_im, d_skip, w_glu, ln_g, ln_b)


def _ffn_kernel(x_ref, wg_ref, wu_ref, wo_ref, g_ref, b_ref, o_hbm, obuf_ref, sem, *, fc, n_batch):
    i = pl.program_id(0)
    last = pl.num_programs(0) - 1
    slot = i % 2
    n_time = x_ref.shape[0] // n_batch

    def out_copy(step, b, from_slot):
        return pltpu.make_async_copy(
            obuf_ref.at[from_slot, :, b, :], o_hbm.at[b, pl.ds(step * n_time, n_time), :],
            sem.at[from_slot])

    @pl.when(i >= 2)
    def _():
        for b in range(n_batch):
            out_copy(i - 2, b, slot).wait()

    x = x_ref[...]
    xb = x.astype(BF16)
    acc = None
    for j in range(wo_ref.shape[0] // fc):
        gate = jnp.dot(xb, wg_ref[:, j * fc:(j + 1) * fc], preferred_element_type=F32)
        up = jnp.dot(xb, wu_ref[:, j * fc:(j + 1) * fc], preferred_element_type=F32)
        h = (gate * jax.nn.sigmoid(gate) * up).astype(BF16)
        part = jnp.dot(h, wo_ref[j * fc:(j + 1) * fc, :], preferred_element_type=F32)
        acc = part if acc is None else acc + part
    out = _layer_norm(ALPHA * x + acc, g_ref[...], b_ref[...])
    obuf_ref[slot] = out.reshape(n_time, n_batch, out.shape[1])
    for b in range(n_batch):
        out_copy(i, b, slot).start()

    @pl.when((i == last) & (i >= 1))
    def _():
        for b in range(n_batch):
            out_copy(i - 1, b, 1 - slot).wait()

    @pl.when(i == last)
    def _():
        for b in range(n_batch):
            out_copy(i, b, slot).wait()


def _dense_ffn(x, w_in, w_out, ln_g, ln_b, *, n_batch, n_time, fc):
    n_rows, d_model = x.shape
    ffn = w_out.shape[0]
    tm = n_time * n_batch
    kern = functools.partial(_ffn_kernel, fc=fc, n_batch=n_batch)
    return pl.pallas_call(
        kern,
        grid=(n_rows // tm,),
        in_specs=[
            pl.BlockSpec((tm, d_model), lambda i: (i, 0)),
            pl.BlockSpec((d_model, ffn), lambda i: (0, 0), pipeline_mode=pl.Buffered(1)),
            pl.BlockSpec((d_model, ffn), lambda i: (0, 1), pipeline_mode=pl.Buffered(1)),
            _const_spec(w_out.shape), _const_spec(ln_g.shape), _const_spec(ln_b.shape),
        ],
        out_specs=pl.BlockSpec(memory_space=pl.ANY),
        out_shape=jax.ShapeDtypeStruct((n_batch, n_rows // n_batch, d_model), F32),
        scratch_shapes=[pltpu.VMEM((2, n_time, n_batch, d_model), F32),
                        pltpu.SemaphoreType.DMA((2,))],
        compiler_params=pltpu.CompilerParams(
            dimension_semantics=("arbitrary",), vmem_limit_bytes=V7X_VMEM_LIMIT),
        name="dense_ffn",
    )(x, w_in, w_in, w_out, ln_g, ln_b)


def _qkv_kernel(x_ref, wkva_ref, kvn_ref, wkvb_ref, wqa_ref, qn_ref, wqb_ref, cos_ref, sin_ref,
                q_ref, k_ref, v_ref, *, scale):
    xb = x_ref[0].astype(BF16)
    cosx = cos_ref[...]
    sinx = sin_ref[...]

    kva = jnp.dot(xb, wkva_ref[...], preferred_element_type=F32)
    ckv = kva[:, :V7X_LANES]
    ckv = ckv * lax.rsqrt(jnp.mean(ckv * ckv, axis=-1, keepdims=True) + RMS_EPS) * kvn_ref[...]
    k_rope = (kva[:, V7X_LANES:2 * V7X_LANES] * cosx
              + kva[:, 2 * V7X_LANES:3 * V7X_LANES] * sinx).astype(BF16)
    kv = jnp.dot(ckv.astype(BF16), wkvb_ref[...], preferred_element_type=F32)

    cq = jnp.dot(xb, wqa_ref[...], preferred_element_type=F32)
    cq = cq * lax.rsqrt(jnp.mean(cq * cq, axis=-1, keepdims=True) + RMS_EPS) * qn_ref[...]
    q = jnp.dot(cq.astype(BF16), wqb_ref[...], preferred_element_type=F32)

    for h in range(MLA_HEADS):
        kb = h * (QK_NOPE + V_DIM)
        k_ref[0, h, :, :QK_NOPE] = kv[:, kb:kb + QK_NOPE].astype(BF16)
        k_ref[0, h, :, QK_NOPE:] = k_rope
        v_ref[0, h] = kv[:, kb + QK_NOPE:kb + QK_NOPE + V_DIM].astype(BF16)
        qb = h * 3 * V7X_LANES
        q_ref[0, h, :, :QK_NOPE] = (q[:, qb:qb + QK_NOPE] * scale).astype(BF16)
        q_rope = (q[:, qb + V7X_LANES:qb + 2 * V7X_LANES] * cosx
                  + q[:, qb + 2 * V7X_LANES:qb + 3 * V7X_LANES] * sinx)
        q_ref[0, h, :, QK_NOPE:] = (q_rope * scale).astype(BF16)


def _qkv_proj(x, wkva, kvn, wkvb, wqa, qn, wqb, cosx, sinx, *, ts):
    n_batch, seq, d_model = x.shape
    scale = 1.0 / math.sqrt(QK_NOPE + QK_ROPE)
    dk = QK_NOPE + V7X_LANES
    kern = functools.partial(_qkv_kernel, scale=scale)
    return pl.pallas_call(
        kern,
        grid=(n_batch, seq // ts),
        in_specs=[
            pl.BlockSpec((1, ts, d_model), lambda b, s: (b, s, 0)),
            _const_spec(wkva.shape), _const_spec(kvn.shape), _const_spec(wkvb.shape),
            _const_spec(wqa.shape), _const_spec(qn.shape), _const_spec(wqb.shape),
            pl.BlockSpec((ts, V7X_LANES), lambda b, s: (s, 0)),
            pl.BlockSpec((ts, V7X_LANES), lambda b, s: (s, 0)),
        ],
        out_specs=[
            pl.BlockSpec((1, MLA_HEADS, ts, dk), lambda b, s: (b, 0, s, 0)),
            pl.BlockSpec((1, MLA_HEADS, ts, dk), lambda b, s: (b, 0, s, 0)),
            pl.BlockSpec((1, MLA_HEADS, ts, V_DIM), lambda b, s: (b, 0, s, 0)),
        ],
        out_shape=[
            jax.ShapeDtypeStruct((n_batch, MLA_HEADS, seq, dk), BF16),
            jax.ShapeDtypeStruct((n_batch, MLA_HEADS, seq, dk), BF16),
            jax.ShapeDtypeStruct((n_batch, MLA_HEADS, seq, V_DIM), BF16),
        ],
        compiler_params=pltpu.CompilerParams(
            dimension_semantics=("parallel", "parallel"), vmem_limit_bytes=V7X_VMEM_LIMIT),
        name="qkv_proj",
    )(x, wkva, kvn, wkvb, wqa, qn, wqb, cosx, sinx)


def _attn_kernel(q_ref, k_ref, v_ref, w_ref, o_ref, wb_ref, *, tq):
    wb_ref[...] = w_ref[...].astype(BF16)
    seq = q_ref.shape[2]
    n_tiles = seq // tq
    row = lax.broadcasted_iota(jnp.int32, (tq, tq), 0)
    col = lax.broadcasted_iota(jnp.int32, (tq, tq), 1)
    causal = col <= row
    for qi in range(n_tiles):
        q = q_ref[0, 0, qi * tq:(qi + 1) * tq, :]
        m = l = acc = None
        for kj in range(qi + 1):
            k = k_ref[0, 0, kj * tq:(kj + 1) * tq, :]
            v = v_ref[0, 0, kj * tq:(kj + 1) * tq, :]
            s = lax.dot_general(q, k, (((1,), (1,)), ((), ())), preferred_element_type=F32)
            if kj == qi:
                s = jnp.where(causal, s, -jnp.inf)
            m_new = jnp.max(s, axis=-1, keepdims=True)
            if kj > 0:
                m_new = jnp.maximum(m, m_new)
            p = jnp.exp(s - m_new)
            pv = jnp.dot(p.astype(BF16), v, preferred_element_type=F32)
            if kj == 0:
                l = jnp.sum(p, axis=-1, keepdims=True)
                acc = pv
            else:
                corr = jnp.exp(m - m_new)
                l = corr * l + jnp.sum(p, axis=-1, keepdims=True)
                acc = corr * acc + pv
            m = m_new
        o_ref[0, qi * tq:(qi + 1) * tq, :] = (acc / l).astype(o_ref.dtype)


def _cast_slab_rows(w2d, n_steps):
    rows = w2d.shape[0] // n_steps
    assert rows * n_steps == w2d.shape[0] and rows % (2 * V7X_SUBLANES) == 0
    return rows


def _attention(q, k, v, w_cast, *, tq):
    n_batch, n_heads, seq, dk = q.shape
    dv = v.shape[3]
    w_rows = _cast_slab_rows(w_cast, n_batch * n_heads)
    w_spec = pl.BlockSpec((w_rows, w_cast.shape[1]), lambda b, h: (b * n_heads + h, 0))
    kern = functools.partial(_attn_kernel, tq=tq)
    return pl.pallas_call(
        kern,
        grid=(n_batch, n_heads),
        in_specs=[
            pl.BlockSpec((1, 1, seq, dk), lambda b, h: (b, h, 0, 0)),
            pl.BlockSpec((1, 1, seq, dk), lambda b, h: (b, h, 0, 0)),
            pl.BlockSpec((1, 1, seq, dv), lambda b, h: (b, h, 0, 0)),
            w_spec,
        ],
        out_specs=[pl.BlockSpec((1, seq, dv), lambda b, h: (b, 0, h)), w_spec],
        out_shape=[jax.ShapeDtypeStruct((n_batch, seq, n_heads * dv), BF16),
                   jax.ShapeDtypeStruct(w_cast.shape, BF16)],
        compiler_params=pltpu.CompilerParams(
            dimension_semantics=("parallel", "parallel"), vmem_limit_bytes=V7X_VMEM_LIMIT),
        name="mla_attention",
    )(q, k, v, w_cast)


def _oproj_router_kernel(o_ref, x_ref, wo_ref, g_ref, b_ref, wrh_ref, wrl_ref, w_ref,
                         y_ref, info_ref, wb_ref):
    wb_ref[...] = w_ref[...].astype(BF16)
    half = info_ref.shape[0] // 2
    for r in range(2):
        rows = pl.ds(r * half, half)
        y, info = _oproj_router_rows(o_ref[0, rows, :], x_ref[0, rows, :], wo_ref, g_ref, b_ref,
                                     wrh_ref, wrl_ref)
        for c in range(V7X_SUBLANES):
            y_ref[pl.ds(r * half * V7X_SUBLANES + c, half, stride=V7X_SUBLANES), :] = (
                y[:, c * V7X_LANES:(c + 1) * V7X_LANES])
        info_ref[rows, :] = info


def _oproj_router_rows(o, x, wo_ref, g_ref, b_ref, wrh_ref, wrl_ref):
    mix = jnp.dot(o, wo_ref[...], preferred_element_type=F32)
    y = _layer_norm(ALPHA * x + mix, g_ref[...], b_ref[...])

    y_hi = y.astype(BF16)
    y_lo = (y - y_hi.astype(F32)).astype(BF16)
    logits = (jnp.dot(y_hi, wrh_ref[...], preferred_element_type=F32)
              + jnp.dot(y_lo, wrh_ref[...], preferred_element_type=F32)
              + jnp.dot(y_hi, wrl_ref[...], preferred_element_type=F32))
    lane = lax.broadcasted_iota(jnp.int32, logits.shape, 1)
    logits = jnp.where(lane < N_EXPERTS, logits, -jnp.inf)
    m1 = jnp.max(logits, axis=-1, keepdims=True)
    i1 = jnp.min(jnp.where(logits == m1, lane, V7X_LANES), axis=-1, keepdims=True)
    rest = jnp.where(lane == i1, -jnp.inf, logits)
    m2 = jnp.max(rest, axis=-1, keepdims=True)
    i2 = jnp.min(jnp.where(rest == m2, lane, V7X_LANES), axis=-1, keepdims=True)
    e2 = jnp.exp(m2 - m1)
    den = 1.0 + e2
    info = jnp.where(lane == 0, i1.astype(F32),
                     jnp.where(lane == 1, i2.astype(F32),
                               jnp.where(lane == 2, 1.0 / den,
                                         jnp.where(lane == 3, e2 / den, 0.0))))
    return y, info


def _oproj_router(o, x, wo, ln_g, ln_b, wr_hi, wr_lo, w_cast, *, ts):
    n_batch, seq, d_attn = o.shape
    d_model = x.shape[2]
    assert d_model == V7X_SUBLANES * V7X_LANES
    nst = seq // ts
    w_rows = _cast_slab_rows(w_cast, n_batch * nst)
    w_spec = pl.BlockSpec((w_rows, w_cast.shape[1]), lambda b, s: (b * nst + s, 0))
    return pl.pallas_call(
        _oproj_router_kernel,
        grid=(n_batch, nst),
        in_specs=[
            pl.BlockSpec((1, ts, d_attn), lambda b, s: (b, s, 0)),
            pl.BlockSpec((1, ts, d_model), lambda b, s: (b, s, 0)),
            _const_spec(wo.shape), _const_spec(ln_g.shape), _const_spec(ln_b.shape),
            _const_spec(wr_hi.shape), _const_spec(wr_lo.shape),
            w_spec,
        ],
        out_specs=[
            pl.BlockSpec((ts * V7X_SUBLANES, V7X_LANES), lambda b, s: (b * nst + s, 0)),
            pl.BlockSpec((ts, V7X_LANES), lambda b, s: (b * nst + s, 0)),
            w_spec,
        ],
        out_shape=[
            jax.ShapeDtypeStruct((n_batch * seq * V7X_SUBLANES, V7X_LANES), F32),
            jax.ShapeDtypeStruct((n_batch * seq, V7X_LANES), F32),
            jax.ShapeDtypeStruct(w_cast.shape, BF16),
        ],
        compiler_params=pltpu.CompilerParams(
            dimension_semantics=("parallel", "parallel"), vmem_limit_bytes=V7X_VMEM_LIMIT),
        name="oproj_router",
    )(o, x, wo, ln_g, ln_b, wr_hi, wr_lo, w_cast)


def _moe_kernel(te_ref, nu_ref, tok_ref, dst_ref, x_hbm, wg_ref, wu_ref, wo_ref, y_hbm,
                xs_ref, ob_ref, xb_ref, acc_ref, gsem, ssem, *, tm, nf):
    del te_ref
    s = pl.program_id(0)
    f = pl.program_id(1)
    gslot = s % 2
    cslot = 1 - gslot
    valid = (s >= 1) & (s <= nu_ref[0])
    sub, lanes = V7X_SUBLANES, V7X_LANES

    def wait_tile_rows(buf_ref, sem):
        slot_rows = tm * sub
        pltpu.make_async_copy(
            buf_ref.at[pl.ds(0, slot_rows)], buf_ref.at[pl.ds(slot_rows, slot_rows)], sem).wait()

    @pl.when((s == 0) & (f == 0))
    def _():
        ob_ref[...] = jnp.zeros_like(ob_ref)

    @pl.when((s >= 1) & (f == 0))
    def _():
        wait_tile_rows(xs_ref, gsem)

    for parity in range(2):
        @pl.when((f == 0) & (gslot == parity))
        def _(parity=parity):
            for row in range(tm):
                vmem_rows = pl.ds((parity * tm + row) * sub, sub)
                pltpu.make_async_copy(
                    x_hbm.at[tok_ref[0, 0, row]], xs_ref.at[vmem_rows], gsem).start()

        @pl.when((f == nf - 1) & (gslot == parity))
        def _(parity=parity):
            for row in range(tm):
                vmem_rows = pl.ds((parity * tm + row) * sub, sub)
                pltpu.make_async_copy(
                    ob_ref.at[vmem_rows], y_hbm.at[dst_ref[0, 0, row]], ssem).start()

    def gathered_rows_bf16():
        return jnp.concatenate(
            [xs_ref[pl.ds(cslot * tm * sub + c, tm, stride=sub), :].astype(BF16)
             for c in range(sub)], axis=-1)

    def store_result(res):
        for c in range(sub):
            ob_ref[pl.ds(cslot * tm * sub + c, tm, stride=sub), :] = (
                res[:, c * lanes:(c + 1) * lanes])

    for chunk in range(nf):
        @pl.when(valid & (f == chunk))
        def _(chunk=chunk):
            if chunk == 0:
                xb = gathered_rows_bf16()
                if nf > 1:
                    xb_ref[...] = xb
            else:
                xb = xb_ref[...]
            gate = jnp.dot(xb, wg_ref[0], preferred_element_type=F32)
            up = jnp.dot(xb, wu_ref[0], preferred_element_type=F32)
            h = (gate * jax.nn.sigmoid(gate) * up).astype(BF16)
            part = jnp.dot(h, wo_ref[0], preferred_element_type=F32)
            if nf == 1:
                store_result(part)
            elif chunk == 0:
                acc_ref[...] = part
            elif chunk < nf - 1:
                acc_ref[...] += part
            else:
                store_result(acc_ref[...] + part)

    @pl.when(f == nf - 1)
    def _():
        wait_tile_rows(ob_ref, ssem)

    @pl.when((s == pl.num_programs(0) - 1) & (f == nf - 1))
    def _():
        wait_tile_rows(xs_ref, gsem)


def _moe_experts(tile_expert, n_used, tok_idx, dst_idx, x_tiles, w_in, w_out, *, tm, fc):
    d_model = x_tiles.shape[1] * x_tiles.shape[2]
    edim = w_out.shape[1]
    nf = edim // fc
    n_tiles = tok_idx.shape[0]
    n_rows = n_tiles * tm
    n_steps = n_tiles + 2

    def expert_chunk(s, f, te, nu):
        tile = jnp.clip(s - 1, 0, nu[0] - 1)
        chunk = jnp.where(s > nu[0], nf - 1, jnp.where(s < 1, 0, f))
        return te[tile], chunk

    def w_in_map(half):
        def index(s, f, te, nu):
            e, chunk = expert_chunk(s, f, te, nu)
            return e, 0, chunk + half * nf
        return index

    def w_out_map(s, f, te, nu):
        e, chunk = expert_chunk(s, f, te, nu)
        return e, chunk, 0

    grid_spec = pltpu.PrefetchScalarGridSpec(
        num_scalar_prefetch=2,
        grid=(n_steps, nf),
        in_specs=[
            pl.BlockSpec((1, 1, tm), lambda s, f, te, nu: (jnp.minimum(s, n_tiles - 1), 0, 0),
                         memory_space=pltpu.SMEM),
            pl.BlockSpec((1, 1, tm), lambda s, f, te, nu: (s, 0, 0), memory_space=pltpu.SMEM),
            pl.BlockSpec(memory_space=pl.ANY),
            pl.BlockSpec((1, d_model, fc), w_in_map(0)),
            pl.BlockSpec((1, d_model, fc), w_in_map(1)),
            pl.BlockSpec((1, fc, d_model), w_out_map),
        ],
        out_specs=pl.BlockSpec(memory_space=pl.ANY),
        scratch_shapes=[
            pltpu.VMEM((2 * tm * V7X_SUBLANES, V7X_LANES), F32),
            pltpu.VMEM((2 * tm * V7X_SUBLANES, V7X_LANES), F32),
            pltpu.VMEM((tm, d_model), BF16), pltpu.VMEM((tm, d_model), F32),
            pltpu.SemaphoreType.DMA(()), pltpu.SemaphoreType.DMA(()),
        ],
    )
    warm = n_rows + jnp.arange(tm, dtype=jnp.int32)[None, :]
    dst_by_step = jnp.concatenate([warm, warm, dst_idx], axis=0)
    kern = functools.partial(_moe_kernel, tm=tm, nf=nf)
    return pl.pallas_call(
        kern,
        grid_spec=grid_spec,
        out_shape=jax.ShapeDtypeStruct((n_rows + tm, V7X_SUBLANES, V7X_LANES), F32),
        compiler_params=pltpu.CompilerParams(
            dimension_semantics=("arbitrary", "arbitrary"), vmem_limit_bytes=V7X_VMEM_LIMIT),
        name="moe_experts",
    )(tile_expert, n_used, tok_idx.reshape(n_tiles, 1, tm), dst_by_step.reshape(n_steps, 1, tm),
      x_tiles, w_in, w_in, w_out)


def _combine_kernel(x_ref, info_ref, y0_ref, y1_ref, g_ref, b_ref, o_ref):
    tc, d_model = o_ref.shape
    sub, lanes = V7X_SUBLANES, V7X_LANES
    info = info_ref[...]
    g0, g1 = info[:, 2:3], info[:, 3:4]
    pieces = []
    for c in range(sub):
        rows = pl.ds(c, tc, stride=sub)
        pieces.append(ALPHA * x_ref[rows, :] + (y0_ref[rows, :] * g0 + y1_ref[rows, :] * g1))
    mu = jnp.sum(sum(pieces), axis=-1, keepdims=True) / d_model
    cen = [p - mu for p in pieces]
    var = jnp.sum(sum(c * c for c in cen), axis=-1, keepdims=True) / d_model
    rstd = lax.rsqrt(var + LN_EPS)
    for c in range(sub):
        cols = slice(c * lanes, (c + 1) * lanes)
        o_ref[:, cols] = cen[c] * rstd * g_ref[:, cols] + b_ref[:, cols]


def _combine(x_tiles, info, y_tiles, ln_g, ln_b, *, tc):
    n_tok = info.shape[0]
    d_model = V7X_SUBLANES * V7X_LANES
    nt = n_tok // tc
    tile_block = (tc * V7X_SUBLANES, V7X_LANES)
    return pl.pallas_call(
        _combine_kernel,
        grid=(nt,),
        in_specs=[
            pl.BlockSpec(tile_block, lambda i: (i, 0)),
            pl.BlockSpec((tc, V7X_LANES), lambda i: (i, 0)),
            pl.BlockSpec(tile_block, lambda i: (i, 0)),
            pl.BlockSpec(tile_block, lambda i: (i + nt, 0)),
            _const_spec(ln_g.shape), _const_spec(ln_b.shape),
        ],
        out_specs=pl.BlockSpec((tc, d_model), lambda i: (i, 0)),
        out_shape=jax.ShapeDtypeStruct((n_tok, d_model), F32),
        compiler_params=pltpu.CompilerParams(
            dimension_semantics=("parallel",), vmem_limit_bytes=V7X_VMEM_LIMIT),
        name="moe_combine",
    )(x_tiles, info, y_tiles, y_tiles, ln_g, ln_b)


def _s5_params(lam_re, lam_im, log_step, b_re, b_im, c_re, c_im):
    n_groups, n_state = lam_re.shape
    dt = jnp.exp(log_step)[:, None]
    mag = jnp.exp(lam_re * dt)
    lb_re = mag * jnp.cos(lam_im * dt)
    lb_im = mag * jnp.sin(lam_im * dt)
    den = lam_re * lam_re + lam_im * lam_im
    f_re = ((lb_re - 1.0) * lam_re + lb_im * lam_im) / den
    f_im = (lb_im * lam_re - (lb_re - 1.0) * lam_im) / den
    bb_re = f_re[..., None] * b_re - f_im[..., None] * b_im
    bb_im = f_re[..., None] * b_im + f_im[..., None] * b_re
    gpb = S5_BLOCK_GROUPS
    n_blocks = n_groups // gpb
    eye = jnp.eye(gpb, dtype=F32)

    def in_blocks(w):
        w = w.transpose(0, 2, 1).reshape(n_blocks, gpb, S5_GROUP, n_state)
        return jnp.einsum('kgcp,gh->kgchp', w, eye).reshape(n_blocks, gpb * S5_GROUP, gpb * n_state)

    def out_blocks(w):
        w = w.reshape(n_blocks, gpb, S5_GROUP, n_state)
        return jnp.einsum('kgcp,gh->kgphc', w, eye).reshape(n_blocks, gpb * n_state, gpb * S5_GROUP)

    bb = jnp.concatenate([in_blocks(bb_re), in_blocks(bb_im)], axis=-1).astype(BF16)
    cc = jnp.concatenate([out_blocks(c_re), out_blocks(-c_im)], axis=1).astype(BF16)
    return bb, cc, lb_re.reshape(1, -1), lb_im.reshape(1, -1)


def _rope_lane_tables(seq):
    pos = jnp.arange(seq, dtype=F32)
    inv_freq = ROPE_THETA ** (-jnp.arange(0, QK_ROPE, 2, dtype=F32) / QK_ROPE)
    ang = pos[:, None] * inv_freq[None, :]
    cos, sin = jnp.cos(ang), jnp.sin(ang)
    pad = jnp.zeros((seq, V7X_LANES - QK_ROPE), F32)
    return (jnp.concatenate([cos, cos, pad], axis=-1),
            jnp.concatenate([-sin, sin, pad], axis=-1))


def _rope_weight_blocks(w_rope):
    half = QK_ROPE // 2
    t1, t2 = w_rope[:, :half], w_rope[:, half:]
    pad = jnp.zeros((w_rope.shape[0], V7X_LANES - QK_ROPE), w_rope.dtype)
    return jnp.concatenate([t1, t2, pad, t2, t1, pad], axis=-1)


def _routing(info, *, tm):
    n_tok = info.shape[0]
    n_assign = n_tok * TOP_K
    flat_e = info[:, :TOP_K].astype(jnp.int32).reshape(-1)
    experts = jnp.arange(N_EXPERTS, dtype=jnp.int32)
    counts = jnp.sum((flat_e[:, None] == experts[None, :]).astype(jnp.int32), axis=0)
    pad = (-counts) % tm
    filler = jnp.arange(N_EXPERTS * tm, dtype=jnp.int32)
    filler_key = jnp.where(filler % tm < pad[filler // tm], filler // tm, N_EXPERTS)
    keys = jnp.concatenate([flat_e, filler_key])
    ids = jnp.arange(keys.shape[0], dtype=jnp.int32)
    sorted_keys, order = lax.sort((keys, ids), num_keys=1, is_stable=True)
    is_real = order < n_assign
    tok_idx = jnp.where(is_real, order // TOP_K, 0)
    dst_idx = jnp.where(is_real, (order % TOP_K) * n_tok + order // TOP_K, order)
    tile_key = sorted_keys[::tm]
    tile_expert = jnp.minimum(tile_key, N_EXPERTS - 1)
    n_used = jnp.sum((tile_key < N_EXPERTS).astype(jnp.int32)).reshape(1)
    return tok_idx.reshape(-1, tm), dst_idx.reshape(-1, tm), tile_expert, n_used


def kernel(x, s5_lam_re, s5_lam_im, s5_log_step, s5_b_re, s5_b_im, s5_c_re, s5_c_im, s5_d, s5_w_glu,
           mla_q_w_a, mla_q_norm, mla_q_w_b, mla_o_w, kv_w_a, kv_norm, kv_w_b, ffn_w_in, ffn_w_out,
           moe_router, moe_w_in, moe_w_out, ln_g, ln_b):
    n_batch, seq, d_model = x.shape
    n_tok = n_batch * seq
    row = lambda v: v.reshape(1, -1).astype(F32)

    bb, cc, a_re, a_im = _s5_params(s5_lam_re[0], s5_lam_im[0], s5_log_step[0], s5_b_re[0], s5_b_im[0],
                                    s5_c_re[0], s5_c_im[0])
    n_time = min(32, seq)
    x1 = _s5_layer(x, bb, cc, a_re, a_im, row(s5_d[0]), s5_w_glu[0].astype(BF16),
                   row(ln_g[0, 0]), row(ln_b[0, 0]), n_time=n_time)
    x2 = _dense_ffn(x1, ffn_w_in[0].astype(BF16), ffn_w_out[0].astype(BF16),
                    row(ln_g[0, 1]), row(ln_b[0, 1]), n_batch=n_batch, n_time=n_time,
                    fc=ffn_w_out.shape[1] // 2)

    kv_lora = kv_norm.shape[0]
    wkva = jnp.concatenate([kv_w_a[:, :kv_lora], _rope_weight_blocks(kv_w_a[:, kv_lora:])],
                           axis=-1).astype(BF16)
    wq = mla_q_w_b[0].reshape(-1, MLA_HEADS, QK_NOPE + QK_ROPE)
    wqb = jnp.concatenate(
        [jnp.concatenate([wq[:, h, :QK_NOPE], _rope_weight_blocks(wq[:, h, QK_NOPE:])], axis=-1)
         for h in range(MLA_HEADS)], axis=-1).astype(BF16)
    cosx, sinx = _rope_lane_tables(seq)
    ts = min(512, seq)
    q, k, v = _qkv_proj(x2, wkva, row(kv_norm), kv_w_b.astype(BF16), mla_q_w_a[0].astype(BF16),
                        row(mla_q_norm[0]), wqb, cosx, sinx, ts=ts)
    w_in, w_out = moe_w_in[0], moe_w_out[0]
    o, w_in_bf = _attention(q, k, v, w_in.reshape(-1, w_in.shape[2]), tq=min(512, seq))
    wr = jnp.pad(moe_router[0], ((0, 0), (0, V7X_LANES - N_EXPERTS)))
    wr_hi = wr.astype(BF16)
    wr_lo = (wr - wr_hi.astype(F32)).astype(BF16)
    x3, info, w_out_bf = _oproj_router(o, x2, mla_o_w[0].astype(BF16), row(ln_g[1, 0]),
                                       row(ln_b[1, 0]), wr_hi, wr_lo,
                                       w_out.reshape(-1, w_out.shape[2]), ts=ts)

    tok_idx, dst_idx, tile_expert, n_used = _routing(info, tm=512)
    ys = _moe_experts(tile_expert, n_used, tok_idx, dst_idx,
                      x3.reshape(n_tok, V7X_SUBLANES, V7X_LANES), w_in_bf.reshape(w_in.shape),
                      w_out_bf.reshape(w_out.shape), tm=512, fc=w_out.shape[1] // 2)
    out = _combine(x3, info, ys.reshape(-1, V7X_LANES), row(ln_g[1, 1]), row(ln_b[1, 1]),
                   tc=min(512, n_tok))
    return out.reshape(n_batch, seq, d_model)
```

```python
import functools
import math

import jax
import jax.numpy as jnp
from jax import lax
from jax.experimental import pallas as pl
from jax.experimental.pallas import tpu as pltpu

F32 = jnp.float32
BF16 = jnp.bfloat16

V7X_LANES = 128
V7X_SUBLANES = 8
V7X_MXU_WIDTH = 256
V7X_VMEM_LIMIT = 56 * 1024 * 1024

DEPTH = 2
ALPHA = (2.0 * DEPTH) ** 0.25
LN_EPS = 1e-5
RMS_EPS = 1e-6
ROPE_THETA = 10000.0

S5_GROUP = 16
S5_STATE = 64
S5_BLOCK_GROUPS = 16

MLA_HEADS = 8
QK_NOPE = 128
QK_ROPE = 64
V_DIM = 128
N_EXPERTS = 8
TOP_K = 2

ROW_TILE = 512
S5_TIME_CHUNK = 32


def _const_spec(shape):
    zeros = (0,) * len(shape)
    return pl.BlockSpec(shape, lambda *_: zeros, pipeline_mode=pl.Buffered(1))


def _layer_norm(h, g, b):
    mu = jnp.mean(h, axis=-1, keepdims=True)
    c = h - mu
    var = jnp.mean(c * c, axis=-1, keepdims=True)
    return c * lax.rsqrt(var + LN_EPS) * g + b


def _gelu_tanh(y):
    return 0.5 * y * (1.0 + jnp.tanh(math.sqrt(2.0 / math.pi) * (y + 0.044715 * (y * y * y))))


def _s5_kernel(x_hbm, bb_ref, cc_ref, are_ref, aim_ref, d_ref, wglu_ref, g_ref, b_ref,
               o_ref, xin_ref, bu_ref, st_ref, sem, *, n_batch, n_time, n_blocks):
    half = bb_ref.shape[2] // 2
    cb = bb_ref.shape[1]
    slab = 512
    d_model = o_ref.shape[1]
    i = pl.program_id(0)
    slot = i % 2

    def chunk_copy(step, b, to_slot):
        return pltpu.make_async_copy(
            x_hbm.at[b, pl.ds(step * n_time, n_time), :], xin_ref.at[to_slot, :, b, :],
            sem.at[to_slot])

    @pl.when(i == 0)
    def _():
        st_ref[...] = jnp.zeros_like(st_ref)
        for b in range(n_batch):
            chunk_copy(0, b, 0).start()

    @pl.when(i + 1 < pl.num_programs(0))
    def _():
        for b in range(n_batch):
            chunk_copy(i + 1, b, 1 - slot).start()

    for b in range(n_batch):
        chunk_copy(i, b, slot).wait()

    x = xin_ref[slot].reshape(n_time * n_batch, d_model)
    xb = x.astype(BF16)
    for k in range(n_blocks):
        bu_ref[:, k * 2 * half:(k + 1) * 2 * half] = jnp.dot(
            xb[:, k * cb:(k + 1) * cb], bb_ref[k], preferred_element_type=F32)

    for k in range(n_blocks):
        for j in range(half // slab):
            re0 = k * 2 * half + j * slab
            im0 = re0 + half
            a0 = k * half + j * slab
            for bh in range(n_batch // V7X_SUBLANES):
                r0 = bh * V7X_SUBLANES
                ar = jnp.broadcast_to(are_ref[:, a0:a0 + slab], (V7X_SUBLANES, slab))
                ai = jnp.broadcast_to(aim_ref[:, a0:a0 + slab], (V7X_SUBLANES, slab))
                s_re = st_ref[r0:r0 + V7X_SUBLANES, re0:re0 + slab]
                s_im = st_ref[r0:r0 + V7X_SUBLANES, im0:im0 + slab]

                for t in range(n_time):
                    rows = pl.ds(t * n_batch + r0, V7X_SUBLANES)
                    b_re = bu_ref[rows, re0:re0 + slab]
                    b_im = bu_ref[rows, im0:im0 + slab]
                    s_re, s_im = (ar * s_re - ai * s_im + b_re, ar * s_im + ai * s_re + b_im)
                    bu_ref[rows, re0:re0 + slab] = s_re
                    bu_ref[rows, im0:im0 + slab] = s_im
                st_ref[r0:r0 + V7X_SUBLANES, re0:re0 + slab] = s_re
                st_ref[r0:r0 + V7X_SUBLANES, im0:im0 + slab] = s_im

    ys = []
    for k in range(n_blocks):
        s_blk = bu_ref[:, k * 2 * half:(k + 1) * 2 * half].astype(BF16)
        ys.append(jnp.dot(s_blk, cc_ref[k], preferred_element_type=F32))
    y = jnp.concatenate(ys, axis=-1) + d_ref[...] * x
    act = _gelu_tanh(y).astype(BF16)
    z = jnp.dot(act, wglu_ref[...], preferred_element_type=F32)
    mix = z[:, :d_model] * jax.nn.sigmoid(z[:, d_model:])
    o_ref[...] = _layer_norm(ALPHA * x + mix, g_ref[...], b_ref[...])


def _s5_layer(x, bb, cc, a_re, a_im, d_skip, w_glu, ln_g, ln_b, *, n_time):
    n_batch, seq, d_model = x.shape
    n_rows = n_batch * seq
    n_blocks = bb.shape[0]
    rows = n_time * n_batch
    state_w = n_blocks * bb.shape[2]
    kern = functools.partial(_s5_kernel, n_batch=n_batch, n_time=n_time, n_blocks=n_blocks)
    return pl.pallas_call(
        kern,
        grid=(n_rows // rows,),
        in_specs=[
            pl.BlockSpec(memory_space=pl.ANY),
            _const_spec(bb.shape), _const_spec(cc.shape),
            _const_spec(a_re.shape), _const_spec(a_im.shape), _const_spec(d_skip.shape),
            _const_spec(w_glu.shape), _const_spec(ln_g.shape), _const_spec(ln_b.shape),
        ],
        out_specs=pl.BlockSpec((rows, d_model), lambda i: (i, 0)),
        out_shape=jax.ShapeDtypeStruct((n_rows, d_model), F32),
        scratch_shapes=[
            pltpu.VMEM((2, n_time, n_batch, d_model), F32),
            pltpu.VMEM((rows, state_w), F32), pltpu.VMEM((n_batch, state_w), F32),
            pltpu.SemaphoreType.DMA((2,)),
        ],
        compiler_params=pltpu.CompilerParams(
            dimension_semantics=("arbitrary",), vmem_limit_bytes=V7X_VMEM_LIMIT),
        name="s5_mixer",
    )(x, bb, cc, a_re, a_im, d_skip, w_glu, ln_g, ln_b)


def _ffn_kernel(x_ref, wg_ref, wu_ref, wo_ref, g_ref, b_ref, o_hbm, obuf_ref, sem, *, fc, n_batch):
    i = pl.program_id(0)
    last = pl.num_programs(0) - 1
    slot = i % 2
    n_time = x_ref.shape[0] // n_batch

    def out_copy(step, b, from_slot):
        return pltpu.make_async_copy(
            obuf_ref.at[from_slot, :, b, :], o_hbm.at[b, pl.ds(step * n_time, n_time), :],
            sem.at[from_slot])

    @pl.when(i >= 2)
    def _():
        for b in range(n_batch):
            out_copy(i - 2, b, slot).wait()

    x = x_ref[...]
    xb = x.astype(BF16)
    acc = None
    for lo, hi in zip(fc[:-1], fc[1:]):
        gate = jnp.dot(xb, wg_ref[:, lo:hi], preferred_element_type=F32)
        up = jnp.dot(xb, wu_ref[:, lo:hi], preferred_element_type=F32)
        h = (gate * jax.nn.sigmoid(gate) * up).astype(BF16)
        part = jnp.dot(h, wo_ref[lo:hi, :], preferred_element_type=F32)
        acc = part if acc is None else acc + part
    out = _layer_norm(ALPHA * x + acc, g_ref[...], b_ref[...])
    obuf_ref[slot] = out.reshape(n_time, n_batch, out.shape[1])
    for b in range(n_batch):
        out_copy(i, b, slot).start()

    @pl.when((i == last) & (i >= 1))
    def _():
        for b in range(n_batch):
            out_copy(i - 1, b, 1 - slot).wait()

    @pl.when(i == last)
    def _():
        for b in range(n_batch):
            out_copy(i, b, slot).wait()


def _dense_ffn(x, w_in, w_out, ln_g, ln_b, *, n_batch, n_time):
    n_rows, d_model = x.shape
    ffn = w_out.shape[0]
    tm = n_time * n_batch
    n_mxu_tiles, rem = divmod(ffn, V7X_MXU_WIDTH)
    assert rem == 0
    fc = (0, (n_mxu_tiles + 1) // 2 * V7X_MXU_WIDTH, ffn)
    kern = functools.partial(_ffn_kernel, fc=fc, n_batch=n_batch)
    return pl.pallas_call(
        kern,
        grid=(n_rows // tm,),
        in_specs=[
            pl.BlockSpec((tm, d_model), lambda i: (i, 0)),
            pl.BlockSpec((d_model, ffn), lambda i: (0, 0), pipeline_mode=pl.Buffered(1)),
            pl.BlockSpec((d_model, ffn), lambda i: (0, 1), pipeline_mode=pl.Buffered(1)),
            _const_spec(w_out.shape), _const_spec(ln_g.shape), _const_spec(ln_b.shape),
        ],
        out_specs=pl.BlockSpec(memory_space=pl.ANY),
        out_shape=jax.ShapeDtypeStruct((n_batch, n_rows // n_batch, d_model), F32),
        scratch_shapes=[pltpu.VMEM((2, n_time, n_batch, d_model), F32),
                        pltpu.SemaphoreType.DMA((2,))],
        compiler_params=pltpu.CompilerParams(
            dimension_semantics=("arbitrary",), vmem_limit_bytes=V7X_VMEM_LIMIT),
        name="dense_ffn",
    )(x, w_in, w_in, w_out, ln_g, ln_b)


def _qkv_kernel(x_ref, wkva_ref, kvn_ref, wkvb_ref, wqa_ref, qn_ref, wqb_ref, cos_ref, sin_ref,
                q_ref, k_ref, v_ref, *, scale):
    xb = x_ref[0].astype(BF16)
    cosx = cos_ref[...]
    sinx = sin_ref[...]

    kva = jnp.dot(xb, wkva_ref[...], preferred_element_type=F32)
    ckv = kva[:, :V7X_LANES]
    ckv = ckv * lax.rsqrt(jnp.mean(ckv * ckv, axis=-1, keepdims=True) + RMS_EPS) * kvn_ref[...]
    k_rope = (kva[:, V7X_LANES:2 * V7X_LANES] * cosx
              + kva[:, 2 * V7X_LANES:3 * V7X_LANES] * sinx).astype(BF16)
    kv = jnp.dot(ckv.astype(BF16), wkvb_ref[...], preferred_element_type=F32)

    cq = jnp.dot(xb, wqa_ref[...], preferred_element_type=F32)
    cq = cq * lax.rsqrt(jnp.mean(cq * cq, axis=-1, keepdims=True) + RMS_EPS) * qn_ref[...]
    q = jnp.dot(cq.astype(BF16), wqb_ref[...], preferred_element_type=F32)

    for h in range(MLA_HEADS):
        kb = h * (QK_NOPE + V_DIM)
        k_ref[0, h, :, :QK_NOPE] = kv[:, kb:kb + QK_NOPE].astype(BF16)
        k_ref[0, h, :, QK_NOPE:] = k_rope
        v_ref[0, h] = kv[:, kb + QK_NOPE:kb + QK_NOPE + V_DIM].astype(BF16)
        qb = h * 3 * V7X_LANES
        q_ref[0, h, :, :QK_NOPE] = (q[:, qb:qb + QK_NOPE] * scale).astype(BF16)
        q_rope = (q[:, qb + V7X_LANES:qb + 2 * V7X_LANES] * cosx
                  + q[:, qb + 2 * V7X_LANES:qb + 3 * V7X_LANES] * sinx)
        q_ref[0, h, :, QK_NOPE:] = (q_rope * scale).astype(BF16)


def _qkv_proj(x, wkva, kvn, wkvb, wqa, qn, wqb, cosx, sinx, *, ts):
    n_batch, seq, d_model = x.shape
    scale = 1.0 / math.sqrt(QK_NOPE + QK_ROPE)
    dk = QK_NOPE + V7X_LANES
    kern = functools.partial(_qkv_kernel, scale=scale)
    return pl.pallas_call(
        kern,
        grid=(n_batch, seq // ts),
        in_specs=[
            pl.BlockSpec((1, ts, d_model), lambda b, s: (b, s, 0)),
            _const_spec(wkva.shape), _const_spec(kvn.shape), _const_spec(wkvb.shape),
            _const_spec(wqa.shape), _const_spec(qn.shape), _const_spec(wqb.shape),
            pl.BlockSpec((ts, V7X_LANES), lambda b, s: (s, 0)),
            pl.BlockSpec((ts, V7X_LANES), lambda b, s: (s, 0)),
        ],
        out_specs=[
            pl.BlockSpec((1, MLA_HEADS, ts, dk), lambda b, s: (b, 0, s, 0)),
            pl.BlockSpec((1, MLA_HEADS, ts, dk), lambda b, s: (b, 0, s, 0)),
            pl.BlockSpec((1, MLA_HEADS, ts, V_DIM), lambda b, s: (b, 0, s, 0)),
        ],
        out_shape=[
            jax.ShapeDtypeStruct((n_batch, MLA_HEADS, seq, dk), BF16),
            jax.ShapeDtypeStruct((n_batch, MLA_HEADS, seq, dk), BF16),
            jax.ShapeDtypeStruct((n_batch, MLA_HEADS, seq, V_DIM), BF16),
        ],
        compiler_params=pltpu.CompilerParams(
            dimension_semantics=("parallel", "parallel"), vmem_limit_bytes=V7X_VMEM_LIMIT),
        name="qkv_proj",
    )(x, wkva, kvn, wkvb, wqa, qn, wqb, cosx, sinx)


def _attn_kernel(q_ref, k_ref, v_ref, w_ref, o_ref, wb_ref, *, tq):
    wb_ref[...] = w_ref[...].astype(BF16)
    seq = q_ref.shape[2]
    n_tiles = seq // tq
    row = lax.broadcasted_iota(jnp.int32, (tq, tq), 0)
    col = lax.broadcasted_iota(jnp.int32, (tq, tq), 1)
    causal = col <= row
    for qi in range(n_tiles):
        q = q_ref[0, 0, qi * tq:(qi + 1) * tq, :]
        m = l = acc = None
        for kj in range(qi + 1):
            k = k_ref[0, 0, kj * tq:(kj + 1) * tq, :]
            v = v_ref[0, 0, kj * tq:(kj + 1) * tq, :]
            s = lax.dot_general(q, k, (((1,), (1,)), ((), ())), preferred_element_type=F32)
            if kj == qi:
                s = jnp.where(causal, s, -jnp.inf)
            m_new = jnp.max(s, axis=-1, keepdims=True)
            if kj > 0:
                m_new = jnp.maximum(m, m_new)
            p = jnp.exp(s - m_new)
            pv = jnp.dot(p.astype(BF16), v, preferred_element_type=F32)
            if kj == 0:
                l = jnp.sum(p, axis=-1, keepdims=True)
                acc = pv
            else:
                corr = jnp.exp(m - m_new)
                l = corr * l + jnp.sum(p, axis=-1, keepdims=True)
                acc = corr * acc + pv
            m = m_new
        o_ref[0, qi * tq:(qi + 1) * tq, :] = (acc / l).astype(o_ref.dtype)


def _cast_slab_rows(w2d, n_steps):
    rows = w2d.shape[0] // n_steps
    assert rows * n_steps == w2d.shape[0] and rows % (2 * V7X_SUBLANES) == 0
    return rows


def _attention(q, k, v, w_cast, *, tq):
    n_batch, n_heads, seq, dk = q.shape
    dv = v.shape[3]
    w_rows = _cast_slab_rows(w_cast, n_batch * n_heads)
    w_spec = pl.BlockSpec((w_rows, w_cast.shape[1]), lambda b, h: (b * n_heads + h, 0))
    kern = functools.partial(_attn_kernel, tq=tq)
    return pl.pallas_call(
        kern,
        grid=(n_batch, n_heads),
        in_specs=[
            pl.BlockSpec((1, 1, seq, dk), lambda b, h: (b, h, 0, 0)),
            pl.BlockSpec((1, 1, seq, dk), lambda b, h: (b, h, 0, 0)),
            pl.BlockSpec((1, 1, seq, dv), lambda b, h: (b, h, 0, 0)),
            w_spec,
        ],
        out_specs=[pl.BlockSpec((1, seq, dv), lambda b, h: (b, 0, h)), w_spec],
        out_shape=[jax.ShapeDtypeStruct((n_batch, seq, n_heads * dv), BF16),
                   jax.ShapeDtypeStruct(w_cast.shape, BF16)],
        compiler_params=pltpu.CompilerParams(
            dimension_semantics=("parallel", "parallel"), vmem_limit_bytes=V7X_VMEM_LIMIT),
        name="mla_attention",
    )(q, k, v, w_cast)


def _oproj_router_kernel(o_ref, x_ref, wo_ref, g_ref, b_ref, wrh_ref, wrl_ref, w_ref,
                         y_ref, info_ref, wb_ref):
    wb_ref[...] = w_ref[...].astype(BF16)
    half = info_ref.shape[0] // 2
    for r in range(2):
        rows = pl.ds(r * half, half)
        y, info = _oproj_router_rows(o_ref[0, rows, :], x_ref[0, rows, :], wo_ref, g_ref, b_ref,
                                     wrh_ref, wrl_ref)
        for c in range(V7X_SUBLANES):
            y_ref[pl.ds(r * half * V7X_SUBLANES + c, half, stride=V7X_SUBLANES), :] = (
                y[:, c * V7X_LANES:(c + 1) * V7X_LANES])
        info_ref[rows, :] = info


def _oproj_router_rows(o, x, wo_ref, g_ref, b_ref, wrh_ref, wrl_ref):
    mix = jnp.dot(o, wo_ref[...], preferred_element_type=F32)
    y = _layer_norm(ALPHA * x + mix, g_ref[...], b_ref[...])

    y_hi = y.astype(BF16)
    y_lo = (y - y_hi.astype(F32)).astype(BF16)
    logits = (jnp.dot(y_hi, wrh_ref[...], preferred_element_type=F32)
              + jnp.dot(y_lo, wrh_ref[...], preferred_element_type=F32)
              + jnp.dot(y_hi, wrl_ref[...], preferred_element_type=F32))
    lane = lax.broadcasted_iota(jnp.int32, logits.shape, 1)
    logits = jnp.where(lane < N_EXPERTS, logits, -jnp.inf)
    m1 = jnp.max(logits, axis=-1, keepdims=True)
    i1 = jnp.min(jnp.where(logits == m1, lane, V7X_LANES), axis=-1, keepdims=True)
    rest = jnp.where(lane == i1, -jnp.inf, logits)
    m2 = jnp.max(rest, axis=-1, keepdims=True)
    i2 = jnp.min(jnp.where(rest == m2, lane, V7X_LANES), axis=-1, keepdims=True)
    e2 = jnp.exp(m2 - m1)
    den = 1.0 + e2
    info = jnp.where(lane == 0, i1.astype(F32),
                     jnp.where(lane == 1, i2.astype(F32),
                               jnp.where(lane == 2, 1.0 / den,
                                         jnp.where(lane == 3, e2 / den, 0.0))))
    return y, info


def _oproj_router(o, x, wo, ln_g, ln_b, wr_hi, wr_lo, w_cast, *, ts):
    n_batch, seq, d_attn = o.shape
    d_model = x.shape[2]
    assert d_model == V7X_SUBLANES * V7X_LANES
    nst = seq // ts
    w_rows = _cast_slab_rows(w_cast, n_batch * nst)
    w_spec = pl.BlockSpec((w_rows, w_cast.shape[1]), lambda b, s: (b * nst + s, 0))
    return pl.pallas_call(
        _oproj_router_kernel,
        grid=(n_batch, nst),
        in_specs=[
            pl.BlockSpec((1, ts, d_attn), lambda b, s: (b, s, 0)),
            pl.BlockSpec((1, ts, d_model), lambda b, s: (b, s, 0)),
            _const_spec(wo.shape), _const_spec(ln_g.shape), _const_spec(ln_b.shape),
            _const_spec(wr_hi.shape), _const_spec(wr_lo.shape),
            w_spec,
        ],
        out_specs=[
            pl.BlockSpec((ts * V7X_SUBLANES, V7X_LANES), lambda b, s: (b * nst + s, 0)),
            pl.BlockSpec((ts, V7X_LANES), lambda b, s: (b * nst + s, 0)),
            w_spec,
        ],
        out_shape=[
            jax.ShapeDtypeStruct((n_batch * seq * V7X_SUBLANES, V7X_LANES), F32),
            jax.ShapeDtypeStruct((n_batch * seq, V7X_LANES), F32),
            jax.ShapeDtypeStruct(w_cast.shape, BF16),
        ],
        compiler_params=pltpu.CompilerParams(
            dimension_semantics=("parallel", "parallel"), vmem_limit_bytes=V7X_VMEM_LIMIT),
        name="oproj_router",
    )(o, x, wo, ln_g, ln_b, wr_hi, wr_lo, w_cast)


def _moe_kernel(te_ref, nu_ref, tok_ref, dst_ref, x_hbm, wg_ref, wu_ref, wo_ref, y_hbm,
                xs_ref, ob_ref, xb_ref, acc_ref, gsem, ssem, *, tm, nf):
    del te_ref
    s = pl.program_id(0)
    f = pl.program_id(1)
    gslot = s % 2
    cslot = 1 - gslot
    valid = (s >= 1) & (s <= nu_ref[0])
    sub, lanes = V7X_SUBLANES, V7X_LANES

    def wait_tile_rows(buf_ref, sem):
        slot_rows = tm * sub
        pltpu.make_async_copy(
            buf_ref.at[pl.ds(0, slot_rows)], buf_ref.at[pl.ds(slot_rows, slot_rows)], sem).wait()

    @pl.when((s == 0) & (f == 0))
    def _():
        ob_ref[...] = jnp.zeros_like(ob_ref)

    @pl.when((s >= 1) & (f == 0))
    def _():
        wait_tile_rows(xs_ref, gsem)

    for parity in range(2):
        @pl.when((f == 0) & (gslot == parity))
        def _(parity=parity):
            for row in range(tm):
                vmem_rows = pl.ds((parity * tm + row) * sub, sub)
                pltpu.make_async_copy(
                    x_hbm.at[tok_ref[0, 0, row]], xs_ref.at[vmem_rows], gsem).start()

        @pl.when((f == nf - 1) & (gslot == parity))
        def _(parity=parity):
            for row in range(tm):
                vmem_rows = pl.ds((parity * tm + row) * sub, sub)
                pltpu.make_async_copy(
                    ob_ref.at[vmem_rows], y_hbm.at[dst_ref[0, 0, row]], ssem).start()

    def gathered_rows_bf16():
        return jnp.concatenate(
            [xs_ref[pl.ds(cslot * tm * sub + c, tm, stride=sub), :].astype(BF16)
             for c in range(sub)], axis=-1)

    def store_result(res):
        for c in range(sub):
            ob_ref[pl.ds(cslot * tm * sub + c, tm, stride=sub), :] = (
                res[:, c * lanes:(c + 1) * lanes])

    for chunk in range(nf):
        @pl.when(valid & (f == chunk))
        def _(chunk=chunk):
            if chunk == 0:
                xb = gathered_rows_bf16()
                if nf > 1:
                    xb_ref[...] = xb
            else:
                xb = xb_ref[...]
            gate = jnp.dot(xb, wg_ref[0], preferred_element_type=F32)
            up = jnp.dot(xb, wu_ref[0], preferred_element_type=F32)
            h = (gate * jax.nn.sigmoid(gate) * up).astype(BF16)
            part = jnp.dot(h, wo_ref[0], preferred_element_type=F32)
            if nf == 1:
                store_result(part)
            elif chunk == 0:
                acc_ref[...] = part
            elif chunk < nf - 1:
                acc_ref[...] += part
            else:
                store_result(acc_ref[...] + part)

    @pl.when(f == nf - 1)
    def _():
        wait_tile_rows(ob_ref, ssem)

    @pl.when((s == pl.num_programs(0) - 1) & (f == nf - 1))
    def _():
        wait_tile_rows(xs_ref, gsem)


def _moe_experts(tile_expert, n_used, tok_idx, dst_idx, x_tiles, w_in, w_out, *, tm, fc):
    d_model = x_tiles.shape[1] * x_tiles.shape[2]
    edim = w_out.shape[1]
    nf = edim // fc
    n_tiles = tok_idx.shape[0]
    n_rows = n_tiles * tm
    n_steps = n_tiles + 2

    def expert_chunk(s, f, te, nu):
        tile = jnp.clip(s - 1, 0, nu[0] - 1)
        chunk = jnp.where(s > nu[0], nf - 1, jnp.where(s < 1, 0, f))
        return te[tile], chunk

    def w_in_map(half):
        def index(s, f, te, nu):
            e, chunk = expert_chunk(s, f, te, nu)
            return e, 0, chunk + half * nf
        return index

    def w_out_map(s, f, te, nu):
        e, chunk = expert_chunk(s, f, te, nu)
        return e, chunk, 0

    grid_spec = pltpu.PrefetchScalarGridSpec(
        num_scalar_prefetch=2,
        grid=(n_steps, nf),
        in_specs=[
            pl.BlockSpec((1, 1, tm), lambda s, f, te, nu: (jnp.minimum(s, n_tiles - 1), 0, 0),
                         memory_space=pltpu.SMEM),
            pl.BlockSpec((1, 1, tm), lambda s, f, te, nu: (s, 0, 0), memory_space=pltpu.SMEM),
            pl.BlockSpec(memory_space=pl.ANY),
            pl.BlockSpec((1, d_model, fc), w_in_map(0)),
            pl.BlockSpec((1, d_model, fc), w_in_map(1)),
            pl.BlockSpec((1, fc, d_model), w_out_map),
        ],
        out_specs=pl.BlockSpec(memory_space=pl.ANY),
        scratch_shapes=[
            pltpu.VMEM((2 * tm * V7X_SUBLANES, V7X_LANES), F32),
            pltpu.VMEM((2 * tm * V7X_SUBLANES, V7X_LANES), F32),
            pltpu.VMEM((tm, d_model), BF16), pltpu.VMEM((tm, d_model), F32),
            pltpu.SemaphoreType.DMA(()), pltpu.SemaphoreType.DMA(()),
        ],
    )
    warm = n_rows + jnp.arange(tm, dtype=jnp.int32)[None, :]
    dst_by_step = jnp.concatenate([warm, warm, dst_idx], axis=0)
    kern = functools.partial(_moe_kernel, tm=tm, nf=nf)
    return pl.pallas_call(
        kern,
        grid_spec=grid_spec,
        out_shape=jax.ShapeDtypeStruct((n_rows + tm, V7X_SUBLANES, V7X_LANES), F32),
        compiler_params=pltpu.CompilerParams(
            dimension_semantics=("arbitrary", "arbitrary"), vmem_limit_bytes=V7X_VMEM_LIMIT),
        name="moe_experts",
    )(tile_expert, n_used, tok_idx.reshape(n_tiles, 1, tm), dst_by_step.reshape(n_steps, 1, tm),
      x_tiles, w_in, w_in, w_out)


def _combine_kernel(x_ref, info_ref, y0_ref, y1_ref, g_ref, b_ref, o_ref):
    tc, d_model = o_ref.shape
    sub, lanes = V7X_SUBLANES, V7X_LANES
    info = info_ref[...]
    g0, g1 = info[:, 2:3], info[:, 3:4]
    pieces = []
    for c in range(sub):
        rows = pl.ds(c, tc, stride=sub)
        pieces.append(ALPHA * x_ref[rows, :] + (y0_ref[rows, :] * g0 + y1_ref[rows, :] * g1))
    mu = jnp.sum(sum(pieces), axis=-1, keepdims=True) / d_model
    cen = [p - mu for p in pieces]
    var = jnp.sum(sum(c * c for c in cen), axis=-1, keepdims=True) / d_model
    rstd = lax.rsqrt(var + LN_EPS)
    for c in range(sub):
        cols = slice(c * lanes, (c + 1) * lanes)
        o_ref[:, cols] = cen[c] * rstd * g_ref[:, cols] + b_ref[:, cols]


def _combine(x_tiles, info, y_tiles, ln_g, ln_b, *, tc):
    n_tok = info.shape[0]
    d_model = V7X_SUBLANES * V7X_LANES
    nt = n_tok // tc
    tile_block = (tc * V7X_SUBLANES, V7X_LANES)
    return pl.pallas_call(
        _combine_kernel,
        grid=(nt,),
        in_specs=[
            pl.BlockSpec(tile_block, lambda i: (i, 0)),
            pl.BlockSpec((tc, V7X_LANES), lambda i: (i, 0)),
            pl.BlockSpec(tile_block, lambda i: (i, 0)),
            pl.BlockSpec(tile_block, lambda i: (i + nt, 0)),
            _const_spec(ln_g.shape), _const_spec(ln_b.shape),
        ],
        out_specs=pl.BlockSpec((tc, d_model), lambda i: (i, 0)),
        out_shape=jax.ShapeDtypeStruct((n_tok, d_model), F32),
        compiler_params=pltpu.CompilerParams(
            dimension_semantics=("parallel",), vmem_limit_bytes=V7X_VMEM_LIMIT),
        name="moe_combine",
    )(x_tiles, info, y_tiles, y_tiles, ln_g, ln_b)


def _s5_params(lam_re, lam_im, log_step, b_re, b_im, c_re, c_im):
    n_groups, n_state = lam_re.shape
    dt = jnp.exp(log_step)[:, None]
    mag = jnp.exp(lam_re * dt)
    lb_re = mag * jnp.cos(lam_im * dt)
    lb_im = mag * jnp.sin(lam_im * dt)
    den = lam_re * lam_re + lam_im * lam_im
    f_re = ((lb_re - 1.0) * lam_re + lb_im * lam_im) / den
    f_im = (lb_im * lam_re - (lb_re - 1.0) * lam_im) / den
    bb_re = f_re[..., None] * b_re - f_im[..., None] * b_im
    bb_im = f_re[..., None] * b_im + f_im[..., None] * b_re
    gpb = S5_BLOCK_GROUPS
    n_blocks = n_groups // gpb
    eye = jnp.eye(gpb, dtype=F32)

    def in_blocks(w):
        w = w.transpose(0, 2, 1).reshape(n_blocks, gpb, S5_GROUP, n_state)
        return jnp.einsum('kgcp,gh->kgchp', w, eye).reshape(n_blocks, gpb * S5_GROUP, gpb * n_state)

    def out_blocks(w):
        w = w.reshape(n_blocks, gpb, S5_GROUP, n_state)
        return jnp.einsum('kgcp,gh->kgphc', w, eye).reshape(n_blocks, gpb * n_state, gpb * S5_GROUP)

    bb = jnp.concatenate([in_blocks(bb_re), in_blocks(bb_im)], axis=-1).astype(BF16)
    cc = jnp.concatenate([out_blocks(c_re), out_blocks(-c_im)], axis=1).astype(BF16)
    return bb, cc, lb_re.reshape(1, -1), lb_im.reshape(1, -1)


def _rope_lane_tables(seq):
    pos = jnp.arange(seq, dtype=F32)
    inv_freq = ROPE_THETA ** (-jnp.arange(0, QK_ROPE, 2, dtype=F32) / QK_ROPE)
    ang = pos[:, None] * inv_freq[None, :]
    cos, sin = jnp.cos(ang), jnp.sin(ang)
    pad = jnp.zeros((seq, V7X_LANES - QK_ROPE), F32)
    return (jnp.concatenate([cos, cos, pad], axis=-1),
            jnp.concatenate([-sin, sin, pad], axis=-1))


def _rope_weight_blocks(w_rope):
    half = QK_ROPE // 2
    t1, t2 = w_rope[:, :half], w_rope[:, half:]
    pad = jnp.zeros((w_rope.shape[0], V7X_LANES - QK_ROPE), w_rope.dtype)
    return jnp.concatenate([t1, t2, pad, t2, t1, pad], axis=-1)


def _routing(info, *, tm):
    n_tok = info.shape[0]
    n_assign = n_tok * TOP_K
    flat_e = info[:, :TOP_K].astype(jnp.int32).reshape(-1)
    experts = jnp.arange(N_EXPERTS, dtype=jnp.int32)
    counts = jnp.sum((flat_e[:, None] == experts[None, :]).astype(jnp.int32), axis=0)
    pad = (-counts) % tm
    filler = jnp.arange(N_EXPERTS * tm, dtype=jnp.int32)
    filler_key = jnp.where(filler % tm < pad[filler // tm], filler // tm, N_EXPERTS)
    keys = jnp.concatenate([flat_e, filler_key])
    ids = jnp.arange(keys.shape[0], dtype=jnp.int32)
    sorted_keys, order = lax.sort((keys, ids), num_keys=1, is_stable=True)
    is_real = order < n_assign
    tok_idx = jnp.where(is_real, order // TOP_K, 0)
    dst_idx = jnp.where(is_real, (order % TOP_K) * n_tok + order // TOP_K, order)
    tile_key = sorted_keys[::tm]
    tile_expert = jnp.minimum(tile_key, N_EXPERTS - 1)
    n_used = jnp.sum((tile_key < N_EXPERTS).astype(jnp.int32)).reshape(1)
    return tok_idx.reshape(-1, tm), dst_idx.reshape(-1, tm), tile_expert, n_used


def kernel(x, s5_lam_re, s5_lam_im, s5_log_step, s5_b_re, s5_b_im, s5_c_re, s5_c_im, s5_d, s5_w_glu,
           mla_q_w_a, mla_q_norm, mla_q_w_b, mla_o_w, kv_w_a, kv_norm, kv_w_b, ffn_w_in, ffn_w_out,
           moe_router, moe_w_in, moe_w_out, ln_g, ln_b):
    n_batch, seq, d_model = x.shape
    n_tok = n_batch * seq
    row = lambda v: v.reshape(1, -1).astype(F32)

    bb, cc, a_re, a_im = _s5_params(s5_lam_re[0], s5_lam_im[0], s5_log_step[0], s5_b_re[0], s5_b_im[0],
                                    s5_c_re[0], s5_c_im[0])
    n_time = min(S5_TIME_CHUNK, seq)
    x1 = _s5_layer(x, bb, cc, a_re, a_im, row(s5_d[0]), s5_w_glu[0].astype(BF16),
                   row(ln_g[0, 0]), row(ln_b[0, 0]), n_time=n_time)
    x2 = _dense_ffn(x1, ffn_w_in[0].astype(BF16), ffn_w_out[0].astype(BF16),
                    row(ln_g[0, 1]), row(ln_b[0, 1]), n_batch=n_batch, n_time=n_time)

    kv_lora = kv_norm.shape[0]
    wkva = jnp.concatenate([kv_w_a[:, :kv_lora], _rope_weight_blocks(kv_w_a[:, kv_lora:])],
                           axis=-1).astype(BF16)
    wq = mla_q_w_b[0].reshape(-1, MLA_HEADS, QK_NOPE + QK_ROPE)
    wqb = jnp.concatenate(
        [jnp.concatenate([wq[:, h, :QK_NOPE], _rope_weight_blocks(wq[:, h, QK_NOPE:])], axis=-1)
         for h in range(MLA_HEADS)], axis=-1).astype(BF16)
    cosx, sinx = _rope_lane_tables(seq)
    ts = min(ROW_TILE, seq)
    q, k, v = _qkv_proj(x2, wkva, row(kv_norm), kv_w_b.astype(BF16), mla_q_w_a[0].astype(BF16),
                        row(mla_q_norm[0]), wqb, cosx, sinx, ts=ts)
    w_in, w_out = moe_w_in[0], moe_w_out[0]
    o, w_in_bf = _attention(q, k, v, w_in.reshape(-1, w_in.shape[2]), tq=ts)
    wr = jnp.pad(moe_router[0], ((0, 0), (0, V7X_LANES - N_EXPERTS)))
    wr_hi = wr.astype(BF16)
    wr_lo = (wr - wr_hi.astype(F32)).astype(BF16)
    x3, info, w_out_bf = _oproj_router(o, x2, mla_o_w[0].astype(BF16), row(ln_g[1, 0]),
                                       row(ln_b[1, 0]), wr_hi, wr_lo,
                                       w_out.reshape(-1, w_out.shape[2]), ts=ts)

    tok_idx, dst_idx, tile_expert, n_used = _routing(info, tm=ROW_TILE)
    ys = _moe_experts(tile_expert, n_used, tok_idx, dst_idx,
                      x3.reshape(n_tok, V7X_SUBLANES, V7X_LANES), w_in_bf.reshape(w_in.shape),
                      w_out_bf.reshape(w_out.shape), tm=ROW_TILE, fc=w_out.shape[1] // 2)
    out = _combine(x3, info, ys.reshape(-1, V7X_LANES), row(ln_g[1, 1]), row(ln_b[1, 1]),
                   tc=min(ROW_TILE, n_tok))
    return out.reshape(n_batch, seq, d_model)
```

```python
import functools
import math

import jax
import jax.numpy as jnp
from jax import lax
from jax.experimental import pallas as pl
from jax.experimental.pallas import tpu as pltpu

F32 = jnp.float32
BF16 = jnp.bfloat16

V7X_LANES = 128
V7X_SUBLANES = 8
V7X_MXU_WIDTH = 256
V7X_VMEM_LIMIT = 56 * 1024 * 1024

DEPTH = 2
ALPHA = (2.0 * DEPTH) ** 0.25
LN_EPS = 1e-5
RMS_EPS = 1e-6
ROPE_THETA = 10000.0

S5_GROUP = 16
S5_STATE = 64
S5_BLOCK_GROUPS = 16

MLA_HEADS = 8
QK_NOPE = 128
QK_ROPE = 64
V_DIM = 128
N_EXPERTS = 8
TOP_K = 2

ROW_TILE = 512
S5_TIME_CHUNK = 32


def _const_spec(shape):
    zeros = (0,) * len(shape)
    return pl.BlockSpec(shape, lambda *_: zeros, pipeline_mode=pl.Buffered(1))


def _layer_norm(h, g, b):
    mu = jnp.mean(h, axis=-1, keepdims=True)
    c = h - mu
    var = jnp.mean(c * c, axis=-1, keepdims=True)
    return c * lax.rsqrt(var + LN_EPS) * g + b


def _gelu_tanh(y):
    return 0.5 * y * (1.0 + jnp.tanh(math.sqrt(2.0 / math.pi) * (y + 0.044715 * (y * y * y))))


def _s5_kernel(x_hbm, bb_ref, cc_ref, are_ref, aim_ref, d_ref, wglu_ref, g_ref, b_ref,
               o_ref, xin_ref, bu_ref, st_ref, sem, *, n_batch, n_time, n_blocks):
    half = bb_ref.shape[2] // 2
    cb = bb_ref.shape[1]
    slab = 512
    d_model = o_ref.shape[1]
    i = pl.program_id(0)
    slot = i % 2

    def chunk_copy(step, b, to_slot):
        return pltpu.make_async_copy(
            x_hbm.at[b, pl.ds(step * n_time, n_time), :], xin_ref.at[to_slot, :, b, :],
            sem.at[to_slot])

    @pl.when(i == 0)
    def _():
        st_ref[...] = jnp.zeros_like(st_ref)
        for b in range(n_batch):
            chunk_copy(0, b, 0).start()

    @pl.when(i + 1 < pl.num_programs(0))
    def _():
        for b in range(n_batch):
            chunk_copy(i + 1, b, 1 - slot).start()

    for b in range(n_batch):
        chunk_copy(i, b, slot).wait()

    x = xin_ref[slot].reshape(n_time * n_batch, d_model)
    xb = x.astype(BF16)
    for k in range(n_blocks):
        bu_ref[:, k * 2 * half:(k + 1) * 2 * half] = jnp.dot(
            xb[:, k * cb:(k + 1) * cb], bb_ref[k], preferred_element_type=F32)

    for k in range(n_blocks):
        for j in range(half // slab):
            re0 = k * 2 * half + j * slab
            im0 = re0 + half
            a0 = k * half + j * slab
            for bh in range(n_batch // V7X_SUBLANES):
                r0 = bh * V7X_SUBLANES
                ar = jnp.broadcast_to(are_ref[:, a0:a0 + slab], (V7X_SUBLANES, slab))
                ai = jnp.broadcast_to(aim_ref[:, a0:a0 + slab], (V7X_SUBLANES, slab))
                s_re = st_ref[r0:r0 + V7X_SUBLANES, re0:re0 + slab]
                s_im = st_ref[r0:r0 + V7X_SUBLANES, im0:im0 + slab]

                for t in range(n_time):
                    rows = pl.ds(t * n_batch + r0, V7X_SUBLANES)
                    b_re = bu_ref[rows, re0:re0 + slab]
                    b_im = bu_ref[rows, im0:im0 + slab]
                    s_re, s_im = (ar * s_re - ai * s_im + b_re, ar * s_im + ai * s_re + b_im)
                    bu_ref[rows, re0:re0 + slab] = s_re
                    bu_ref[rows, im0:im0 + slab] = s_im
                st_ref[r0:r0 + V7X_SUBLANES, re0:re0 + slab] = s_re
                st_ref[r0:r0 + V7X_SUBLANES, im0:im0 + slab] = s_im

    ys = []
    for k in range(n_blocks):
        s_blk = bu_ref[:, k * 2 * half:(k + 1) * 2 * half].astype(BF16)
        ys.append(jnp.dot(s_blk, cc_ref[k], preferred_element_type=F32))
    y = jnp.concatenate(ys, axis=-1) + d_ref[...] * x
    act = _gelu_tanh(y).astype(BF16)
    z = jnp.dot(act, wglu_ref[...], preferred_element_type=F32)
    mix = z[:, :d_model] * jax.nn.sigmoid(z[:, d_model:])
    o_ref[...] = _layer_norm(ALPHA * x + mix, g_ref[...], b_ref[...])


def _s5_layer(x, bb, cc, a_re, a_im, d_skip, w_glu, ln_g, ln_b, *, n_time):
    n_batch, seq, d_model = x.shape
    n_rows = n_batch * seq
    n_blocks = bb.shape[0]
    rows = n_time * n_batch
    state_w = n_blocks * bb.shape[2]
    kern = functools.partial(_s5_kernel, n_batch=n_batch, n_time=n_time, n_blocks=n_blocks)
    return pl.pallas_call(
        kern,
        grid=(n_rows // rows,),
        in_specs=[
            pl.BlockSpec(memory_space=pl.ANY),
            _const_spec(bb.shape), _const_spec(cc.shape),
            _const_spec(a_re.shape), _const_spec(a_im.shape), _const_spec(d_skip.shape),
            _const_spec(w_glu.shape), _const_spec(ln_g.shape), _const_spec(ln_b.shape),
        ],
        out_specs=pl.BlockSpec((rows, d_model), lambda i: (i, 0)),
        out_shape=jax.ShapeDtypeStruct((n_rows, d_model), F32),
        scratch_shapes=[
            pltpu.VMEM((2, n_time, n_batch, d_model), F32),
            pltpu.VMEM((rows, state_w), F32), pltpu.VMEM((n_batch, state_w), F32),
            pltpu.SemaphoreType.DMA((2,)),
        ],
        compiler_params=pltpu.CompilerParams(
            dimension_semantics=("arbitrary",), vmem_limit_bytes=V7X_VMEM_LIMIT),
        name="s5_mixer",
    )(x, bb, cc, a_re, a_im, d_skip, w_glu, ln_g, ln_b)


def _ffn_kernel(x_ref, wg_ref, wu_ref, wo_ref, g_ref, b_ref, o_hbm, obuf_ref, sem, *, fc, n_batch):
    i = pl.program_id(0)
    last = pl.num_programs(0) - 1
    slot = i % 2
    n_time = x_ref.shape[0] // n_batch

    def out_copy(step, b, from_slot):
        return pltpu.make_async_copy(
            obuf_ref.at[from_slot, :, b, :], o_hbm.at[b, pl.ds(step * n_time, n_time), :],
            sem.at[from_slot])

    @pl.when(i >= 2)
    def _():
        for b in range(n_batch):
            out_copy(i - 2, b, slot).wait()

    x = x_ref[...]
    xb = x.astype(BF16)
    acc = None
    for lo, hi in zip(fc[:-1], fc[1:]):
        gate = jnp.dot(xb, wg_ref[:, lo:hi], preferred_element_type=F32)
        up = jnp.dot(xb, wu_ref[:, lo:hi], preferred_element_type=F32)
        h = (gate * jax.nn.sigmoid(gate) * up).astype(BF16)
        part = jnp.dot(h, wo_ref[lo:hi, :], preferred_element_type=F32)
        acc = part if acc is None else acc + part
    out = _layer_norm(ALPHA * x + acc, g_ref[...], b_ref[...])
    obuf_ref[slot] = out.reshape(n_time, n_batch, out.shape[1])
    for b in range(n_batch):
        out_copy(i, b, slot).start()

    @pl.when((i == last) & (i >= 1))
    def _():
        for b in range(n_batch):
            out_copy(i - 1, b, 1 - slot).wait()

    @pl.when(i == last)
    def _():
        for b in range(n_batch):
            out_copy(i, b, slot).wait()


def _dense_ffn(x, w_in, w_out, ln_g, ln_b, *, n_batch, n_time):
    n_rows, d_model = x.shape
    ffn = w_out.shape[0]
    tm = n_time * n_batch
    n_mxu_tiles, rem = divmod(ffn, V7X_MXU_WIDTH)
    assert rem == 0
    fc = (0, (n_mxu_tiles + 1) // 2 * V7X_MXU_WIDTH, ffn)
    kern = functools.partial(_ffn_kernel, fc=fc, n_batch=n_batch)
    return pl.pallas_call(
        kern,
        grid=(n_rows // tm,),
        in_specs=[
            pl.BlockSpec((tm, d_model), lambda i: (i, 0)),
            pl.BlockSpec((d_model, ffn), lambda i: (0, 0), pipeline_mode=pl.Buffered(1)),
            pl.BlockSpec((d_model, ffn), lambda i: (0, 1), pipeline_mode=pl.Buffered(1)),
            _const_spec(w_out.shape), _const_spec(ln_g.shape), _const_spec(ln_b.shape),
        ],
        out_specs=pl.BlockSpec(memory_space=pl.ANY),
        out_shape=jax.ShapeDtypeStruct((n_batch, n_rows // n_batch, d_model), F32),
        scratch_shapes=[pltpu.VMEM((2, n_time, n_batch, d_model), F32),
                        pltpu.SemaphoreType.DMA((2,))],
        compiler_params=pltpu.CompilerParams(
            dimension_semantics=("arbitrary",), vmem_limit_bytes=V7X_VMEM_LIMIT),
        name="dense_ffn",
    )(x, w_in, w_in, w_out, ln_g, ln_b)


def _qkv_kernel(x_ref, wkva_ref, kvn_ref, wkvb_ref, wqa_ref, qn_ref, wqb_ref, cos_ref, sin_ref,
                q_ref, k_ref, v_ref, *, scale):
    xb = x_ref[0].astype(BF16)
    cosx = cos_ref[...]
    sinx = sin_ref[...]

    kva = jnp.dot(xb, wkva_ref[...], preferred_element_type=F32)
    ckv = kva[:, :V7X_LANES]
    ckv = ckv * lax.rsqrt(jnp.mean(ckv * ckv, axis=-1, keepdims=True) + RMS_EPS) * kvn_ref[...]
    k_rope = (kva[:, V7X_LANES:2 * V7X_LANES] * cosx
              + kva[:, 2 * V7X_LANES:3 * V7X_LANES] * sinx).astype(BF16)
    kv = jnp.dot(ckv.astype(BF16), wkvb_ref[...], preferred_element_type=F32)

    cq = jnp.dot(xb, wqa_ref[...], preferred_element_type=F32)
    cq = cq * lax.rsqrt(jnp.mean(cq * cq, axis=-1, keepdims=True) + RMS_EPS) * qn_ref[...]
    q = jnp.dot(cq.astype(BF16), wqb_ref[...], preferred_element_type=F32)

    for h in range(MLA_HEADS):
        kb = h * (QK_NOPE + V_DIM)
        k_ref[0, h, :, :QK_NOPE] = kv[:, kb:kb + QK_NOPE].astype(BF16)
        k_ref[0, h, :, QK_NOPE:] = k_rope
        v_ref[0, h] = kv[:, kb + QK_NOPE:kb + QK_NOPE + V_DIM].astype(BF16)
        qb = h * 3 * V7X_LANES
        q_ref[0, h, :, :QK_NOPE] = (q[:, qb:qb + QK_NOPE] * scale).astype(BF16)
        q_rope = (q[:, qb + V7X_LANES:qb + 2 * V7X_LANES] * cosx
                  + q[:, qb + 2 * V7X_LANES:qb + 3 * V7X_LANES] * sinx)
        q_ref[0, h, :, QK_NOPE:] = (q_rope * scale).astype(BF16)


def _qkv_proj(x, wkva, kvn, wkvb, wqa, qn, wqb, cosx, sinx, *, ts):
    n_batch, seq, d_model = x.shape
    scale = 1.0 / math.sqrt(QK_NOPE + QK_ROPE)
    dk = QK_NOPE + V7X_LANES
    kern = functools.partial(_qkv_kernel, scale=scale)
    return pl.pallas_call(
        kern,
        grid=(n_batch, seq // ts),
        in_specs=[
            pl.BlockSpec((1, ts, d_model), lambda b, s: (b, s, 0)),
            _const_spec(wkva.shape), _const_spec(kvn.shape), _const_spec(wkvb.shape),
            _const_spec(wqa.shape), _const_spec(qn.shape), _const_spec(wqb.shape),
            pl.BlockSpec((ts, V7X_LANES), lambda b, s: (s, 0)),
            pl.BlockSpec((ts, V7X_LANES), lambda b, s: (s, 0)),
        ],
        out_specs=[
            pl.BlockSpec((1, MLA_HEADS, ts, dk), lambda b, s: (b, 0, s, 0)),
            pl.BlockSpec((1, MLA_HEADS, ts, dk), lambda b, s: (b, 0, s, 0)),
            pl.BlockSpec((1, MLA_HEADS, ts, V_DIM), lambda b, s: (b, 0, s, 0)),
        ],
        out_shape=[
            jax.ShapeDtypeStruct((n_batch, MLA_HEADS, seq, dk), BF16),
            jax.ShapeDtypeStruct((n_batch, MLA_HEADS, seq, dk), BF16),
            jax.ShapeDtypeStruct((n_batch, MLA_HEADS, seq, V_DIM), BF16),
        ],
        compiler_params=pltpu.CompilerParams(
            dimension_semantics=("parallel", "parallel"), vmem_limit_bytes=V7X_VMEM_LIMIT),
        name="qkv_proj",
    )(x, wkva, kvn, wkvb, wqa, qn, wqb, cosx, sinx)


def _attn_kernel(q_ref, k_ref, v_ref, w_ref, o_ref, wb_ref, *, tq):
    wb_ref[...] = w_ref[...].astype(BF16)
    seq = q_ref.shape[2]
    n_tiles = seq // tq
    row = lax.broadcasted_iota(jnp.int32, (tq, tq), 0)
    col = lax.broadcasted_iota(jnp.int32, (tq, tq), 1)
    causal = col <= row
    for qi in range(n_tiles):
        q = q_ref[0, 0, qi * tq:(qi + 1) * tq, :]
        m = l = acc = None
        for kj in range(qi + 1):
            k = k_ref[0, 0, kj * tq:(kj + 1) * tq, :]
            v = v_ref[0, 0, kj * tq:(kj + 1) * tq, :]
            s = lax.dot_general(q, k, (((1,), (1,)), ((), ())), preferred_element_type=F32)
            if kj == qi:
                s = jnp.where(causal, s, -jnp.inf)
            m_new = jnp.max(s, axis=-1, keepdims=True)
            if kj > 0:
                m_new = jnp.maximum(m, m_new)
            p = jnp.exp(s - m_new)
            pv = jnp.dot(p.astype(BF16), v, preferred_element_type=F32)
            if kj == 0:
                l = jnp.sum(p, axis=-1, keepdims=True)
                acc = pv
            else:
                corr = jnp.exp(m - m_new)
                l = corr * l + jnp.sum(p, axis=-1, keepdims=True)
                acc = corr * acc + pv
            m = m_new
        o_ref[0, qi * tq:(qi + 1) * tq, :] = (acc / l).astype(o_ref.dtype)


def _cast_slab_rows(w2d, n_steps):
    rows = w2d.shape[0] // n_steps
    assert rows * n_steps == w2d.shape[0] and rows % (2 * V7X_SUBLANES) == 0
    return rows


def _attention(q, k, v, w_cast, *, tq):
    n_batch, n_heads, seq, dk = q.shape
    dv = v.shape[3]
    w_rows = _cast_slab_rows(w_cast, n_batch * n_heads)
    w_spec = pl.BlockSpec((w_rows, w_cast.shape[1]), lambda b, h: (b * n_heads + h, 0))
    kern = functools.partial(_attn_kernel, tq=tq)
    return pl.pallas_call(
        kern,
        grid=(n_batch, n_heads),
        in_specs=[
            pl.BlockSpec((1, 1, seq, dk), lambda b, h: (b, h, 0, 0)),
            pl.BlockSpec((1, 1, seq, dk), lambda b, h: (b, h, 0, 0)),
            pl.BlockSpec((1, 1, seq, dv), lambda b, h: (b, h, 0, 0)),
            w_spec,
        ],
        out_specs=[pl.BlockSpec((1, seq, dv), lambda b, h: (b, 0, h)), w_spec],
        out_shape=[jax.ShapeDtypeStruct((n_batch, seq, n_heads * dv), BF16),
                   jax.ShapeDtypeStruct(w_cast.shape, BF16)],
        compiler_params=pltpu.CompilerParams(
            dimension_semantics=("parallel", "parallel"), vmem_limit_bytes=V7X_VMEM_LIMIT),
        name="mla_attention",
    )(q, k, v, w_cast)


def _oproj_router_kernel(o_ref, x_ref, wo_ref, g_ref, b_ref, wrh_ref, wrl_ref, w_ref,
                         y_ref, info_ref, wb_ref):
    wb_ref[...] = w_ref[...].astype(BF16)
    half = info_ref.shape[0] // 2
    for r in range(2):
        rows = pl.ds(r * half, half)
        y, info = _oproj_router_rows(o_ref[0, rows, :], x_ref[0, rows, :], wo_ref, g_ref, b_ref,
                                     wrh_ref, wrl_ref)
        for c in range(V7X_SUBLANES):
            y_ref[pl.ds(r * half * V7X_SUBLANES + c, half, stride=V7X_SUBLANES), :] = (
                y[:, c * V7X_LANES:(c + 1) * V7X_LANES])
        info_ref[rows, :] = info


def _oproj_router_rows(o, x, wo_ref, g_ref, b_ref, wrh_ref, wrl_ref):
    mix = jnp.dot(o, wo_ref[...], preferred_element_type=F32)
    y = _layer_norm(ALPHA * x + mix, g_ref[...], b_ref[...])

    y_hi = y.astype(BF16)
    y_lo = (y - y_hi.astype(F32)).astype(BF16)
    logits = (jnp.dot(y_hi, wrh_ref[...], preferred_element_type=F32)
              + jnp.dot(y_lo, wrh_ref[...], preferred_element_type=F32)
              + jnp.dot(y_hi, wrl_ref[...], preferred_element_type=F32))
    lane = lax.broadcasted_iota(jnp.int32, logits.shape, 1)
    logits = jnp.where(lane < N_EXPERTS, logits, -jnp.inf)
    m1 = jnp.max(logits, axis=-1, keepdims=True)
    i1 = jnp.min(jnp.where(logits == m1, lane, V7X_LANES), axis=-1, keepdims=True)
    rest = jnp.where(lane == i1, -jnp.inf, logits)
    m2 = jnp.max(rest, axis=-1, keepdims=True)
    i2 = jnp.min(jnp.where(rest == m2, lane, V7X_LANES), axis=-1, keepdims=True)
    e2 = jnp.exp(m2 - m1)
    den = 1.0 + e2
    info = jnp.where(lane == 0, i1.astype(F32),
                     jnp.where(lane == 1, i2.astype(F32),
                               jnp.where(lane == 2, 1.0 / den,
                                         jnp.where(lane == 3, e2 / den, 0.0))))
    return y, info


def _oproj_router(o, x, wo, ln_g, ln_b, wr_hi, wr_lo, w_cast, *, ts):
    n_batch, seq, d_attn = o.shape
    d_model = x.shape[2]
    assert d_model == V7X_SUBLANES * V7X_LANES
    nst = seq // ts
    w_rows = _cast_slab_rows(w_cast, n_batch * nst)
    w_spec = pl.BlockSpec((w_rows, w_cast.shape[1]), lambda b, s: (b * nst + s, 0))
    return pl.pallas_call(
        _oproj_router_kernel,
        grid=(n_batch, nst),
        in_specs=[
            pl.BlockSpec((1, ts, d_attn), lambda b, s: (b, s, 0)),
            pl.BlockSpec((1, ts, d_model), lambda b, s: (b, s, 0)),
            _const_spec(wo.shape), _const_spec(ln_g.shape), _const_spec(ln_b.shape),
            _const_spec(wr_hi.shape), _const_spec(wr_lo.shape),
            w_spec,
        ],
        out_specs=[
            pl.BlockSpec((ts * V7X_SUBLANES, V7X_LANES), lambda b, s: (b * nst + s, 0)),
            pl.BlockSpec((ts, V7X_LANES), lambda b, s: (b * nst + s, 0)),
            w_spec,
        ],
        out_shape=[
            jax.ShapeDtypeStruct((n_batch * seq * V7X_SUBLANES, V7X_LANES), F32),
            jax.ShapeDtypeStruct((n_batch * seq, V7X_LANES), F32),
            jax.ShapeDtypeStruct(w_cast.shape, BF16),
        ],
        compiler_params=pltpu.CompilerParams(
            dimension_semantics=("parallel", "parallel"), vmem_limit_bytes=V7X_VMEM_LIMIT),
        name="oproj_router",
    )(o, x, wo, ln_g, ln_b, wr_hi, wr_lo, w_cast)


def _moe_kernel(te_ref, nu_ref, tok_ref, dst_ref, x_hbm, wg_ref, wu_ref, wo_ref, y_hbm,
                xs_ref, ob_ref, xb_ref, acc_ref, gsem, ssem, *, tm, nf):
    del te_ref
    s = pl.program_id(0)
    f = pl.program_id(1)
    gslot = s % 2
    cslot = 1 - gslot
    valid = (s >= 1) & (s <= nu_ref[0])
    sub, lanes = V7X_SUBLANES, V7X_LANES

    def wait_tile_rows(buf_ref, sem):
        slot_rows = tm * sub
        pltpu.make_async_copy(
            buf_ref.at[pl.ds(0, slot_rows)], buf_ref.at[pl.ds(slot_rows, slot_rows)], sem).wait()

    @pl.when((s == 0) & (f == 0))
    def _():
        ob_ref[...] = jnp.zeros_like(ob_ref)

    @pl.when((s >= 1) & (f == 0))
    def _():
        wait_tile_rows(xs_ref, gsem)

    for parity in range(2):
        @pl.when((f == 0) & (gslot == parity))
        def _(parity=parity):
            for row in range(tm):
                vmem_rows = pl.ds((parity * tm + row) * sub, sub)
                pltpu.make_async_copy(
                    x_hbm.at[tok_ref[0, 0, row]], xs_ref.at[vmem_rows], gsem).start()

        @pl.when((f == nf - 1) & (gslot == parity))
        def _(parity=parity):
            for row in range(tm):
                vmem_rows = pl.ds((parity * tm + row) * sub, sub)
                pltpu.make_async_copy(
                    ob_ref.at[vmem_rows], y_hbm.at[dst_ref[0, 0, row]], ssem).start()

    def gathered_rows_bf16():
        return jnp.concatenate(
            [xs_ref[pl.ds(cslot * tm * sub + c, tm, stride=sub), :].astype(BF16)
             for c in range(sub)], axis=-1)

    def store_result(res):
        for c in range(sub):
            ob_ref[pl.ds(cslot * tm * sub + c, tm, stride=sub), :] = (
                res[:, c * lanes:(c + 1) * lanes])

    for chunk in range(nf):
        @pl.when(valid & (f == chunk))
        def _(chunk=chunk):
            if chunk == 0:
                xb = gathered_rows_bf16()
                if nf > 1:
                    xb_ref[...] = xb
            else:
                xb = xb_ref[...]
            gate = jnp.dot(xb, wg_ref[0], preferred_element_type=F32)
            up = jnp.dot(xb, wu_ref[0], preferred_element_type=F32)
            h = (gate * jax.nn.sigmoid(gate) * up).astype(BF16)
            part = jnp.dot(h, wo_ref[0], preferred_element_type=F32)
            if nf == 1:
                store_result(part)
            elif chunk == 0:
                acc_ref[...] = part
            elif chunk < nf - 1:
                acc_ref[...] += part
            else:
                store_result(acc_ref[...] + part)

    @pl.when(f == nf - 1)
    def _():
        wait_tile_rows(ob_ref, ssem)

    @pl.when((s == pl.num_programs(0) - 1) & (f == nf - 1))
    def _():
        wait_tile_rows(xs_ref, gsem)


def _moe_experts(tile_expert, n_used, tok_idx, dst_idx, x_tiles, w_in, w_out, *, tm, fc):
    d_model = x_tiles.shape[1] * x_tiles.shape[2]
    edim = w_out.shape[1]
    nf = edim // fc
    n_tiles = tok_idx.shape[0]
    n_rows = n_tiles * tm
    n_steps = n_tiles + 2

    def expert_chunk(s, f, te, nu):
        tile = jnp.clip(s - 1, 0, nu[0] - 1)
        chunk = jnp.where(s > nu[0], nf - 1, jnp.where(s < 1, 0, f))
        return te[tile], chunk

    def w_in_map(half):
        def index(s, f, te, nu):
            e, chunk = expert_chunk(s, f, te, nu)
            return e, 0, chunk + half * nf
        return index

    def w_out_map(s, f, te, nu):
        e, chunk = expert_chunk(s, f, te, nu)
        return e, chunk, 0

    grid_spec = pltpu.PrefetchScalarGridSpec(
        num_scalar_prefetch=2,
        grid=(n_steps, nf),
        in_specs=[
            pl.BlockSpec((1, 1, tm), lambda s, f, te, nu: (jnp.minimum(s, n_tiles - 1), 0, 0),
                         memory_space=pltpu.SMEM),
            pl.BlockSpec((1, 1, tm), lambda s, f, te, nu: (s, 0, 0), memory_space=pltpu.SMEM),
            pl.BlockSpec(memory_space=pl.ANY),
            pl.BlockSpec((1, d_model, fc), w_in_map(0)),
            pl.BlockSpec((1, d_model, fc), w_in_map(1)),
            pl.BlockSpec((1, fc, d_model), w_out_map),
        ],
        out_specs=pl.BlockSpec(memory_space=pl.ANY),
        scratch_shapes=[
            pltpu.VMEM((2 * tm * V7X_SUBLANES, V7X_LANES), F32),
            pltpu.VMEM((2 * tm * V7X_SUBLANES, V7X_LANES), F32),
            pltpu.VMEM((tm, d_model), BF16), pltpu.VMEM((tm, d_model), F32),
            pltpu.SemaphoreType.DMA(()), pltpu.SemaphoreType.DMA(()),
        ],
    )
    warm = n_rows + jnp.arange(tm, dtype=jnp.int32)[None, :]
    dst_by_step = jnp.concatenate([warm, warm, dst_idx], axis=0)
    kern = functools.partial(_moe_kernel, tm=tm, nf=nf)
    return pl.pallas_call(
        kern,
        grid_spec=grid_spec,
        out_shape=jax.ShapeDtypeStruct((n_rows + tm, V7X_SUBLANES, V7X_LANES), F32),
        compiler_params=pltpu.CompilerParams(
            dimension_semantics=("arbitrary", "arbitrary"), vmem_limit_bytes=V7X_VMEM_LIMIT),
        name="moe_experts",
    )(tile_expert, n_used, tok_idx.reshape(n_tiles, 1, tm), dst_by_step.reshape(n_steps, 1, tm),
      x_tiles, w_in, w_in, w_out)


def _combine_kernel(x_ref, info_ref, y0_ref, y1_ref, g_ref, b_ref, o_ref):
    tc, d_model = o_ref.shape
    sub, lanes = V7X_SUBLANES, V7X_LANES
    info = info_ref[...]
    g0, g1 = info[:, 2:3], info[:, 3:4]
    pieces = []
    for c in range(sub):
        rows = pl.ds(c, tc, stride=sub)
        pieces.append(ALPHA * x_ref[rows, :] + (y0_ref[rows, :] * g0 + y1_ref[rows, :] * g1))
    mu = jnp.sum(sum(pieces), axis=-1, keepdims=True) / d_model
    cen = [p - mu for p in pieces]
    var = jnp.sum(sum(c * c for c in cen), axis=-1, keepdims=True) / d_model
    rstd = lax.rsqrt(var + LN_EPS)
    for c in range(sub):
        cols = slice(c * lanes, (c + 1) * lanes)
        o_ref[:, cols] = cen[c] * rstd * g_ref[:, cols] + b_ref[:, cols]


def _combine(x_tiles, info, y_tiles, ln_g, ln_b, *, tc):
    n_tok = info.shape[0]
    d_model = V7X_SUBLANES * V7X_LANES
    nt = n_tok // tc
    tile_block = (tc * V7X_SUBLANES, V7X_LANES)
    return pl.pallas_call(
        _combine_kernel,
        grid=(nt,),
        in_specs=[
            pl.BlockSpec(tile_block, lambda i: (i, 0)),
            pl.BlockSpec((tc, V7X_LANES), lambda i: (i, 0)),
            pl.BlockSpec(tile_block, lambda i: (i, 0)),
            pl.BlockSpec(tile_block, lambda i: (i + nt, 0)),
            _const_spec(ln_g.shape), _const_spec(ln_b.shape),
        ],
        out_specs=pl.BlockSpec((tc, d_model), lambda i: (i, 0)),
        out_shape=jax.ShapeDtypeStruct((n_tok, d_model), F32),
        compiler_params=pltpu.CompilerParams(
            dimension_semantics=("parallel",), vmem_limit_bytes=V7X_VMEM_LIMIT),
        name="moe_combine",
    )(x_tiles, info, y_tiles, y_tiles, ln_g, ln_b)


def _s5_params(lam_re, lam_im, log_step, b_re, b_im, c_re, c_im):
    n_groups, n_state = lam_re.shape
    dt = jnp.exp(log_step)[:, None]
    mag = jnp.exp(lam_re * dt)
    lb_re = mag * jnp.cos(lam_im * dt)
    lb_im = mag * jnp.sin(lam_im * dt)
    den = lam_re * lam_re + lam_im * lam_im
    f_re = ((lb_re - 1.0) * lam_re + lb_im * lam_im) / den
    f_im = (lb_im * lam_re - (lb_re - 1.0) * lam_im) / den
    bb_re = f_re[..., None] * b_re - f_im[..., None] * b_im
    bb_im = f_re[..., None] * b_im + f_im[..., None] * b_re
    gpb = S5_BLOCK_GROUPS
    n_blocks = n_groups // gpb
    eye = jnp.eye(gpb, dtype=F32)

    def in_blocks(w):
        w = w.transpose(0, 2, 1).reshape(n_blocks, gpb, S5_GROUP, n_state)
        return jnp.einsum('kgcp,gh->kgchp', w, eye).reshape(n_blocks, gpb * S5_GROUP, gpb * n_state)

    def out_blocks(w):
        w = w.reshape(n_blocks, gpb, S5_GROUP, n_state)
        return jnp.einsum('kgcp,gh->kgphc', w, eye).reshape(n_blocks, gpb * n_state, gpb * S5_GROUP)

    bb = jnp.concatenate([in_blocks(bb_re), in_blocks(bb_im)], axis=-1).astype(BF16)
    cc = jnp.concatenate([out_blocks(c_re), out_blocks(-c_im)], axis=1).astype(BF16)
    return bb, cc, lb_re.reshape(1, -1), lb_im.reshape(1, -1)


def _rope_lane_tables(seq):
    pos = jnp.arange(seq, dtype=F32)
    inv_freq = ROPE_THETA ** (-jnp.arange(0, QK_ROPE, 2, dtype=F32) / QK_ROPE)
    ang = pos[:, None] * inv_freq[None, :]
    cos, sin = jnp.cos(ang), jnp.sin(ang)
    pad = jnp.zeros((seq, V7X_LANES - QK_ROPE), F32)
    return (jnp.concatenate([cos, cos, pad], axis=-1),
            jnp.concatenate([-sin, sin, pad], axis=-1))


def _rope_weight_blocks(w_rope):
    half = QK_ROPE // 2
    t1, t2 = w_rope[:, :half], w_rope[:, half:]
    pad = jnp.zeros((w_rope.shape[0], V7X_LANES - QK_ROPE), w_rope.dtype)
    return jnp.concatenate([t1, t2, pad, t2, t1, pad], axis=-1)


def _routing(info, *, tm):
    n_tok = info.shape[0]
    n_assign = n_tok * TOP_K
    flat_e = info[:, :TOP_K].astype(jnp.int32).reshape(-1)
    experts = jnp.arange(N_EXPERTS, dtype=jnp.int32)
    counts = jnp.sum((flat_e[:, None] == experts[None, :]).astype(jnp.int32), axis=0)
    pad = (-counts) % tm
    filler = jnp.arange(N_EXPERTS * tm, dtype=jnp.int32)
    filler_key = jnp.where(filler % tm < pad[filler // tm], filler // tm, N_EXPERTS)
    keys = jnp.concatenate([flat_e, filler_key])
    id_bits = int(keys.shape[0] - 1).bit_length()
    assert (N_EXPERTS + 1) << id_bits < 2 ** 31
    ids = jnp.arange(keys.shape[0], dtype=jnp.int32)
    packed = jnp.sort(jnp.left_shift(keys, id_bits) | ids)
    sorted_keys = jnp.right_shift(packed, id_bits)
    order = packed & ((1 << id_bits) - 1)
    is_real = order < n_assign
    tok_idx = jnp.where(is_real, order // TOP_K, 0)
    dst_idx = jnp.where(is_real, (order % TOP_K) * n_tok + order // TOP_K, order)
    tile_key = sorted_keys[::tm]
    tile_expert = jnp.minimum(tile_key, N_EXPERTS - 1)
    n_used = jnp.sum((tile_key < N_EXPERTS).astype(jnp.int32)).reshape(1)
    return tok_idx.reshape(-1, tm), dst_idx.reshape(-1, tm), tile_expert, n_used


def kernel(x, s5_lam_re, s5_lam_im, s5_log_step, s5_b_re, s5_b_im, s5_c_re, s5_c_im, s5_d, s5_w_glu,
           mla_q_w_a, mla_q_norm, mla_q_w_b, mla_o_w, kv_w_a, kv_norm, kv_w_b, ffn_w_in, ffn_w_out,
           moe_router, moe_w_in, moe_w_out, ln_g, ln_b):
    n_batch, seq, d_model = x.shape
    n_tok = n_batch * seq
    row = lambda v: v.reshape(1, -1).astype(F32)

    bb, cc, a_re, a_im = _s5_params(s5_lam_re[0], s5_lam_im[0], s5_log_step[0], s5_b_re[0], s5_b_im[0],
                                    s5_c_re[0], s5_c_im[0])
    n_time = min(S5_TIME_CHUNK, seq)
    x1 = _s5_layer(x, bb, cc, a_re, a_im, row(s5_d[0]), s5_w_glu[0].astype(BF16),
                   row(ln_g[0, 0]), row(ln_b[0, 0]), n_time=n_time)
    x2 = _dense_ffn(x1, ffn_w_in[0].astype(BF16), ffn_w_out[0].astype(BF16),
                    row(ln_g[0, 1]), row(ln_b[0, 1]), n_batch=n_batch, n_time=n_time)

    kv_lora = kv_norm.shape[0]
    wkva = jnp.concatenate([kv_w_a[:, :kv_lora], _rope_weight_blocks(kv_w_a[:, kv_lora:])],
                           axis=-1).astype(BF16)
    wq = mla_q_w_b[0].reshape(-1, MLA_HEADS, QK_NOPE + QK_ROPE)
    wqb = jnp.concatenate(
        [jnp.concatenate([wq[:, h, :QK_NOPE], _rope_weight_blocks(wq[:, h, QK_NOPE:])], axis=-1)
         for h in range(MLA_HEADS)], axis=-1).astype(BF16)
    cosx, sinx = _rope_lane_tables(seq)
    ts = min(ROW_TILE, seq)
    q, k, v = _qkv_proj(x2, wkva, row(kv_norm), kv_w_b.astype(BF16), mla_q_w_a[0].astype(BF16),
                        row(mla_q_norm[0]), wqb, cosx, sinx, ts=ts)
    w_in, w_out = moe_w_in[0], moe_w_out[0]
    o, w_in_bf = _attention(q, k, v, w_in.reshape(-1, w_in.shape[2]), tq=ts)
    wr = jnp.pad(moe_router[0], ((0, 0), (0, V7X_LANES - N_EXPERTS)))
    wr_hi = wr.astype(BF16)
    wr_lo = (wr - wr_hi.astype(F32)).astype(BF16)
    x3, info, w_out_bf = _oproj_router(o, x2, mla_o_w[0].astype(BF16), row(ln_g[1, 0]),
                                       row(ln_b[1, 0]), wr_hi, wr_lo,
                                       w_out.reshape(-1, w_out.shape[2]), ts=ts)

    tok_idx, dst_idx, tile_expert, n_used = _routing(info, tm=ROW_TILE)
    ys = _moe_experts(tile_expert, n_used, tok_idx, dst_idx,
                      x3.reshape(n_tok, V7X_SUBLANES, V7X_LANES), w_in_bf.reshape(w_in.shape),
                      w_out_bf.reshape(w_out.shape), tm=ROW_TILE, fc=w_out.shape[1] // 2)
    out = _combine(x3, info, ys.reshape(-1, V7X_LANES), row(ln_g[1, 1]), row(ln_b[1, 1]),
                   tc=min(2 * ROW_TILE, n_tok))
    return out.reshape(n_batch, seq, d_model)
```

```python
import functools
import math

import jax
import jax.numpy as jnp
from jax import lax
from jax.experimental import pallas as pl
from jax.experimental.pallas import tpu as pltpu

F32 = jnp.float32
BF16 = jnp.bfloat16

V7X_LANES = 128
V7X_SUBLANES = 8
V7X_MXU_WIDTH = 256
V7X_VMEM_LIMIT = 56 * 1024 * 1024

DEPTH = 2
ALPHA = (2.0 * DEPTH) ** 0.25
LN_EPS = 1e-5
RMS_EPS = 1e-6
ROPE_THETA = 10000.0

S5_GROUP = 16
S5_STATE = 64
S5_BLOCK_GROUPS = 16

MLA_HEADS = 8
QK_NOPE = 128
QK_ROPE = 64
V_DIM = 128
N_EXPERTS = 8
TOP_K = 2

ROW_TILE = 512
S5_TIME_CHUNK = 32


def _const_spec(shape):
    zeros = (0,) * len(shape)
    return pl.BlockSpec(shape, lambda *_: zeros, pipeline_mode=pl.Buffered(1))


def _layer_norm(h, g, b):
    mu = jnp.mean(h, axis=-1, keepdims=True)
    c = h - mu
    var = jnp.mean(c * c, axis=-1, keepdims=True)
    return c * lax.rsqrt(var + LN_EPS) * g + b


def _gelu_tanh(y):
    return 0.5 * y * (1.0 + jnp.tanh(math.sqrt(2.0 / math.pi) * (y + 0.044715 * (y * y * y))))


def _s5_kernel(x_hbm, bb_ref, cc_ref, are_ref, aim_ref, d_ref, wglu_ref, g_ref, b_ref,
               o_ref, xin_ref, bu_ref, st_ref, sem, *, n_batch, n_time, n_blocks):
    half = bb_ref.shape[2] // 2
    cb = bb_ref.shape[1]
    slab = 512
    d_model = o_ref.shape[1]
    i = pl.program_id(0)
    slot = i % 2

    def chunk_copy(step, b, to_slot):
        return pltpu.make_async_copy(
            x_hbm.at[b, pl.ds(step * n_time, n_time), :], xin_ref.at[to_slot, :, b, :],
            sem.at[to_slot])

    @pl.when(i == 0)
    def _():
        st_ref[...] = jnp.zeros_like(st_ref)
        for b in range(n_batch):
            chunk_copy(0, b, 0).start()

    @pl.when(i + 1 < pl.num_programs(0))
    def _():
        for b in range(n_batch):
            chunk_copy(i + 1, b, 1 - slot).start()

    for b in range(n_batch):
        chunk_copy(i, b, slot).wait()

    x = xin_ref[slot].reshape(n_time * n_batch, d_model)
    xb = x.astype(BF16)
    for k in range(n_blocks):
        bu_ref[:, k * 2 * half:(k + 1) * 2 * half] = jnp.dot(
            xb[:, k * cb:(k + 1) * cb], bb_ref[k], preferred_element_type=F32)

    for k in range(n_blocks):
        for j in range(half // slab):
            re0 = k * 2 * half + j * slab
            im0 = re0 + half
            a0 = k * half + j * slab
            for bh in range(n_batch // V7X_SUBLANES):
                r0 = bh * V7X_SUBLANES
                ar = jnp.broadcast_to(are_ref[:, a0:a0 + slab], (V7X_SUBLANES, slab))
                ai = jnp.broadcast_to(aim_ref[:, a0:a0 + slab], (V7X_SUBLANES, slab))
                s_re = st_ref[r0:r0 + V7X_SUBLANES, re0:re0 + slab]
                s_im = st_ref[r0:r0 + V7X_SUBLANES, im0:im0 + slab]

                for t in range(n_time):
                    rows = pl.ds(t * n_batch + r0, V7X_SUBLANES)
                    b_re = bu_ref[rows, re0:re0 + slab]
                    b_im = bu_ref[rows, im0:im0 + slab]
                    s_re, s_im = (ar * s_re - ai * s_im + b_re, ar * s_im + ai * s_re + b_im)
                    bu_ref[rows, re0:re0 + slab] = s_re
                    bu_ref[rows, im0:im0 + slab] = s_im
                st_ref[r0:r0 + V7X_SUBLANES, re0:re0 + slab] = s_re
                st_ref[r0:r0 + V7X_SUBLANES, im0:im0 + slab] = s_im

    ys = []
    for k in range(n_blocks):
        s_blk = bu_ref[:, k * 2 * half:(k + 1) * 2 * half].astype(BF16)
        ys.append(jnp.dot(s_blk, cc_ref[k], preferred_element_type=F32))
    y = jnp.concatenate(ys, axis=-1) + d_ref[...] * x
    act = _gelu_tanh(y).astype(BF16)
    z = jnp.dot(act, wglu_ref[...], preferred_element_type=F32)
    mix = z[:, :d_model] * jax.nn.sigmoid(z[:, d_model:])
    o_ref[...] = _layer_norm(ALPHA * x + mix, g_ref[...], b_ref[...])


def _s5_layer(x, bb, cc, a_re, a_im, d_skip, w_glu, ln_g, ln_b, *, n_time):
    n_batch, seq, d_model = x.shape
    n_rows = n_batch * seq
    n_blocks = bb.shape[0]
    rows = n_time * n_batch
    state_w = n_blocks * bb.shape[2]
    kern = functools.partial(_s5_kernel, n_batch=n_batch, n_time=n_time, n_blocks=n_blocks)
    return pl.pallas_call(
        kern,
        grid=(n_rows // rows,),
        in_specs=[
            pl.BlockSpec(memory_space=pl.ANY),
            _const_spec(bb.shape), _const_spec(cc.shape),
            _const_spec(a_re.shape), _const_spec(a_im.shape), _const_spec(d_skip.shape),
            _const_spec(w_glu.shape), _const_spec(ln_g.shape), _const_spec(ln_b.shape),
        ],
        out_specs=pl.BlockSpec((rows, d_model), lambda i: (i, 0)),
        out_shape=jax.ShapeDtypeStruct((n_rows, d_model), F32),
        scratch_shapes=[
            pltpu.VMEM((2, n_time, n_batch, d_model), F32),
            pltpu.VMEM((rows, state_w), F32), pltpu.VMEM((n_batch, state_w), F32),
            pltpu.SemaphoreType.DMA((2,)),
        ],
        compiler_params=pltpu.CompilerParams(
            dimension_semantics=("arbitrary",), vmem_limit_bytes=V7X_VMEM_LIMIT),
        name="s5_mixer",
    )(x, bb, cc, a_re, a_im, d_skip, w_glu, ln_g, ln_b)


def _ffn_kernel(x_ref, wg_ref, wu_ref, wo_ref, g_ref, b_ref, o_hbm, obuf_ref, sem, *, fc, n_batch):
    i = pl.program_id(0)
    last = pl.num_programs(0) - 1
    slot = i % 2
    n_time = x_ref.shape[0] // n_batch

    def out_copy(step, b, from_slot):
        return pltpu.make_async_copy(
            obuf_ref.at[from_slot, :, b, :], o_hbm.at[b, pl.ds(step * n_time, n_time), :],
            sem.at[from_slot])

    @pl.when(i >= 2)
    def _():
        for b in range(n_batch):
            out_copy(i - 2, b, slot).wait()

    x = x_ref[...]
    xb = x.astype(BF16)
    acc = None
    for lo, hi in zip(fc[:-1], fc[1:]):
        gate = jnp.dot(xb, wg_ref[:, lo:hi], preferred_element_type=F32)
        up = jnp.dot(xb, wu_ref[:, lo:hi], preferred_element_type=F32)
        h = (gate * jax.nn.sigmoid(gate) * up).astype(BF16)
        part = jnp.dot(h, wo_ref[lo:hi, :], preferred_element_type=F32)
        acc = part if acc is None else acc + part
    out = _layer_norm(ALPHA * x + acc, g_ref[...], b_ref[...])
    obuf_ref[slot] = out.reshape(n_time, n_batch, out.shape[1])
    for b in range(n_batch):
        out_copy(i, b, slot).start()

    @pl.when((i == last) & (i >= 1))
    def _():
        for b in range(n_batch):
            out_copy(i - 1, b, 1 - slot).wait()

    @pl.when(i == last)
    def _():
        for b in range(n_batch):
            out_copy(i, b, slot).wait()


def _dense_ffn(x, w_in, w_out, ln_g, ln_b, *, n_batch, n_time):
    n_rows, d_model = x.shape
    ffn = w_out.shape[0]
    tm = n_time * n_batch
    n_mxu_tiles, rem = divmod(ffn, V7X_MXU_WIDTH)
    assert rem == 0
    fc = (0, (n_mxu_tiles + 1) // 2 * V7X_MXU_WIDTH, ffn)
    kern = functools.partial(_ffn_kernel, fc=fc, n_batch=n_batch)
    return pl.pallas_call(
        kern,
        grid=(n_rows // tm,),
        in_specs=[
            pl.BlockSpec((tm, d_model), lambda i: (i, 0)),
            pl.BlockSpec((d_model, ffn), lambda i: (0, 0), pipeline_mode=pl.Buffered(1)),
            pl.BlockSpec((d_model, ffn), lambda i: (0, 1), pipeline_mode=pl.Buffered(1)),
            _const_spec(w_out.shape), _const_spec(ln_g.shape), _const_spec(ln_b.shape),
        ],
        out_specs=pl.BlockSpec(memory_space=pl.ANY),
        out_shape=jax.ShapeDtypeStruct((n_batch, n_rows // n_batch, d_model), F32),
        scratch_shapes=[pltpu.VMEM((2, n_time, n_batch, d_model), F32),
                        pltpu.SemaphoreType.DMA((2,))],
        compiler_params=pltpu.CompilerParams(
            dimension_semantics=("arbitrary",), vmem_limit_bytes=V7X_VMEM_LIMIT),
        name="dense_ffn",
    )(x, w_in, w_in, w_out, ln_g, ln_b)


def _qkv_kernel(x_ref, wkva_ref, kvn_ref, wkvb_ref, wqa_ref, qn_ref, wqb_ref, cos_ref, sin_ref,
                q_ref, k_ref, v_ref, *, scale):
    xb = x_ref[0].astype(BF16)
    cosx = cos_ref[...]
    sinx = sin_ref[...]

    kva = jnp.dot(xb, wkva_ref[...], preferred_element_type=F32)
    ckv = kva[:, :V7X_LANES]
    ckv = ckv * lax.rsqrt(jnp.mean(ckv * ckv, axis=-1, keepdims=True) + RMS_EPS) * kvn_ref[...]
    k_rope = (kva[:, V7X_LANES:2 * V7X_LANES] * cosx
              + kva[:, 2 * V7X_LANES:3 * V7X_LANES] * sinx).astype(BF16)
    kv = jnp.dot(ckv.astype(BF16), wkvb_ref[...], preferred_element_type=F32)

    cq = jnp.dot(xb, wqa_ref[...], preferred_element_type=F32)
    cq = cq * lax.rsqrt(jnp.mean(cq * cq, axis=-1, keepdims=True) + RMS_EPS) * qn_ref[...]
    q = jnp.dot(cq.astype(BF16), wqb_ref[...], preferred_element_type=F32)

    for h in range(MLA_HEADS):
        kb = h * (QK_NOPE + V_DIM)
        k_ref[0, h, :, :QK_NOPE] = kv[:, kb:kb + QK_NOPE].astype(BF16)
        k_ref[0, h, :, QK_NOPE:] = k_rope
        v_ref[0, h] = kv[:, kb + QK_NOPE:kb + QK_NOPE + V_DIM].astype(BF16)
        qb = h * 3 * V7X_LANES
        q_ref[0, h, :, :QK_NOPE] = (q[:, qb:qb + QK_NOPE] * scale).astype(BF16)
        q_rope = (q[:, qb + V7X_LANES:qb + 2 * V7X_LANES] * cosx
                  + q[:, qb + 2 * V7X_LANES:qb + 3 * V7X_LANES] * sinx)
        q_ref[0, h, :, QK_NOPE:] = (q_rope * scale).astype(BF16)


def _qkv_proj(x, wkva, kvn, wkvb, wqa, qn, wqb, cosx, sinx, *, ts):
    n_batch, seq, d_model = x.shape
    scale = 1.0 / math.sqrt(QK_NOPE + QK_ROPE)
    dk = QK_NOPE + V7X_LANES
    kern = functools.partial(_qkv_kernel, scale=scale)
    return pl.pallas_call(
        kern,
        grid=(n_batch, seq // ts),
        in_specs=[
            pl.BlockSpec((1, ts, d_model), lambda b, s: (b, s, 0)),
            _const_spec(wkva.shape), _const_spec(kvn.shape), _const_spec(wkvb.shape),
            _const_spec(wqa.shape), _const_spec(qn.shape), _const_spec(wqb.shape),
            pl.BlockSpec((ts, V7X_LANES), lambda b, s: (s, 0)),
            pl.BlockSpec((ts, V7X_LANES), lambda b, s: (s, 0)),
        ],
        out_specs=[
            pl.BlockSpec((1, MLA_HEADS, ts, dk), lambda b, s: (b, 0, s, 0)),
            pl.BlockSpec((1, MLA_HEADS, ts, dk), lambda b, s: (b, 0, s, 0)),
            pl.BlockSpec((1, MLA_HEADS, ts, V_DIM), lambda b, s: (b, 0, s, 0)),
        ],
        out_shape=[
            jax.ShapeDtypeStruct((n_batch, MLA_HEADS, seq, dk), BF16),
            jax.ShapeDtypeStruct((n_batch, MLA_HEADS, seq, dk), BF16),
            jax.ShapeDtypeStruct((n_batch, MLA_HEADS, seq, V_DIM), BF16),
        ],
        compiler_params=pltpu.CompilerParams(
            dimension_semantics=("parallel", "parallel"), vmem_limit_bytes=V7X_VMEM_LIMIT),
        name="qkv_proj",
    )(x, wkva, kvn, wkvb, wqa, qn, wqb, cosx, sinx)


def _attn_kernel(q_ref, k_ref, v_ref, w_ref, o_ref, wb_ref, *, tq):
    wb_ref[...] = w_ref[...].astype(BF16)
    seq = q_ref.shape[2]
    n_tiles = seq // tq
    row = lax.broadcasted_iota(jnp.int32, (tq, tq), 0)
    col = lax.broadcasted_iota(jnp.int32, (tq, tq), 1)
    causal = col <= row
    for qi in range(n_tiles):
        q = q_ref[0, 0, qi * tq:(qi + 1) * tq, :]
        m = l = acc = None
        for kj in range(qi + 1):
            k = k_ref[0, 0, kj * tq:(kj + 1) * tq, :]
            v = v_ref[0, 0, kj * tq:(kj + 1) * tq, :]
            s = lax.dot_general(q, k, (((1,), (1,)), ((), ())), preferred_element_type=F32)
            if kj == qi:
                s = jnp.where(causal, s, -jnp.inf)
            m_new = jnp.max(s, axis=-1, keepdims=True)
            if kj > 0:
                m_new = jnp.maximum(m, m_new)
            p = jnp.exp(s - m_new)
            pv = jnp.dot(p.astype(BF16), v, preferred_element_type=F32)
            if kj == 0:
                l = jnp.sum(p, axis=-1, keepdims=True)
                acc = pv
            else:
                corr = jnp.exp(m - m_new)
                l = corr * l + jnp.sum(p, axis=-1, keepdims=True)
                acc = corr * acc + pv
            m = m_new
        o_ref[0, qi * tq:(qi + 1) * tq, :] = (acc / l).astype(o_ref.dtype)


def _cast_slab_rows(w2d, n_steps):
    rows = w2d.shape[0] // n_steps
    assert rows * n_steps == w2d.shape[0] and rows % (2 * V7X_SUBLANES) == 0
    return rows


def _attention(q, k, v, w_cast, *, tq):
    n_batch, n_heads, seq, dk = q.shape
    dv = v.shape[3]
    w_rows = _cast_slab_rows(w_cast, n_batch * n_heads)
    w_spec = pl.BlockSpec((w_rows, w_cast.shape[1]), lambda b, h: (b * n_heads + h, 0))
    kern = functools.partial(_attn_kernel, tq=tq)
    return pl.pallas_call(
        kern,
        grid=(n_batch, n_heads),
        in_specs=[
            pl.BlockSpec((1, 1, seq, dk), lambda b, h: (b, h, 0, 0)),
            pl.BlockSpec((1, 1, seq, dk), lambda b, h: (b, h, 0, 0)),
            pl.BlockSpec((1, 1, seq, dv), lambda b, h: (b, h, 0, 0)),
            w_spec,
        ],
        out_specs=[pl.BlockSpec((1, seq, dv), lambda b, h: (b, 0, h)), w_spec],
        out_shape=[jax.ShapeDtypeStruct((n_batch, seq, n_heads * dv), BF16),
                   jax.ShapeDtypeStruct(w_cast.shape, BF16)],
        compiler_params=pltpu.CompilerParams(
            dimension_semantics=("parallel", "parallel"), vmem_limit_bytes=V7X_VMEM_LIMIT),
        name="mla_attention",
    )(q, k, v, w_cast)


def _oproj_router_kernel(o_ref, x_ref, wo_ref, g_ref, b_ref, wrh_ref, wrl_ref, w_ref,
                         y_ref, info_ref, wb_ref):
    wb_ref[...] = w_ref[...].astype(BF16)
    half = info_ref.shape[0] // 2
    for r in range(2):
        rows = pl.ds(r * half, half)
        y, info = _oproj_router_rows(o_ref[0, rows, :], x_ref[0, rows, :], wo_ref, g_ref, b_ref,
                                     wrh_ref, wrl_ref)
        for c in range(V7X_SUBLANES):
            y_ref[pl.ds(r * half * V7X_SUBLANES + c, half, stride=V7X_SUBLANES), :] = (
                y[:, c * V7X_LANES:(c + 1) * V7X_LANES])
        info_ref[rows, :] = info


def _oproj_router_rows(o, x, wo_ref, g_ref, b_ref, wrh_ref, wrl_ref):
    mix = jnp.dot(o, wo_ref[...], preferred_element_type=F32)
    y = _layer_norm(ALPHA * x + mix, g_ref[...], b_ref[...])

    y_hi = y.astype(BF16)
    y_lo = (y - y_hi.astype(F32)).astype(BF16)
    logits = (jnp.dot(y_hi, wrh_ref[...], preferred_element_type=F32)
              + jnp.dot(y_lo, wrh_ref[...], preferred_element_type=F32)
              + jnp.dot(y_hi, wrl_ref[...], preferred_element_type=F32))
    lane = lax.broadcasted_iota(jnp.int32, logits.shape, 1)
    logits = jnp.where(lane < N_EXPERTS, logits, -jnp.inf)
    m1 = jnp.max(logits, axis=-1, keepdims=True)
    i1 = jnp.min(jnp.where(logits == m1, lane, V7X_LANES), axis=-1, keepdims=True)
    rest = jnp.where(lane == i1, -jnp.inf, logits)
    m2 = jnp.max(rest, axis=-1, keepdims=True)
    i2 = jnp.min(jnp.where(rest == m2, lane, V7X_LANES), axis=-1, keepdims=True)
    e2 = jnp.exp(m2 - m1)
    den = 1.0 + e2
    info = jnp.where(lane == 0, i1.astype(F32),
                     jnp.where(lane == 1, i2.astype(F32),
                               jnp.where(lane == 2, 1.0 / den,
                                         jnp.where(lane == 3, e2 / den, 0.0))))
    return y, info


def _oproj_router(o, x, wo, ln_g, ln_b, wr_hi, wr_lo, w_cast, *, ts):
    n_batch, seq, d_attn = o.shape
    d_model = x.shape[2]
    assert d_model == V7X_SUBLANES * V7X_LANES
    nst = seq // ts
    w_rows = _cast_slab_rows(w_cast, n_batch * nst)
    w_spec = pl.BlockSpec((w_rows, w_cast.shape[1]), lambda b, s: (b * nst + s, 0))
    return pl.pallas_call(
        _oproj_router_kernel,
        grid=(n_batch, nst),
        in_specs=[
            pl.BlockSpec((1, ts, d_attn), lambda b, s: (b, s, 0)),
            pl.BlockSpec((1, ts, d_model), lambda b, s: (b, s, 0)),
            _const_spec(wo.shape), _const_spec(ln_g.shape), _const_spec(ln_b.shape),
            _const_spec(wr_hi.shape), _const_spec(wr_lo.shape),
            w_spec,
        ],
        out_specs=[
            pl.BlockSpec((ts * V7X_SUBLANES, V7X_LANES), lambda b, s: (b * nst + s, 0)),
            pl.BlockSpec((ts, V7X_LANES), lambda b, s: (b * nst + s, 0)),
            w_spec,
        ],
        out_shape=[
            jax.ShapeDtypeStruct((n_batch * seq * V7X_SUBLANES, V7X_LANES), F32),
            jax.ShapeDtypeStruct((n_batch * seq, V7X_LANES), F32),
            jax.ShapeDtypeStruct(w_cast.shape, BF16),
        ],
        compiler_params=pltpu.CompilerParams(
            dimension_semantics=("parallel", "parallel"), vmem_limit_bytes=V7X_VMEM_LIMIT),
        name="oproj_router",
    )(o, x, wo, ln_g, ln_b, wr_hi, wr_lo, w_cast)


def _moe_kernel(te_ref, nu_ref, tok_ref, dst_ref, x_hbm, wg_ref, wu_ref, wo_ref, y_hbm,
                xs_ref, ob_ref, xb_ref, acc_ref, gsem, ssem, *, tm, nf):
    del te_ref
    s = pl.program_id(0)
    f = pl.program_id(1)
    gslot = s % 2
    cslot = 1 - gslot
    valid = (s >= 1) & (s <= nu_ref[0])
    sub, lanes = V7X_SUBLANES, V7X_LANES

    def wait_tile_rows(buf_ref, sem):
        slot_rows = tm * sub
        pltpu.make_async_copy(
            buf_ref.at[pl.ds(0, slot_rows)], buf_ref.at[pl.ds(slot_rows, slot_rows)], sem).wait()

    @pl.when((s == 0) & (f == 0))
    def _():
        ob_ref[...] = jnp.zeros_like(ob_ref)

    @pl.when((s >= 1) & (f == 0))
    def _():
        wait_tile_rows(xs_ref, gsem)

    for parity in range(2):
        @pl.when((f == 0) & (gslot == parity))
        def _(parity=parity):
            for row in range(tm):
                vmem_rows = pl.ds((parity * tm + row) * sub, sub)
                pltpu.make_async_copy(
                    x_hbm.at[tok_ref[0, 0, row]], xs_ref.at[vmem_rows], gsem).start()

        @pl.when((f == nf - 1) & (gslot == parity))
        def _(parity=parity):
            for row in range(tm):
                vmem_rows = pl.ds((parity * tm + row) * sub, sub)
                pltpu.make_async_copy(
                    ob_ref.at[vmem_rows], y_hbm.at[dst_ref[0, 0, row]], ssem).start()

    def gathered_rows_bf16():
        return jnp.concatenate(
            [xs_ref[pl.ds(cslot * tm * sub + c, tm, stride=sub), :].astype(BF16)
             for c in range(sub)], axis=-1)

    def store_result(res):
        for c in range(sub):
            ob_ref[pl.ds(cslot * tm * sub + c, tm, stride=sub), :] = (
                res[:, c * lanes:(c + 1) * lanes])

    for chunk in range(nf):
        @pl.when(valid & (f == chunk))
        def _(chunk=chunk):
            if chunk == 0:
                xb = gathered_rows_bf16()
                if nf > 1:
                    xb_ref[...] = xb
            else:
                xb = xb_ref[...]
            gate = jnp.dot(xb, wg_ref[0], preferred_element_type=F32)
            up = jnp.dot(xb, wu_ref[0], preferred_element_type=F32)
            h = (gate * jax.nn.sigmoid(gate) * up).astype(BF16)
            part = jnp.dot(h, wo_ref[0], preferred_element_type=F32)
            if nf == 1:
                store_result(part)
            elif chunk == 0:
                acc_ref[...] = part
            elif chunk < nf - 1:
                acc_ref[...] += part
            else:
                store_result(acc_ref[...] + part)

    @pl.when(f == nf - 1)
    def _():
        wait_tile_rows(ob_ref, ssem)

    @pl.when((s == pl.num_programs(0) - 1) & (f == nf - 1))
    def _():
        wait_tile_rows(xs_ref, gsem)


def _moe_experts(tile_expert, n_used, tok_idx, dst_idx, x_tiles, w_in, w_out, *, tm, fc):
    d_model = x_tiles.shape[1] * x_tiles.shape[2]
    edim = w_out.shape[1]
    nf = edim // fc
    n_tiles = tok_idx.shape[0]
    n_rows = n_tiles * tm
    n_steps = n_tiles + 2

    def expert_chunk(s, f, te, nu):
        tile = jnp.clip(s - 1, 0, nu[0] - 1)
        chunk = jnp.where(s > nu[0], nf - 1, jnp.where(s < 1, 0, f))
        return te[tile], chunk

    def w_in_map(half):
        def index(s, f, te, nu):
            e, chunk = expert_chunk(s, f, te, nu)
            return e, 0, chunk + half * nf
        return index

    def w_out_map(s, f, te, nu):
        e, chunk = expert_chunk(s, f, te, nu)
        return e, chunk, 0

    grid_spec = pltpu.PrefetchScalarGridSpec(
        num_scalar_prefetch=2,
        grid=(n_steps, nf),
        in_specs=[
            pl.BlockSpec((1, 1, tm), lambda s, f, te, nu: (jnp.minimum(s, n_tiles - 1), 0, 0),
                         memory_space=pltpu.SMEM),
            pl.BlockSpec((1, 1, tm), lambda s, f, te, nu: (s, 0, 0), memory_space=pltpu.SMEM),
            pl.BlockSpec(memory_space=pl.ANY),
            pl.BlockSpec((1, d_model, fc), w_in_map(0)),
            pl.BlockSpec((1, d_model, fc), w_in_map(1)),
            pl.BlockSpec((1, fc, d_model), w_out_map),
        ],
        out_specs=pl.BlockSpec(memory_space=pl.ANY),
        scratch_shapes=[
            pltpu.VMEM((2 * tm * V7X_SUBLANES, V7X_LANES), F32),
            pltpu.VMEM((2 * tm * V7X_SUBLANES, V7X_LANES), F32),
            pltpu.VMEM((tm, d_model), BF16), pltpu.VMEM((tm, d_model), F32),
            pltpu.SemaphoreType.DMA(()), pltpu.SemaphoreType.DMA(()),
        ],
    )
    warm = n_rows + jnp.arange(tm, dtype=jnp.int32)[None, :]
    dst_by_step = jnp.concatenate([warm, warm, dst_idx], axis=0)
    kern = functools.partial(_moe_kernel, tm=tm, nf=nf)
    return pl.pallas_call(
        kern,
        grid_spec=grid_spec,
        out_shape=jax.ShapeDtypeStruct((n_rows + tm, V7X_SUBLANES, V7X_LANES), F32),
        compiler_params=pltpu.CompilerParams(
            dimension_semantics=("arbitrary", "arbitrary"), vmem_limit_bytes=V7X_VMEM_LIMIT),
        name="moe_experts",
    )(tile_expert, n_used, tok_idx.reshape(n_tiles, 1, tm), dst_by_step.reshape(n_steps, 1, tm),
      x_tiles, w_in, w_in, w_out)


def _combine_kernel(x_ref, info_ref, y0_ref, y1_ref, g_ref, b_ref, o_ref):
    tc, d_model = o_ref.shape
    sub, lanes = V7X_SUBLANES, V7X_LANES
    info = info_ref[...]
    g0, g1 = info[:, 2:3], info[:, 3:4]
    pieces = []
    for c in range(sub):
        rows = pl.ds(c, tc, stride=sub)
        pieces.append(ALPHA * x_ref[rows, :] + (y0_ref[rows, :] * g0 + y1_ref[rows, :] * g1))
    mu = jnp.sum(sum(pieces), axis=-1, keepdims=True) / d_model
    cen = [p - mu for p in pieces]
    var = jnp.sum(sum(c * c for c in cen), axis=-1, keepdims=True) / d_model
    rstd = lax.rsqrt(var + LN_EPS)
    for c in range(sub):
        cols = slice(c * lanes, (c + 1) * lanes)
        o_ref[:, cols] = cen[c] * rstd * g_ref[:, cols] + b_ref[:, cols]


def _combine(x_tiles, info, y_tiles, ln_g, ln_b, *, tc):
    n_tok = info.shape[0]
    d_model = V7X_SUBLANES * V7X_LANES
    nt = n_tok // tc
    tile_block = (tc * V7X_SUBLANES, V7X_LANES)
    return pl.pallas_call(
        _combine_kernel,
        grid=(nt,),
        in_specs=[
            pl.BlockSpec(tile_block, lambda i: (i, 0)),
            pl.BlockSpec((tc, V7X_LANES), lambda i: (i, 0)),
            pl.BlockSpec(tile_block, lambda i: (i, 0)),
            pl.BlockSpec(tile_block, lambda i: (i + nt, 0)),
            _const_spec(ln_g.shape), _const_spec(ln_b.shape),
        ],
        out_specs=pl.BlockSpec((tc, d_model), lambda i: (i, 0)),
        out_shape=jax.ShapeDtypeStruct((n_tok, d_model), F32),
        compiler_params=pltpu.CompilerParams(
            dimension_semantics=("parallel",), vmem_limit_bytes=V7X_VMEM_LIMIT),
        name="moe_combine",
    )(x_tiles, info, y_tiles, y_tiles, ln_g, ln_b)


def _s5_params(lam_re, lam_im, log_step, b_re, b_im, c_re, c_im):
    n_groups, n_state = lam_re.shape
    dt = jnp.exp(log_step)[:, None]
    mag = jnp.exp(lam_re * dt)
    lb_re = mag * jnp.cos(lam_im * dt)
    lb_im = mag * jnp.sin(lam_im * dt)
    den = lam_re * lam_re + lam_im * lam_im
    f_re = ((lb_re - 1.0) * lam_re + lb_im * lam_im) / den
    f_im = (lb_im * lam_re - (lb_re - 1.0) * lam_im) / den
    bb_re = f_re[..., None] * b_re - f_im[..., None] * b_im
    bb_im = f_re[..., None] * b_im + f_im[..., None] * b_re
    gpb = S5_BLOCK_GROUPS
    n_blocks = n_groups // gpb
    eye = jnp.eye(gpb, dtype=F32)

    def in_blocks(w):
        w = w.transpose(0, 2, 1).reshape(n_blocks, gpb, S5_GROUP, n_state)
        return jnp.einsum('kgcp,gh->kgchp', w, eye).reshape(n_blocks, gpb * S5_GROUP, gpb * n_state)

    def out_blocks(w):
        w = w.reshape(n_blocks, gpb, S5_GROUP, n_state)
        return jnp.einsum('kgcp,gh->kgphc', w, eye).reshape(n_blocks, gpb * n_state, gpb * S5_GROUP)

    bb = jnp.concatenate([in_blocks(bb_re), in_blocks(bb_im)], axis=-1).astype(BF16)
    cc = jnp.concatenate([out_blocks(c_re), out_blocks(-c_im)], axis=1).astype(BF16)
    return bb, cc, lb_re.reshape(1, -1), lb_im.reshape(1, -1)


def _rope_lane_tables(seq):
    pos = jnp.arange(seq, dtype=F32)
    inv_freq = ROPE_THETA ** (-jnp.arange(0, QK_ROPE, 2, dtype=F32) / QK_ROPE)
    ang = pos[:, None] * inv_freq[None, :]
    cos, sin = jnp.cos(ang), jnp.sin(ang)
    pad = jnp.zeros((seq, V7X_LANES - QK_ROPE), F32)
    return (jnp.concatenate([cos, cos, pad], axis=-1),
            jnp.concatenate([-sin, sin, pad], axis=-1))


def _rope_weight_blocks(w_rope):
    half = QK_ROPE // 2
    t1, t2 = w_rope[:, :half], w_rope[:, half:]
    pad = jnp.zeros((w_rope.shape[0], V7X_LANES - QK_ROPE), w_rope.dtype)
    return jnp.concatenate([t1, t2, pad, t2, t1, pad], axis=-1)


def _routing(info, *, tm):
    n_tok = info.shape[0]
    n_assign = n_tok * TOP_K
    flat_e = info[:, :TOP_K].astype(jnp.int32).reshape(-1)
    experts = jnp.arange(N_EXPERTS, dtype=jnp.int32)
    counts = jnp.sum((flat_e[:, None] == experts[None, :]).astype(jnp.int32), axis=0)
    pad = (-counts) % tm
    filler = jnp.arange(N_EXPERTS * tm, dtype=jnp.int32)
    filler_key = jnp.where(filler % tm < pad[filler // tm], filler // tm, N_EXPERTS)
    keys = jnp.concatenate([flat_e, filler_key])
    ids = jnp.arange(keys.shape[0], dtype=jnp.int32)
    sorted_keys, order = lax.sort((keys, ids), num_keys=1, is_stable=True)
    is_real = order < n_assign
    tok_idx = jnp.where(is_real, order // TOP_K, 0)
    dst_idx = jnp.where(is_real, (order % TOP_K) * n_tok + order // TOP_K, order)
    tile_key = sorted_keys[::tm]
    tile_expert = jnp.minimum(tile_key, N_EXPERTS - 1)
    n_used = jnp.sum((tile_key < N_EXPERTS).astype(jnp.int32)).reshape(1)
    return tok_idx.reshape(-1, tm), dst_idx.reshape(-1, tm), tile_expert, n_used


def kernel(x, s5_lam_re, s5_lam_im, s5_log_step, s5_b_re, s5_b_im, s5_c_re, s5_c_im, s5_d, s5_w_glu,
           mla_q_w_a, mla_q_norm, mla_q_w_b, mla_o_w, kv_w_a, kv_norm, kv_w_b, ffn_w_in, ffn_w_out,
           moe_router, moe_w_in, moe_w_out, ln_g, ln_b):
    n_batch, seq, d_model = x.shape
    n_tok = n_batch * seq
    row = lambda v: v.reshape(1, -1).astype(F32)

    bb, cc, a_re, a_im = _s5_params(s5_lam_re[0], s5_lam_im[0], s5_log_step[0], s5_b_re[0], s5_b_im[0],
                                    s5_c_re[0], s5_c_im[0])
    n_time = min(S5_TIME_CHUNK, seq)
    x1 = _s5_layer(x, bb, cc, a_re, a_im, row(s5_d[0]), s5_w_glu[0].astype(BF16),
                   row(ln_g[0, 0]), row(ln_b[0, 0]), n_time=n_time)
    x2 = _dense_ffn(x1, ffn_w_in[0].astype(BF16), ffn_w_out[0].astype(BF16),
                    row(ln_g[0, 1]), row(ln_b[0, 1]), n_batch=n_batch, n_time=n_time)

    kv_lora = kv_norm.shape[0]
    wkva = jnp.concatenate([kv_w_a[:, :kv_lora], _rope_weight_blocks(kv_w_a[:, kv_lora:])],
                           axis=-1).astype(BF16)
    wq = mla_q_w_b[0].reshape(-1, MLA_HEADS, QK_NOPE + QK_ROPE)
    wqb = jnp.concatenate(
        [jnp.concatenate([wq[:, h, :QK_NOPE], _rope_weight_blocks(wq[:, h, QK_NOPE:])], axis=-1)
         for h in range(MLA_HEADS)], axis=-1).astype(BF16)
    cosx, sinx = _rope_lane_tables(seq)
    ts = min(ROW_TILE, seq)
    q, k, v = _qkv_proj(x2, wkva, row(kv_norm), kv_w_b.astype(BF16), mla_q_w_a[0].astype(BF16),
                        row(mla_q_norm[0]), wqb, cosx, sinx, ts=ts)
    w_in, w_out = moe_w_in[0], moe_w_out[0]
    o, w_in_bf = _attention(q, k, v, w_in.reshape(-1, w_in.shape[2]), tq=ts)
    wr = jnp.pad(moe_router[0], ((0, 0), (0, V7X_LANES - N_EXPERTS)))
    wr_hi = wr.astype(BF16)
    wr_lo = (wr - wr_hi.astype(F32)).astype(BF16)
    x3, info, w_out_bf = _oproj_router(o, x2, mla_o_w[0].astype(BF16), row(ln_g[1, 0]),
                                       row(ln_b[1, 0]), wr_hi, wr_lo,
                                       w_out.reshape(-1, w_out.shape[2]), ts=ts)

    tok_idx, dst_idx, tile_expert, n_used = _routing(info, tm=ROW_TILE)
    ys = _moe_experts(tile_expert, n_used, tok_idx, dst_idx,
                      x3.reshape(n_tok, V7X_SUBLANES, V7X_LANES), w_in_bf.reshape(w_in.shape),
                      w_out_bf.reshape(w_out.shape), tm=ROW_TILE, fc=w_out.shape[1] // 2)
    out = _combine(x3, info, ys.reshape(-1, V7X_LANES), row(ln_g[1, 1]), row(ln_b[1, 1]),
                   tc=min(ROW_TILE, n_tok))
    return out.reshape(n_batch, seq, d_model)
```
